```python
import math
import numpy as np
import jax
import jax.numpy as jnp
from jax import lax

D_MODEL = 1024
BATCH = 16
SEQ = 4096
DEPTH = 1

GDN_HEADS = 4
GDN_HEAD_DIM = 128
GDN_WIDTH = GDN_HEADS * GDN_HEAD_DIM
GDN_CONV = 4
GDN_CHUNK = 64

NSA_Q_HEADS = 8
NSA_KV_HEADS = 2
NSA_HEAD_DIM = 64
NSA_WIDTH = NSA_Q_HEADS * NSA_HEAD_DIM
NSA_KV_WIDTH = NSA_KV_HEADS * NSA_HEAD_DIM
NSA_CMP_LEN = 32
NSA_CMP_STRIDE = 16
NSA_CMP_HIDDEN = 256
NSA_SEL_LEN = 64
NSA_TOPN = 16
NSA_WINDOW = 512
NSA_QBLOCK = 64
NSA_FORCE_SCORE = 1e4
NEG_INF = -1e30

MIX_WIDTH = GDN_WIDTH + NSA_WIDTH
IN_SIZES = (GDN_WIDTH, GDN_WIDTH, GDN_WIDTH, GDN_WIDTH, GDN_HEADS, GDN_HEADS,
            NSA_WIDTH, NSA_KV_WIDTH, NSA_KV_WIDTH, NSA_KV_WIDTH, NSA_KV_WIDTH,
            NSA_KV_WIDTH, NSA_KV_WIDTH, 3 * NSA_Q_HEADS)
IN_WIDTH = 4 * GDN_WIDTH + 2 * GDN_HEADS + NSA_WIDTH + 6 * NSA_KV_WIDTH + 3 * NSA_Q_HEADS

D_FF = 2816
LN_EPS = 1e-5
RMS_EPS = 1e-6
L2_EPS = 1e-6
DN_ALPHA = (2 * DEPTH) ** 0.25
DN_BETA = (8 * DEPTH) ** -0.25

kernel_name = 'hybrid_gdn_nsa_macaron_deepnorm'


def layer_norm(x, g, b):
    xf = x.astype(jnp.float32)
    mu = xf.mean(-1, keepdims=True)
    var = jnp.square(xf - mu).mean(-1, keepdims=True)
    return ((xf - mu) * lax.rsqrt(var + LN_EPS) * g.astype(jnp.float32) + b.astype(jnp.float32)).astype(x.dtype)


def swiglu(x, wg, wu, wd):
    return (jax.nn.silu(x @ wg) * (x @ wu)) @ wd


def causal_dwconv(x, w):
    k = w.shape[0]
    return lax.conv_general_dilated(x, w[:, None, :], window_strides=(1,), padding=[(k - 1, 0)],
                                    dimension_numbers=('NWC', 'WIO', 'NWC'),
                                    feature_group_count=x.shape[-1])


def l2norm(x):
    return x * lax.rsqrt(jnp.sum(x * x, -1, keepdims=True) + L2_EPS)


def masked_softmax(s, mask):
    p = jax.nn.softmax(jnp.where(mask, s, NEG_INF), axis=-1)
    return jnp.where(mask, p, 0.0)


def gated_delta_chunked(q, k, v, beta, g):
    B, T, H, Dk = q.shape
    Dv = v.shape[-1]
    C = GDN_CHUNK
    N = T // C

    def chunks(t):
        return jnp.moveaxis(t.reshape((B, N, C, H) + t.shape[3:]), 3, 1)

    q, k, v, beta = chunks(q), chunks(k), chunks(v), chunks(beta)
    gc = jnp.cumsum(chunks(g), axis=-1)
    k_beta = k * beta[..., None]
    v_beta = v * beta[..., None]
    causal = jnp.tril(jnp.ones((C, C), dtype=bool))
    strict = jnp.tril(jnp.ones((C, C), dtype=bool), -1)
    decay = jnp.exp(jnp.where(causal, gc[..., :, None] - gc[..., None, :], -jnp.inf))
    lmat = jnp.where(strict, jnp.einsum('bhncd,bhnsd->bhncs', k_beta, k) * decay, 0.0)
    amat = lmat + jnp.eye(C, dtype=jnp.float32)
    u = lax.linalg.triangular_solve(amat, v_beta, left_side=True, lower=True, unit_diagonal=True)
    w = lax.linalg.triangular_solve(amat, k_beta * jnp.exp(gc)[..., None], left_side=True,
                                    lower=True, unit_diagonal=True)
    qk = jnp.einsum('bhncd,bhnsd->bhncs', q, k) * decay

    def step(S, inp):
        q_i, k_i, u_i, w_i, gc_i, qk_i = inp
        v_new = u_i - jnp.einsum('bhck,bhkv->bhcv', w_i, S)
        o_i = (jnp.einsum('bhck,bhkv->bhcv', q_i * jnp.exp(gc_i)[..., None], S)
               + jnp.einsum('bhcs,bhsv->bhcv', qk_i, v_new))
        g_last = gc_i[..., -1:]
        S = (S * jnp.exp(g_last)[..., None]
             + jnp.einsum('bhck,bhcv->bhkv', k_i * jnp.exp(g_last - gc_i)[..., None], v_new))
        return S, o_i

    xs = tuple(jnp.moveaxis(t, 2, 0) for t in (q, k, u, w, gc, qk))
    S0 = jnp.zeros((B, H, Dk, Dv), jnp.float32)
    _, o = lax.scan(step, S0, xs)
    o = jnp.transpose(o, (1, 0, 3, 2, 4))
    return o.reshape(B, T, H, Dv)


def gdn_mixer(q, k, v, z, b, a, conv_w, a_log, dt_bias, norm_w):
    B, T, _ = q.shape
    f32 = jnp.float32
    qkv = jax.nn.silu(causal_dwconv(jnp.concatenate([q, k, v], -1), conv_w))
    q, k, v = jnp.split(qkv, 3, axis=-1)

    def heads(t):
        return t.reshape(B, T, GDN_HEADS, GDN_HEAD_DIM).astype(f32)

    q = l2norm(heads(q)) * (GDN_HEAD_DIM ** -0.5)
    k = l2norm(heads(k))
    v = heads(v)
    beta = jax.nn.sigmoid(b.astype(f32))
    g = -jnp.exp(a_log.astype(f32)) * jax.nn.softplus(a.astype(f32) + dt_bias.astype(f32))
    o = gated_delta_chunked(q, k, v, beta, g)
    o = (o * lax.rsqrt(jnp.mean(o * o, -1, keepdims=True) + RMS_EPS) * norm_w.astype(f32)
         * jax.nn.silu(heads(z)))
    return o.reshape(B, T, GDN_WIDTH).astype(q.dtype if False else z.dtype)


def nsa_mixer(q, kc, vc, ks, vs, kw, vw, gates, pos_k, pos_v, ck_w1, ck_w2, cv_w1, cv_w2):
    B, T, _ = q.shape
    out_dtype = q.dtype
    f32 = jnp.float32
    Hk, G, dh = NSA_KV_HEADS, NSA_Q_HEADS // NSA_KV_HEADS, NSA_HEAD_DIM
    l, d, ls, W, QB = NSA_CMP_LEN, NSA_CMP_STRIDE, NSA_SEL_LEN, NSA_WINDOW, NSA_QBLOCK
    scale = dh ** -0.5

    qh = q.astype(f32).reshape(B, T, Hk, G, dh).transpose(0, 2, 3, 1, 4)

    def kv_heads(t):
        return t.astype(f32).reshape(B, T, Hk, dh).transpose(0, 2, 1, 3)

    Nc = (T - l) // d + 1
    c_start = np.arange(Nc) * d
    blk_idx = c_start[:, None] + np.arange(l)[None, :]

    def compress(t, pos, w1, w2):
        blocks = kv_heads(t)[:, :, blk_idx] + pos.astype(f32)
        hid = jax.nn.gelu(blocks.reshape(B, Hk, Nc, l * dh) @ w1.astype(f32))
        return hid @ w2.astype(f32)

    k_cmp = compress(kc, pos_k, ck_w1, ck_w2)
    v_cmp = compress(vc, pos_v, cv_w1, cv_w2)
    c_end = jnp.asarray(c_start + l - 1)

    Ns = T // ls
    n_sel = min(NSA_TOPN, Ns)
    k_blocks = kv_heads(ks).reshape(B, Hk, Ns, ls, dh)
    v_blocks = kv_heads(vs).reshape(B, Hk, Ns, ls, dh)
    s_start = np.arange(Ns) * ls
    overlap = np.clip(np.minimum(c_start[:, None] + l, s_start[None, :] + ls)
                      - np.maximum(c_start[:, None], s_start[None, :]), 0, None) / l
    cmp_to_sel = jnp.asarray(overlap, f32)
    blk_ids = jnp.arange(Ns)
    gather_blocks = jax.vmap(jax.vmap(lambda blocks, ix: blocks[ix]))

    pad = ((0, 0), (0, 0), (W, 0), (0, 0))
    k_win = jnp.pad(kv_heads(kw), pad)
    v_win = jnp.pad(kv_heads(vw), pad)

    gate_h = jax.nn.sigmoid(gates.astype(f32)).reshape(B, T, Hk, G, 3).transpose(0, 2, 3, 1, 4)

    def block(s):
        t = s + jnp.arange(QB)
        q_b = lax.dynamic_slice_in_dim(qh, s, QB, axis=3) * scale
        valid_c = c_end[None, :] <= t[:, None]
        p_c = masked_softmax(jnp.einsum('bhgqd,bhnd->bhgqn', q_b, k_cmp), valid_c)
        o_c = jnp.einsum('bhgqn,bhnd->bhgqd', p_c, v_cmp)
        imp = jnp.einsum('bhgqn,nj->bhqj', p_c, cmp_to_sel)
        cur = t // ls
        visible = blk_ids[None, :] * ls <= t[:, None]
        forced = ((blk_ids[None, :] == 0) | (blk_ids[None, :] == cur[:, None])
                  | (blk_ids[None, :] == cur[:, None] - 1))
        imp = jnp.where(visible, jnp.where(forced, NSA_FORCE_SCORE, imp), -1.0)
        _, idx = lax.top_k(imp, n_sel)
        k_sel = gather_blocks(k_blocks, idx).reshape(B, Hk, QB, n_sel * ls, dh)
        v_sel = gather_blocks(v_blocks, idx).reshape(B, Hk, QB, n_sel * ls, dh)
        kpos = (idx[..., None] * ls + jnp.arange(ls)).reshape(B, Hk, QB, n_sel * ls)
        valid_s = (kpos <= t[:, None])[:, :, None]
        p_s = masked_softmax(jnp.einsum('bhgqd,bhqkd->bhgqk', q_b, k_sel), valid_s)
        o_s = jnp.einsum('bhgqk,bhqkd->bhgqd', p_s, v_sel)
        k_w = lax.dynamic_slice_in_dim(k_win, s, W + QB, axis=2)
        v_w = lax.dynamic_slice_in_dim(v_win, s, W + QB, axis=2)
        wpos = s - W + jnp.arange(W + QB)
        rel = t[:, None] - wpos[None, :]
        valid_w = (rel >= 0) & (rel < W) & (wpos[None, :] >= 0)
        p_w = masked_softmax(jnp.einsum('bhgqd,bhkd->bhgqk', q_b, k_w), valid_w)
        o_w = jnp.einsum('bhgqk,bhkd->bhgqd', p_w, v_w)
        g_b = lax.dynamic_slice_in_dim(gate_h, s, QB, axis=3)
        return g_b[..., 0:1] * o_c + g_b[..., 1:2] * o_s + g_b[..., 2:3] * o_w

    starts = jnp.arange(T // QB, dtype=jnp.int32) * QB
    out = lax.map(block, starts)
    out = out.transpose(1, 0, 4, 2, 3, 5).reshape(B, T, NSA_WIDTH)
    return out.astype(out_dtype)


def setup_inputs(seed: int = 0) -> dict:
    key = jax.random.key(seed)
    ks = jax.random.split(key, 26)
    f32 = jnp.float32
    L = DEPTH

    def nrm(k, shape, scale):
        return jax.random.normal(k, shape, f32) * scale

    dt = jnp.exp(jax.random.uniform(ks[9], (L, GDN_HEADS), f32, math.log(1e-3), math.log(1e-1)))
    return {
        'x': nrm(ks[0], (BATCH, SEQ, D_MODEL), 1.0),
        'ln1_g': 1.0 + nrm(ks[1], (L, D_MODEL), 0.02),
        'ln1_b': nrm(ks[2], (L, D_MODEL), 0.02),
        'ffn1_wg': nrm(ks[3], (L, D_MODEL, D_FF), D_MODEL ** -0.5),
        'ffn1_wu': nrm(ks[4], (L, D_MODEL, D_FF), D_MODEL ** -0.5),
        'ffn1_wd': nrm(ks[5], (L, D_FF, D_MODEL), D_FF ** -0.5 * DN_BETA),
        'w_in': nrm(ks[6], (L, D_MODEL, IN_WIDTH), D_MODEL ** -0.5),
        'gdn_conv_w': nrm(ks[7], (L, GDN_CONV, 3 * GDN_WIDTH), GDN_CONV ** -0.5),
        'gdn_a_log': jnp.log(jax.random.uniform(ks[8], (L, GDN_HEADS), f32, 1.0, 16.0)),
        'gdn_dt_bias': jnp.log(jnp.expm1(dt)),
        'gdn_norm_w': 1.0 + nrm(ks[10], (L, GDN_HEAD_DIM), 0.02),
        'nsa_cmp_pos_k': nrm(ks[11], (L, NSA_CMP_LEN, NSA_HEAD_DIM), 0.1),
        'nsa_cmp_pos_v': nrm(ks[12], (L, NSA_CMP_LEN, NSA_HEAD_DIM), 0.1),
        'nsa_cmp_k_w1': nrm(ks[13], (L, NSA_CMP_LEN * NSA_HEAD_DIM, NSA_CMP_HIDDEN), (NSA_CMP_LEN * NSA_HEAD_DIM) ** -0.5),
        'nsa_cmp_k_w2': nrm(ks[14], (L, NSA_CMP_HIDDEN, NSA_HEAD_DIM), NSA_CMP_HIDDEN ** -0.5),
        'nsa_cmp_v_w1': nrm(ks[15], (L, NSA_CMP_LEN * NSA_HEAD_DIM, NSA_CMP_HIDDEN), (NSA_CMP_LEN * NSA_HEAD_DIM) ** -0.5),
        'nsa_cmp_v_w2': nrm(ks[16], (L, NSA_CMP_HIDDEN, NSA_HEAD_DIM), NSA_CMP_HIDDEN ** -0.5),
        'w_out': nrm(ks[17], (L, MIX_WIDTH, D_MODEL), MIX_WIDTH ** -0.5 * DN_BETA),
        'ln2_g': 1.0 + nrm(ks[18], (L, D_MODEL), 0.02),
        'ln2_b': nrm(ks[19], (L, D_MODEL), 0.02),
        'ffn2_wg': nrm(ks[20], (L, D_MODEL, D_FF), D_MODEL ** -0.5),
        'ffn2_wu': nrm(ks[21], (L, D_MODEL, D_FF), D_MODEL ** -0.5),
        'ffn2_wd': nrm(ks[22], (L, D_FF, D_MODEL), D_FF ** -0.5 * DN_BETA),
        'ln3_g': 1.0 + nrm(ks[23], (L, D_MODEL), 0.02),
        'ln3_b': nrm(ks[24], (L, D_MODEL), 0.02),
    }


def reference(x, ln1_g, ln1_b, ffn1_wg, ffn1_wu, ffn1_wd, w_in, gdn_conv_w, gdn_a_log, gdn_dt_bias,
              gdn_norm_w, nsa_cmp_pos_k, nsa_cmp_pos_v, nsa_cmp_k_w1, nsa_cmp_k_w2, nsa_cmp_v_w1,
              nsa_cmp_v_w2, w_out, ln2_g, ln2_b, ffn2_wg, ffn2_wu, ffn2_wd, ln3_g, ln3_b):
    offsets = [int(o) for o in np.cumsum(IN_SIZES)[:-1]]
    for i in range(DEPTH):
        x = layer_norm(DN_ALPHA * x + 0.5 * swiglu(x, ffn1_wg[i], ffn1_wu[i], ffn1_wd[i]), ln1_g[i], ln1_b[i])
        proj = x @ w_in[i]
        (g_q, g_k, g_v, g_z, g_b, g_a, n_q, n_kc, n_vc, n_ks, n_vs, n_kw, n_vw,
         n_gate) = jnp.split(proj, offsets, axis=-1)
        o_gdn = gdn_mixer(g_q, g_k, g_v, g_z, g_b, g_a, gdn_conv_w[i], gdn_a_log[i],
                          gdn_dt_bias[i], gdn_norm_w[i])
        o_nsa = nsa_mixer(n_q, n_kc, n_vc, n_ks, n_vs, n_kw, n_vw, n_gate, nsa_cmp_pos_k[i],
                          nsa_cmp_pos_v[i], nsa_cmp_k_w1[i], nsa_cmp_k_w2[i], nsa_cmp_v_w1[i],
                          nsa_cmp_v_w2[i])
        mix = jnp.concatenate([o_gdn, o_nsa], axis=-1) @ w_out[i]
        x = layer_norm(DN_ALPHA * x + mix, ln2_g[i], ln2_b[i])
        x = layer_norm(DN_ALPHA * x + 0.5 * swiglu(x, ffn2_wg[i], ffn2_wu[i], ffn2_wd[i]), ln3_g[i], ln3_b[i])
    return x
```

```python
import functools

import numpy as np
import jax
import jax.numpy as jnp
from jax import lax
from jax.experimental import pallas as pl
from jax.experimental.pallas import tpu as pltpu

_F32 = jnp.float32
_BF16 = jnp.bfloat16

GDN_HEADS = 4
GDN_HEAD_DIM = 128
GDN_WIDTH = GDN_HEADS * GDN_HEAD_DIM
GDN_CONV = 4
GDN_CHUNK = 64

NSA_Q_HEADS = 8
NSA_KV_HEADS = 2
NSA_GROUP = NSA_Q_HEADS // NSA_KV_HEADS
NSA_HEAD_DIM = 64
NSA_WIDTH = NSA_Q_HEADS * NSA_HEAD_DIM
NSA_KV_WIDTH = NSA_KV_HEADS * NSA_HEAD_DIM
NSA_CMP_LEN = 32
NSA_CMP_STRIDE = 16
NSA_SEL_LEN = 64
NSA_TOPN = 16
NSA_WINDOW = 512
NSA_FORCE_SCORE = 1e4

LN_EPS = 1e-5
RMS_EPS = 1e-6
L2_EPS = 1e-6
DEPTH = 1
DN_ALPHA = (2 * DEPTH) ** 0.25

NEG_BIG = -(2.0 ** 100)
M_INIT = -3.0e38

LANES = 128
VMEM_LIMIT = 56 * 1024 * 1024

FFN_ROWS = 512
FFN_CHUNK = 256
PROJ_ROWS = 512
GDN_ROWS = 256
NSA_QT = 256
NSA_KT = 256


def _sigmoid(x):
    return 1.0 / (1.0 + jnp.exp(-x))


def _softplus(x):
    return jnp.maximum(x, 0.0) + jnp.log(1.0 + jnp.exp(-jnp.abs(x)))


def _mm(a, b):
    return jnp.dot(a.astype(_BF16), b.astype(_BF16), preferred_element_type=_F32)


def _mm_nt(a, b):
    return lax.dot_general(a.astype(_BF16), b.astype(_BF16), (((1,), (1,)), ((), ())),
                           preferred_element_type=_F32)


def _mm_tn(a, b):
    return lax.dot_general(a.astype(_BF16), b.astype(_BF16), (((0,), (0,)), ((), ())),
                           preferred_element_type=_F32)


def _split3(x):
    hi = x.astype(_BF16)
    r = x - hi.astype(_F32)
    mid = r.astype(_BF16)
    lo = (r - mid.astype(_F32)).astype(_BF16)
    return hi, mid, lo


def _layer_norm(y, g, b):
    mu = jnp.mean(y, axis=-1, keepdims=True)
    yc = y - mu
    var = jnp.mean(yc * yc, axis=-1, keepdims=True)
    return yc * lax.rsqrt(var + LN_EPS) * g + b


def _ffn_ln_kernel(x_ref, wgu_ref, wd_ref, g_ref, b_ref, o_ref, acc_ref, *, n_chunks, fc):
    x = x_ref[...]
    xb = x.astype(_BF16)
    acc_ref[...] = jnp.zeros_like(acc_ref)

    def body(c, carry):
        gu = jnp.dot(xb, wgu_ref[c], preferred_element_type=_F32)
        gate = gu[:, :fc]
        h = gate * _sigmoid(gate) * gu[:, fc:]
        acc_ref[...] += jnp.dot(h.astype(_BF16), wd_ref[c], preferred_element_type=_F32)
        return carry

    lax.fori_loop(0, n_chunks, body, 0)
    y = DN_ALPHA * x + 0.5 * acc_ref[...]
    o_ref[...] = _layer_norm(y, g_ref[...], b_ref[...])


def _const_spec(shape):
    zeros = (0,) * len(shape)
    return pl.BlockSpec(shape, lambda *_: zeros, pipeline_mode=pl.Buffered(1))


def _ffn_ln(x, wgu, wd, g, b):
    n, d = x.shape
    nc, _, fc2 = wgu.shape
    tm = min(FFN_ROWS, n)
    return pl.pallas_call(
        functools.partial(_ffn_ln_kernel, n_chunks=nc, fc=fc2 // 2),
        grid=(n // tm,),
        in_specs=[pl.BlockSpec((tm, d), lambda i: (i, 0)),
                  _const_spec((nc, d, fc2)),
                  _const_spec((nc, fc2 // 2, d)),
                  _const_spec((1, d)),
                  _const_spec((1, d))],
        out_specs=pl.BlockSpec((tm, d), lambda i: (i, 0)),
        out_shape=jax.ShapeDtypeStruct((n, d), _F32),
        scratch_shapes=[pltpu.VMEM((tm, d), _F32)],
        compiler_params=pltpu.CompilerParams(dimension_semantics=("parallel",),
                                             vmem_limit_bytes=VMEM_LIMIT),
        name="ffn_ln",
    )(x, wgu, wd, g, b)


def _ffn_weights(wg, wu, wd):
    d, f = wg.shape
    nc = f // FFN_CHUNK
    wg = wg.reshape(d, nc, FFN_CHUNK).transpose(1, 0, 2)
    wu = wu.reshape(d, nc, FFN_CHUNK).transpose(1, 0, 2)
    wgu = jnp.concatenate([wg, wu], axis=-1).astype(_BF16)
    return wgu, wd.reshape(nc, FFN_CHUNK, d).astype(_BF16)


def _proj_kernel(x_ref, w_ref, *o_refs, splits):
    xb = x_ref[...].astype(_BF16)
    for o_ref, (start, width) in zip(o_refs, splits):
        o_ref[...] = jnp.dot(xb, w_ref[:, start:start + width], preferred_element_type=_F32)


def _in_proj(x, w, splits):
    n, d = x.shape
    tm = min(PROJ_ROWS, n)
    return pl.pallas_call(
        functools.partial(_proj_kernel, splits=splits),
        grid=(n // tm,),
        in_specs=[pl.BlockSpec((tm, d), lambda i: (i, 0)), _const_spec(w.shape)],
        out_specs=[pl.BlockSpec((tm, width), lambda i: (i, 0)) for _, width in splits],
        out_shape=[jax.ShapeDtypeStruct((n, width), _F32) for _, width in splits],
        compiler_params=pltpu.CompilerParams(dimension_semantics=("parallel",),
                                             vmem_limit_bytes=VMEM_LIMIT),
        name="in_proj",
    )(x, w)


def _unit_lower_inverse(lmat):
    c = lmat.shape[0]
    eye = jnp.where(lax.broadcasted_iota(jnp.int32, (c, c), 0)
                    == lax.broadcasted_iota(jnp.int32, (c, c), 1), 1.0, 0.0)
    r = eye - lmat
    p = _mm(lmat, lmat)
    span = 2
    while True:
        r = r + _mm(r, p)
        span *= 2
        if span >= c:
            return r
        p = _mm(p, p)


def _gdn_kernel(qkv_ref, z_ref, bac_ref, bar_ref, cw_ref, hpr_ref, hpc_ref, nw_ref, o_ref,
                xs_ref, act_ref, s_ref, *, tb):
    nh, hd, c = GDN_HEADS, GDN_HEAD_DIM, GDN_CHUNK
    width = 3 * nh * hd

    @pl.when(pl.program_id(1) == 0)
    def _():
        xs_ref[0:8, :] = jnp.zeros((8, width), _F32)
        s_ref[...] = jnp.zeros_like(s_ref)

    xs_ref[8:8 + tb, :] = qkv_ref[0]
    conv = cw_ref[0:1, :] * xs_ref[5:5 + tb, :]
    for j in range(1, GDN_CONV):
        conv = conv + cw_ref[j:j + 1, :] * xs_ref[5 + j:5 + j + tb, :]
    act_ref[...] = conv * _sigmoid(conv)
    xs_ref[0:8, :] = xs_ref[tb:tb + 8, :]

    row = lax.broadcasted_iota(jnp.int32, (c, c), 0)
    col = lax.broadcasted_iota(jnp.int32, (c, c), 1)
    causal = row >= col
    strict = row > col
    tril = jnp.where(causal, 1.0, 0.0).astype(_BF16)
    triu = jnp.where(row <= col, 1.0, 0.0).astype(_BF16)

    for ci in range(tb // c):
        r0 = ci * c
        bac = bac_ref[0, r0:r0 + c, :]
        beta_all = _sigmoid(bac)
        g_all = -jnp.exp(hpr_ref[0:1, :]) * _softplus(bac + hpr_ref[1:2, :])
        gc_all = sum(jnp.dot(tril, part, preferred_element_type=_F32) for part in _split3(g_all))
        bar = bar_ref[0, ci]
        gr_all = -jnp.exp(hpc_ref[:, 0:1]) * _softplus(bar + hpc_ref[:, 1:2])
        gcr_all = sum(jnp.dot(part, triu, preferred_element_type=_F32) for part in _split3(gr_all))

        for h in range(nh):
            q = act_ref[r0:r0 + c, h * hd:(h + 1) * hd]
            k = act_ref[r0:r0 + c, (nh + h) * hd:(nh + h + 1) * hd]
            v = act_ref[r0:r0 + c, (2 * nh + h) * hd:(2 * nh + h + 1) * hd]
            q = q * lax.rsqrt(jnp.sum(q * q, axis=-1, keepdims=True) + L2_EPS) * (hd ** -0.5)
            k = k * lax.rsqrt(jnp.sum(k * k, axis=-1, keepdims=True) + L2_EPS)
            beta = beta_all[:, h:h + 1]
            gcc = gc_all[:, nh + h:nh + h + 1]
            gcr = gcr_all[nh + h:nh + h + 1, :]
            glast = gc_all[c - 1:c, nh + h:nh + h + 1]
            decay = jnp.exp(jnp.where(causal, gcc - gcr, NEG_BIG))
            kb = k * beta
            lmat = jnp.where(strict, _mm_nt(kb, k) * decay, 0.0)
            tinv = _unit_lower_inverse(lmat)
            u = _mm(tinv, v * beta)
            w = _mm(tinv, kb * jnp.exp(gcc))
            qk = _mm_nt(q, k) * decay
            s = s_ref[h]
            v_new = u - _mm(w, s)
            o = _mm(q * jnp.exp(gcc), s) + _mm(qk, v_new)
            s_ref[h] = s * jnp.exp(glast) + _mm_tn(k * jnp.exp(glast - gcc), v_new)
            zz = z_ref[0, r0:r0 + c, h * hd:(h + 1) * hd]
            o = (o * lax.rsqrt(jnp.mean(o * o, axis=-1, keepdims=True) + RMS_EPS) * nw_ref[...]
                 * (zz * _sigmoid(zz)))
            o_ref[0, r0:r0 + c, h * hd:(h + 1) * hd] = o


def _gdn(qkv, z, small, conv_w, a_log, dt_bias, norm_w):
    bsz, t, width = qkv.shape
    nh, c = GDN_HEADS, GDN_CHUNK
    tb = min(GDN_ROWS, t)
    ba_rows = small[:, :, :8].reshape(bsz, t // c, c, 8).transpose(0, 1, 3, 2)
    hp_row = jnp.zeros((8, LANES), _F32)
    hp_row = hp_row.at[0, nh:2 * nh].set(a_log).at[1, nh:2 * nh].set(dt_bias)
    hp_col = jnp.zeros((8, LANES), _F32)
    hp_col = hp_col.at[nh:2 * nh, 0].set(a_log).at[nh:2 * nh, 1].set(dt_bias)
    return pl.pallas_call(
        functools.partial(_gdn_kernel, tb=tb),
        grid=(bsz, t // tb),
        in_specs=[pl.BlockSpec((1, tb, width), lambda b, i: (b, i, 0)),
                  pl.BlockSpec((1, tb, GDN_WIDTH), lambda b, i: (b, i, 0)),
                  pl.BlockSpec((1, tb, LANES), lambda b, i: (b, i, 0)),
                  pl.BlockSpec((1, tb // c, 8, c), lambda b, i: (b, i, 0, 0)),
                  pl.BlockSpec((GDN_CONV, width), lambda b, i: (0, 0)),
                  pl.BlockSpec((8, LANES), lambda b, i: (0, 0)),
                  pl.BlockSpec((8, LANES), lambda b, i: (0, 0)),
                  pl.BlockSpec((1, GDN_HEAD_DIM), lambda b, i: (0, 0))],
        out_specs=pl.BlockSpec((1, tb, GDN_WIDTH), lambda b, i: (b, i, 0)),
        out_shape=jax.ShapeDtypeStruct((bsz, t, GDN_WIDTH), _F32),
        scratch_shapes=[pltpu.VMEM((tb + 8, width), _F32),
                        pltpu.VMEM((tb, width), _F32),
                        pltpu.VMEM((nh, GDN_HEAD_DIM, GDN_HEAD_DIM), _F32)],
        compiler_params=pltpu.CompilerParams(dimension_semantics=("parallel", "arbitrary"),
                                             vmem_limit_bytes=VMEM_LIMIT),
        name="gdn",
    )(qkv, z, small, ba_rows, conv_w, hp_row, hp_col, norm_w.reshape(1, -1))


def _gelu_tanh(x):
    return 0.5 * x * (1.0 + jnp.tanh(0.7978845608028654 * (x + 0.044715 * (x * x * x))))


def _cmp_kernel(xk_ref, xv_ref, pk_ref, pv_ref, wk1_ref, wk2_ref, wv1_ref, wv2_ref, ok_ref, ov_ref):
    def compress(x, pos_ref, w1_ref, w2_ref):
        n = x.shape[0]
        ya = _mm(x + pos_ref[0:1, :], w1_ref[0])
        yb = _mm(x + pos_ref[1:2, :], w1_ref[1])
        hid = ya + pltpu.roll(yb, n - 1, 0)
        return _mm(_gelu_tanh(hid), w2_ref[...])

    ok_ref[0, 0] = compress(xk_ref[0, 0], pk_ref, wk1_ref, wk2_ref)
    ov_ref[0, 0] = compress(xv_ref[0, 0], pv_ref, wv1_ref, wv2_ref)


def _nsa_compress(kc, vc, pos_k, pos_v, k_w1, k_w2, v_w1, v_w2):
    bsz, hk, t, dh = kc.shape
    half = NSA_CMP_STRIDE * dh
    nrow = t // NSA_CMP_STRIDE
    hidden = k_w1.shape[1]
    xk = kc.reshape(bsz, hk, nrow, half)
    xv = vc.reshape(bsz, hk, nrow, half)
    x_spec = pl.BlockSpec((1, 1, nrow, half), lambda b, h: (b, h, 0, 0))
    o_spec = pl.BlockSpec((1, 1, nrow, dh), lambda b, h: (b, h, 0, 0))
    full = lambda shape: pl.BlockSpec(shape, lambda b, h: (0,) * len(shape))
    return pl.pallas_call(
        _cmp_kernel,
        grid=(bsz, hk),
        in_specs=[x_spec, x_spec, full((2, half)), full((2, half)),
                  full((2, half, hidden)), full((hidden, dh)),
                  full((2, half, hidden)), full((hidden, dh))],
        out_specs=[o_spec, o_spec],
        out_shape=[jax.ShapeDtypeStruct((bsz, hk, nrow, dh), _F32)] * 2,
        compiler_params=pltpu.CompilerParams(dimension_semantics=("parallel", "parallel"),
                                             vmem_limit_bytes=VMEM_LIMIT),
        name="nsa_cmp",
    )(xk, xv, pos_k.reshape(2, half), pos_v.reshape(2, half),
      k_w1.reshape(2, half, hidden).astype(_BF16), k_w2.astype(_BF16),
      v_w1.reshape(2, half, hidden).astype(_BF16), v_w2.astype(_BF16))


def _nsa_kernel(q_ref, kc_ref, vct_ref, cst_ref, ks_ref, vst_ref, kw_ref, vwt_ref, gt_ref, o_ref,
                kaug_ref, acc_ref, *, qt, kt, n_sel):
    grp, dh = NSA_GROUP, NSA_HEAD_DIM
    lanes = grp * qt
    t_all = ks_ref.shape[2]
    ns = t_all // NSA_SEL_LEN
    nc = kc_ref.shape[2]
    i = pl.program_id(2)
    s0 = i * qt

    @pl.when(i == 0)
    def _():
        rblk = lax.broadcasted_iota(jnp.int32, (t_all, ns), 0) // NSA_SEL_LEN
        cblk = lax.broadcasted_iota(jnp.int32, (t_all, ns), 1)
        kaug_ref[:, 0:dh] = ks_ref[0, 0]
        kaug_ref[:, dh:dh + ns] = jnp.where(rblk == cblk, 1.0, 0.0).astype(_BF16)

    lane = lax.broadcasted_iota(jnp.int32, (1, lanes), 1)
    tq = s0 + (lane & (qt - 1))
    qs = (q_ref[0, 0, 0] * (dh ** -0.5)).astype(_BF16)

    sc = jnp.dot(kc_ref[0, 0], qs, preferred_element_type=_F32)
    cend = lax.broadcasted_iota(jnp.int32, (nc, 1), 0) * NSA_CMP_STRIDE + (NSA_CMP_LEN - 1)
    valid = cend <= tq
    scm = jnp.where(valid, sc, NEG_BIG)
    mc = jnp.max(scm, axis=0, keepdims=True)
    ec = jnp.where(valid, jnp.exp(scm - mc), 0.0)
    lc = jnp.sum(ec, axis=0, keepdims=True)
    pc = ec / jnp.where(lc > 0.0, lc, 1.0)
    o_cmp = jnp.dot(vct_ref[0, 0], pc.astype(_BF16), preferred_element_type=_F32)

    psum = pc[:, 0:qt]
    for g in range(1, grp):
        psum = psum + pc[:, g * qt:(g + 1) * qt]
    p_hi = psum.astype(_BF16)
    p_lo = (psum - p_hi.astype(_F32)).astype(_BF16)
    cst = cst_ref[...]
    imp = (jnp.dot(cst, p_hi, preferred_element_type=_F32)
           + jnp.dot(cst, p_lo, preferred_element_type=_F32))
    blk = lax.broadcasted_iota(jnp.int32, (ns, qt), 0)
    tq1 = s0 + lax.broadcasted_iota(jnp.int32, (ns, qt), 1)
    cur = tq1 // NSA_SEL_LEN
    visible = blk * NSA_SEL_LEN <= tq1
    forced = (blk == 0) | (blk == cur) | (blk == cur - 1)
    imp = jnp.where(visible, jnp.where(forced, NSA_FORCE_SCORE, imp), -1.0)
    rank = jnp.zeros((ns, qt), _F32)
    for j in range(ns):
        rowj = imp[j:j + 1, :]
        ge = jnp.where(rowj >= imp, 1.0, 0.0)
        gt = jnp.where(rowj > imp, 1.0, 0.0)
        rank = rank + jnp.where(blk > j, ge, gt)
    bias = jnp.where(rank < n_sel, 0.0, NEG_BIG).astype(_BF16)
    qaug = jnp.concatenate([qs, jnp.concatenate([bias] * grp, axis=1)], axis=0)

    def tile_update(s, v_tile, m, l):
        m_new = jnp.maximum(m, jnp.max(s, axis=0, keepdims=True))
        a = jnp.exp(m - m_new)
        p = jnp.exp(s - m_new)
        acc_ref[...] = a * acc_ref[...] + jnp.dot(v_tile, p.astype(_BF16),
                                                  preferred_element_type=_F32)
        return m_new, a * l + jnp.sum(p, axis=0, keepdims=True)

    m0 = jnp.full((1, lanes), M_INIT, _F32)
    l0 = jnp.zeros((1, lanes), _F32)
    kidx = lax.broadcasted_iota(jnp.int32, (kt, 1), 0)

    acc_ref[...] = jnp.zeros_like(acc_ref)

    def sel_body(j, carry):
        k_tile = kaug_ref[pl.ds(pl.multiple_of(j * kt, kt), kt), :]
        s = jnp.dot(k_tile, qaug, preferred_element_type=_F32)
        return tile_update(s, vst_ref[0, 0, j], *carry)

    m, l = lax.fori_loop(0, i, sel_body, (m0, l0))
    k_tile = kaug_ref[pl.ds(pl.multiple_of(i * kt, kt), kt), :]
    s = jnp.dot(k_tile, qaug, preferred_element_type=_F32)
    s = jnp.where(i * kt + kidx <= tq, s, NEG_BIG)
    m, l = tile_update(s, vst_ref[0, 0, i], m, l)
    o_sel = acc_ref[...] / l

    acc_ref[...] = jnp.zeros_like(acc_ref)

    def win_body(j, carry):
        k_tile = kw_ref[0, 0, pl.ds(pl.multiple_of(j * kt, kt), kt), :]
        s = jnp.dot(k_tile, qs, preferred_element_type=_F32)
        rel = tq - (j * kt + kidx)
        s = jnp.where((rel >= 0) & (rel < NSA_WINDOW), s, NEG_BIG)
        return tile_update(s, vwt_ref[0, 0, j], *carry)

    first = jnp.maximum(i - (NSA_WINDOW + kt - 1) // kt, 0)
    m, l = lax.fori_loop(first, i + 1, win_body, (m0, l0))
    o_win = acc_ref[...] / l

    gate = _sigmoid(gt_ref[0, 0, 0])
    o_ref[0, 0, 0] = gate[0:1, :] * o_cmp + gate[1:2, :] * o_sel + gate[2:3, :] * o_win


def _nsa_attention(qt_arr, k_cmp, v_cmp_t, cst, ks, vs_t, kw, vw_t, gates_t):
    bsz, hk, nq, dh, lanes = qt_arr.shape
    qt = lanes // NSA_GROUP
    kt = qt
    t = ks.shape[2]
    ns = t // NSA_SEL_LEN
    nc = k_cmp.shape[2]
    ntile = t // kt
    per_head = lambda shape: pl.BlockSpec((1, 1) + shape, lambda b, h, i: (b, h) + (0,) * len(shape))
    per_tile = lambda shape: pl.BlockSpec((1, 1, 1) + shape,
                                          lambda b, h, i: (b, h, i) + (0,) * len(shape))
    return pl.pallas_call(
        functools.partial(_nsa_kernel, qt=qt, kt=kt, n_sel=min(NSA_TOPN, ns)),
        grid=(bsz, hk, nq),
        in_specs=[per_tile((dh, lanes)),
                  per_head((nc, dh)), per_head((dh, nc)),
                  pl.BlockSpec((ns, nc), lambda b, h, i: (0, 0)),
                  per_head((t, dh)), per_head((ntile, dh, kt)),
                  per_head((t, dh)), per_head((ntile, dh, kt)),
                  per_tile((3, lanes))],
        out_specs=per_tile((dh, lanes)),
        out_shape=jax.ShapeDtypeStruct((bsz, hk, nq, dh, lanes), _F32),
        scratch_shapes=[pltpu.VMEM((t, dh + ns), _BF16),
                        pltpu.VMEM((dh, lanes), _F32)],
        compiler_params=pltpu.CompilerParams(
            dimension_semantics=("parallel", "parallel", "arbitrary"),
            vmem_limit_bytes=VMEM_LIMIT),
        name="nsa_attn",
    )(qt_arr, k_cmp, v_cmp_t, cst, ks, vs_t, kw, vw_t, gates_t)


def _cmp_to_sel_t(t):
    ncmp = (t - NSA_CMP_LEN) // NSA_CMP_STRIDE + 1
    c_start = np.arange(ncmp) * NSA_CMP_STRIDE
    s_start = np.arange(t // NSA_SEL_LEN) * NSA_SEL_LEN
    overlap = np.clip(np.minimum(c_start[:, None] + NSA_CMP_LEN, s_start[None, :] + NSA_SEL_LEN)
                      - np.maximum(c_start[:, None], s_start[None, :]), 0, None) / NSA_CMP_LEN
    out = np.zeros((t // NSA_SEL_LEN, t // NSA_CMP_STRIDE), np.float32)
    out[:, :ncmp] = overlap.T
    return jnp.asarray(out, _BF16)


def _nsa(nq, nkv, small, pos_k, pos_v, k_w1, k_w2, v_w1, v_w2):
    bsz, t, _ = nq.shape
    hk, grp, dh = NSA_KV_HEADS, NSA_GROUP, NSA_HEAD_DIM
    qt = min(NSA_QT, t)
    nqt = t // qt
    kt = qt

    def kv_heads(idx):
        w = NSA_KV_WIDTH
        return nkv[:, :, idx * w:(idx + 1) * w].reshape(bsz, t, hk, dh).transpose(0, 2, 1, 3)

    def tiles_t(x):
        return x.reshape(bsz, hk, t // kt, kt, dh).transpose(0, 1, 2, 4, 3).astype(_BF16)

    k_cmp, v_cmp = _nsa_compress(kv_heads(0), kv_heads(1), pos_k, pos_v, k_w1, k_w2, v_w1, v_w2)
    q_t = (nq.reshape(bsz, nqt, qt, hk, grp, dh).transpose(0, 3, 1, 5, 4, 2)
           .reshape(bsz, hk, nqt, dh, grp * qt))
    gates_t = (small[:, :, 8:8 + 3 * NSA_Q_HEADS].reshape(bsz, nqt, qt, hk, grp, 3)
               .transpose(0, 3, 1, 5, 4, 2).reshape(bsz, hk, nqt, 3, grp * qt))
    o_t = _nsa_attention(q_t, k_cmp.astype(_BF16), v_cmp.transpose(0, 1, 3, 2).astype(_BF16),
                         _cmp_to_sel_t(t), kv_heads(2).astype(_BF16), tiles_t(kv_heads(3)),
                         kv_heads(4).astype(_BF16), tiles_t(kv_heads(5)), gates_t)
    return (o_t.reshape(bsz, hk, nqt, dh, grp, qt).transpose(0, 2, 5, 1, 4, 3)
            .reshape(bsz, t, NSA_WIDTH))


def _out_ln_kernel(x_ref, a_ref, b_ref, wa_ref, wb_ref, g_ref, beta_ref, o_ref):
    mix = (jnp.dot(a_ref[...].astype(_BF16), wa_ref[...], preferred_element_type=_F32)
           + jnp.dot(b_ref[...].astype(_BF16), wb_ref[...], preferred_element_type=_F32))
    o_ref[...] = _layer_norm(DN_ALPHA * x_ref[...] + mix, g_ref[...], beta_ref[...])


def _out_ln(x, oa, ob, wa, wb, g, b):
    n, d = x.shape
    tm = min(PROJ_ROWS, n)
    row = lambda width: pl.BlockSpec((tm, width), lambda i: (i, 0))
    return pl.pallas_call(
        _out_ln_kernel,
        grid=(n // tm,),
        in_specs=[row(d), row(oa.shape[1]), row(ob.shape[1]), _const_spec(wa.shape),
                  _const_spec(wb.shape), _const_spec((1, d)), _const_spec((1, d))],
        out_specs=row(d),
        out_shape=jax.ShapeDtypeStruct((n, d), _F32),
        compiler_params=pltpu.CompilerParams(dimension_semantics=("parallel",),
                                             vmem_limit_bytes=VMEM_LIMIT),
        name="out_ln",
    )(x, oa, ob, wa, wb, g, b)


_QKV = 3 * GDN_WIDTH
_SPLITS = ((0, _QKV), (_QKV, GDN_WIDTH), (_QKV + GDN_WIDTH, NSA_WIDTH),
           (_QKV + GDN_WIDTH + NSA_WIDTH, 6 * NSA_KV_WIDTH),
           (_QKV + GDN_WIDTH + NSA_WIDTH + 6 * NSA_KV_WIDTH, LANES))


def _reorder_w_in(w):
    d = w.shape[0]
    big = 4 * GDN_WIDTH
    nsa0 = big + 2 * GDN_HEADS
    nsa1 = nsa0 + NSA_WIDTH + 6 * NSA_KV_WIDTH
    small = jnp.concatenate([w[:, big:nsa0], w[:, nsa1:]], axis=1)
    pad = jnp.zeros((d, LANES - small.shape[1]), w.dtype)
    return jnp.concatenate([w[:, :big], w[:, nsa0:nsa1], small, pad], axis=1).astype(_BF16)


def kernel(x, ln1_g, ln1_b, ffn1_wg, ffn1_wu, ffn1_wd, w_in, gdn_conv_w, gdn_a_log, gdn_dt_bias,
           gdn_norm_w, nsa_cmp_pos_k, nsa_cmp_pos_v, nsa_cmp_k_w1, nsa_cmp_k_w2, nsa_cmp_v_w1,
           nsa_cmp_v_w2, w_out, ln2_g, ln2_b, ffn2_wg, ffn2_wu, ffn2_wd, ln3_g, ln3_b):
    bsz, t, d = x.shape
    n = bsz * t
    h = x.reshape(n, d)
    for i in range(DEPTH):
        wgu, wd = _ffn_weights(ffn1_wg[i], ffn1_wu[i], ffn1_wd[i])
        h = _ffn_ln(h, wgu, wd, ln1_g[i].reshape(1, d), ln1_b[i].reshape(1, d))
        qkv, z, nq, nkv, small = _in_proj(h, _reorder_w_in(w_in[i]), _SPLITS)
        small = small.reshape(bsz, t, LANES)
        o_gdn = _gdn(qkv.reshape(bsz, t, _QKV), z.reshape(bsz, t, GDN_WIDTH), small,
                     gdn_conv_w[i], gdn_a_log[i], gdn_dt_bias[i], gdn_norm_w[i])
        o_nsa = _nsa(nq.reshape(bsz, t, NSA_WIDTH), nkv.reshape(bsz, t, 6 * NSA_KV_WIDTH), small,
                     nsa_cmp_pos_k[i], nsa_cmp_pos_v[i], nsa_cmp_k_w1[i], nsa_cmp_k_w2[i],
                     nsa_cmp_v_w1[i], nsa_cmp_v_w2[i])
        wo = w_out[i].astype(_BF16)
        h = _out_ln(h, o_gdn.reshape(n, GDN_WIDTH), o_nsa.reshape(n, NSA_WIDTH),
                    wo[:GDN_WIDTH], wo[GDN_WIDTH:], ln2_g[i].reshape(1, d), ln2_b[i].reshape(1, d))
        wgu, wd = _ffn_weights(ffn2_wg[i], ffn2_wu[i], ffn2_wd[i])
        h = _ffn_ln(h, wgu, wd, ln3_g[i].reshape(1, d), ln3_b[i].reshape(1, d))
    return h.reshape(bsz, t, d)
```

```python
import functools

import numpy as np
import jax
import jax.numpy as jnp
from jax import lax
from jax.experimental import pallas as pl
from jax.experimental.pallas import tpu as pltpu

_F32 = jnp.float32
_BF16 = jnp.bfloat16

GDN_HEADS = 4
GDN_HEAD_DIM = 128
GDN_WIDTH = GDN_HEADS * GDN_HEAD_DIM
GDN_CONV = 4
GDN_CHUNK = 64

NSA_Q_HEADS = 8
NSA_KV_HEADS = 2
NSA_GROUP = NSA_Q_HEADS // NSA_KV_HEADS
NSA_HEAD_DIM = 64
NSA_WIDTH = NSA_Q_HEADS * NSA_HEAD_DIM
NSA_KV_WIDTH = NSA_KV_HEADS * NSA_HEAD_DIM
NSA_CMP_LEN = 32
NSA_CMP_STRIDE = 16
NSA_SEL_LEN = 64
NSA_TOPN = 16
NSA_WINDOW = 512
NSA_FORCE_SCORE = 1e4

LN_EPS = 1e-5
RMS_EPS = 1e-6
L2_EPS = 1e-6
DEPTH = 1
DN_ALPHA = (2 * DEPTH) ** 0.25

NEG_BIG = -(2.0 ** 100)
M_INIT = -3.0e38

LANES = 128
VMEM_LIMIT = 56 * 1024 * 1024

FFN_ROWS = 512
FFN_CHUNK = 256
PROJ_ROWS = 512
GDN_ROWS = 256
NSA_QT = 256
NSA_KT = 256


def _sigmoid(x):
    return 1.0 / (1.0 + jnp.exp(-x))


def _softplus(x):
    return jnp.maximum(x, 0.0) + jnp.log(1.0 + jnp.exp(-jnp.abs(x)))


def _mm(a, b):
    return jnp.dot(a.astype(_BF16), b.astype(_BF16), preferred_element_type=_F32)


def _mm_nt(a, b):
    return lax.dot_general(a.astype(_BF16), b.astype(_BF16), (((1,), (1,)), ((), ())),
                           preferred_element_type=_F32)


def _mm_tn(a, b):
    return lax.dot_general(a.astype(_BF16), b.astype(_BF16), (((0,), (0,)), ((), ())),
                           preferred_element_type=_F32)


def _split3(x):
    hi = x.astype(_BF16)
    r = x - hi.astype(_F32)
    mid = r.astype(_BF16)
    lo = (r - mid.astype(_F32)).astype(_BF16)
    return hi, mid, lo


def _layer_norm(y, g, b):
    mu = jnp.mean(y, axis=-1, keepdims=True)
    yc = y - mu
    var = jnp.mean(yc * yc, axis=-1, keepdims=True)
    return yc * lax.rsqrt(var + LN_EPS) * g + b


def _ffn_ln_kernel(x_ref, wgu_ref, wd_ref, g_ref, b_ref, o_ref, acc_ref, *, n_chunks, fc):
    x = x_ref[...]
    xb = x.astype(_BF16)
    acc_ref[...] = jnp.zeros_like(acc_ref)

    def body(c, carry):
        gu = jnp.dot(xb, wgu_ref[c], preferred_element_type=_F32)
        gate = gu[:, :fc]
        h = gate * _sigmoid(gate) * gu[:, fc:]
        acc_ref[...] += jnp.dot(h.astype(_BF16), wd_ref[c], preferred_element_type=_F32)
        return carry

    lax.fori_loop(0, n_chunks, body, 0)
    y = DN_ALPHA * x + 0.5 * acc_ref[...]
    o_ref[...] = _layer_norm(y, g_ref[...], b_ref[...])


def _const_spec(shape):
    zeros = (0,) * len(shape)
    return pl.BlockSpec(shape, lambda *_: zeros, pipeline_mode=pl.Buffered(1))


def _ffn_ln(x, wgu, wd, g, b):
    n, d = x.shape
    nc, _, fc2 = wgu.shape
    tm = min(FFN_ROWS, n)
    return pl.pallas_call(
        functools.partial(_ffn_ln_kernel, n_chunks=nc, fc=fc2 // 2),
        grid=(n // tm,),
        in_specs=[pl.BlockSpec((tm, d), lambda i: (i, 0)),
                  _const_spec((nc, d, fc2)),
                  _const_spec((nc, fc2 // 2, d)),
                  _const_spec((1, d)),
                  _const_spec((1, d))],
        out_specs=pl.BlockSpec((tm, d), lambda i: (i, 0)),
        out_shape=jax.ShapeDtypeStruct((n, d), _F32),
        scratch_shapes=[pltpu.VMEM((tm, d), _F32)],
        compiler_params=pltpu.CompilerParams(dimension_semantics=("parallel",),
                                             vmem_limit_bytes=VMEM_LIMIT),
        name="ffn_ln",
    )(x, wgu, wd, g, b)


def _ffn_weights(wg, wu, wd):
    d, f = wg.shape
    nc = f // FFN_CHUNK
    wg = wg.reshape(d, nc, FFN_CHUNK).transpose(1, 0, 2)
    wu = wu.reshape(d, nc, FFN_CHUNK).transpose(1, 0, 2)
    wgu = jnp.concatenate([wg, wu], axis=-1).astype(_BF16)
    return wgu, wd.reshape(nc, FFN_CHUNK, d).astype(_BF16)


def _proj_kernel(x_ref, w_ref, *o_refs, splits):
    xb = x_ref[...].astype(_BF16)
    for o_ref, (start, width) in zip(o_refs, splits):
        o_ref[...] = jnp.dot(xb, w_ref[:, start:start + width], preferred_element_type=_F32)


def _in_proj(x, w, splits):
    n, d = x.shape
    tm = min(PROJ_ROWS, n)
    return pl.pallas_call(
        functools.partial(_proj_kernel, splits=splits),
        grid=(n // tm,),
        in_specs=[pl.BlockSpec((tm, d), lambda i: (i, 0)), _const_spec(w.shape)],
        out_specs=[pl.BlockSpec((tm, width), lambda i: (i, 0)) for _, width in splits],
        out_shape=[jax.ShapeDtypeStruct((n, width), _F32) for _, width in splits],
        compiler_params=pltpu.CompilerParams(dimension_semantics=("parallel",),
                                             vmem_limit_bytes=VMEM_LIMIT),
        name="in_proj",
    )(x, w)


def _bdot(a, b):
    return jnp.dot(a, b, preferred_element_type=_F32)


def _unit_lower_inverses(lmats):
    c = lmats[0].shape[0]
    eye = jnp.where(lax.broadcasted_iota(jnp.int32, (c, c), 0)
                    == lax.broadcasted_iota(jnp.int32, (c, c), 1), 1.0, 0.0)
    rs = [eye - l for l in lmats]
    pb = [l.astype(_BF16) for l in lmats]
    ps = [_bdot(l, l) for l in pb]
    span = 2
    while True:
        pb = [p.astype(_BF16) for p in ps]
        rs = [r + _bdot(r.astype(_BF16), p) for r, p in zip(rs, pb)]
        span *= 2
        if span >= c:
            return rs
        ps = [_bdot(p, p) for p in pb]


def _gdn_kernel(qkv_ref, z_ref, bac_ref, bar_ref, cw_ref, hpr_ref, hpc_ref, nw_ref, o_ref,
                xs_ref, act_ref, s_ref, *, tb):
    nh, hd, c = GDN_HEADS, GDN_HEAD_DIM, GDN_CHUNK
    width = 3 * nh * hd

    @pl.when(pl.program_id(1) == 0)
    def _():
        xs_ref[0:8, :] = jnp.zeros((8, width), _F32)
        s_ref[...] = jnp.zeros_like(s_ref)

    xs_ref[8:8 + tb, :] = qkv_ref[0]
    conv = cw_ref[0:1, :] * xs_ref[5:5 + tb, :]
    for j in range(1, GDN_CONV):
        conv = conv + cw_ref[j:j + 1, :] * xs_ref[5 + j:5 + j + tb, :]
    act_ref[...] = conv * _sigmoid(conv)
    xs_ref[0:8, :] = xs_ref[tb:tb + 8, :]

    row = lax.broadcasted_iota(jnp.int32, (c, c), 0)
    col = lax.broadcasted_iota(jnp.int32, (c, c), 1)
    causal = row >= col
    strict = row > col
    tril = jnp.where(causal, 1.0, 0.0).astype(_BF16)
    triu = jnp.where(row <= col, 1.0, 0.0).astype(_BF16)

    nchunk = tb // c
    qg_l, qb_l, kb_l, kbeta_l, rhs_l, kd_l, decay_l, eg_l = [], [], [], [], [], [], [], []
    for ci in range(nchunk):
        r0 = ci * c
        bac = bac_ref[0, r0:r0 + c, :]
        beta_all = _sigmoid(bac)
        g_all = -jnp.exp(hpr_ref[0:1, :]) * _softplus(bac + hpr_ref[1:2, :])
        gc_all = sum(_bdot(tril, part) for part in _split3(g_all))
        bar = bar_ref[0, ci]
        gr_all = -jnp.exp(hpc_ref[:, 0:1]) * _softplus(bar + hpc_ref[:, 1:2])
        gcr_all = sum(_bdot(part, triu) for part in _split3(gr_all))
        for h in range(nh):
            q = act_ref[r0:r0 + c, h * hd:(h + 1) * hd]
            k = act_ref[r0:r0 + c, (nh + h) * hd:(nh + h + 1) * hd]
            v = act_ref[r0:r0 + c, (2 * nh + h) * hd:(2 * nh + h + 1) * hd]
            q = q * lax.rsqrt(jnp.sum(q * q, axis=-1, keepdims=True) + L2_EPS) * (hd ** -0.5)
            k = k * lax.rsqrt(jnp.sum(k * k, axis=-1, keepdims=True) + L2_EPS)
            beta = beta_all[:, h:h + 1]
            gcc = gc_all[:, nh + h:nh + h + 1]
            gcr = gcr_all[nh + h:nh + h + 1, :]
            glast = gc_all[c - 1:c, nh + h:nh + h + 1]
            decay_l.append(jnp.exp(jnp.where(causal, gcc - gcr, NEG_BIG)))
            kbeta = k * beta
            egc = jnp.exp(gcc)
            qg_l.append(q * egc)
            qb_l.append(q.astype(_BF16))
            kb_l.append(k.astype(_BF16))
            kbeta_l.append(kbeta.astype(_BF16))
            rhs_l.append(jnp.concatenate([kbeta * egc, v * beta], axis=1).astype(_BF16))
            kd_l.append((k * jnp.exp(glast - gcc)).astype(_BF16))
            eg_l.append(jnp.exp(glast))

    nt = (((1,), (1,)), ((), ()))
    tn = (((0,), (0,)), ((), ()))
    kk_l = [lax.dot_general(a, b, nt, preferred_element_type=_F32) for a, b in zip(kbeta_l, kb_l)]
    qk_l = [lax.dot_general(a, b, nt, preferred_element_type=_F32) for a, b in zip(qb_l, kb_l)]
    lmat_l = [jnp.where(strict, kk * d, 0.0) for kk, d in zip(kk_l, decay_l)]
    tinv_l = _unit_lower_inverses(lmat_l)
    wu_l = [_bdot(t.astype(_BF16), r).astype(_BF16) for t, r in zip(tinv_l, rhs_l)]
    qkb_l = [(qk * d).astype(_BF16) for qk, d in zip(qk_l, decay_l)]
    pn_l = [lax.dot_general(kd, wu, tn, preferred_element_type=_F32) for kd, wu in zip(kd_l, wu_l)]
    ab_l = [_bdot(qk, wu) for qk, wu in zip(qkb_l, wu_l)]
    a_l = [(qg - ab[:, :hd]).astype(_BF16) for qg, ab in zip(qg_l, ab_l)]
    p_l = [pn[:, :hd].astype(_BF16) for pn in pn_l]

    states = [s_ref[h] for h in range(nh)]
    for ci in range(nchunk):
        r0 = ci * c
        sb = [s.astype(_BF16) for s in states]
        outs = [_bdot(a_l[ci * nh + h], sb[h]) + ab_l[ci * nh + h][:, hd:] for h in range(nh)]
        states = [states[h] * eg_l[ci * nh + h] + pn_l[ci * nh + h][:, hd:]
                  - _bdot(p_l[ci * nh + h], sb[h]) for h in range(nh)]
        for h in range(nh):
            o = outs[h]
            zz = z_ref[0, r0:r0 + c, h * hd:(h + 1) * hd]
            o = (o * lax.rsqrt(jnp.mean(o * o, axis=-1, keepdims=True) + RMS_EPS) * nw_ref[...]
                 * (zz * _sigmoid(zz)))
            o_ref[0, r0:r0 + c, h * hd:(h + 1) * hd] = o
    for h in range(nh):
        s_ref[h] = states[h]


def _gdn(qkv, z, small, conv_w, a_log, dt_bias, norm_w):
    bsz, t, width = qkv.shape
    nh, c = GDN_HEADS, GDN_CHUNK
    tb = min(GDN_ROWS, t)
    ba_rows = small[:, :, :8].reshape(bsz, t // c, c, 8).transpose(0, 1, 3, 2)
    hp_row = jnp.zeros((8, LANES), _F32)
    hp_row = hp_row.at[0, nh:2 * nh].set(a_log).at[1, nh:2 * nh].set(dt_bias)
    hp_col = jnp.zeros((8, LANES), _F32)
    hp_col = hp_col.at[nh:2 * nh, 0].set(a_log).at[nh:2 * nh, 1].set(dt_bias)
    return pl.pallas_call(
        functools.partial(_gdn_kernel, tb=tb),
        grid=(bsz, t // tb),
        in_specs=[pl.BlockSpec((1, tb, width), lambda b, i: (b, i, 0)),
                  pl.BlockSpec((1, tb, GDN_WIDTH), lambda b, i: (b, i, 0)),
                  pl.BlockSpec((1, tb, LANES), lambda b, i: (b, i, 0)),
                  pl.BlockSpec((1, tb // c, 8, c), lambda b, i: (b, i, 0, 0)),
                  pl.BlockSpec((GDN_CONV, width), lambda b, i: (0, 0)),
                  pl.BlockSpec((8, LANES), lambda b, i: (0, 0)),
                  pl.BlockSpec((8, LANES), lambda b, i: (0, 0)),
                  pl.BlockSpec((1, GDN_HEAD_DIM), lambda b, i: (0, 0))],
        out_specs=pl.BlockSpec((1, tb, GDN_WIDTH), lambda b, i: (b, i, 0)),
        out_shape=jax.ShapeDtypeStruct((bsz, t, GDN_WIDTH), _F32),
        scratch_shapes=[pltpu.VMEM((tb + 8, width), _F32),
                        pltpu.VMEM((tb, width), _F32),
                        pltpu.VMEM((nh, GDN_HEAD_DIM, GDN_HEAD_DIM), _F32)],
        compiler_params=pltpu.CompilerParams(dimension_semantics=("parallel", "arbitrary"),
                                             vmem_limit_bytes=VMEM_LIMIT),
        name="gdn",
    )(qkv, z, small, ba_rows, conv_w, hp_row, hp_col, norm_w.reshape(1, -1))


def _gelu_tanh(x):
    return 0.5 * x * (1.0 + jnp.tanh(0.7978845608028654 * (x + 0.044715 * (x * x * x))))


def _cmp_kernel(xk_ref, xv_ref, pk_ref, pv_ref, wk1_ref, wk2_ref, wv1_ref, wv2_ref, ok_ref, ov_ref):
    def compress(x, pos_ref, w1_ref, w2_ref):
        n = x.shape[0]
        ya = _mm(x + pos_ref[0:1, :], w1_ref[0])
        yb = _mm(x + pos_ref[1:2, :], w1_ref[1])
        hid = ya + pltpu.roll(yb, n - 1, 0)
        return _mm(_gelu_tanh(hid), w2_ref[...])

    ok_ref[0, 0] = compress(xk_ref[0, 0], pk_ref, wk1_ref, wk2_ref)
    ov_ref[0, 0] = compress(xv_ref[0, 0], pv_ref, wv1_ref, wv2_ref)


def _nsa_compress(kc, vc, pos_k, pos_v, k_w1, k_w2, v_w1, v_w2):
    bsz, hk, t, dh = kc.shape
    half = NSA_CMP_STRIDE * dh
    nrow = t // NSA_CMP_STRIDE
    hidden = k_w1.shape[1]
    xk = kc.reshape(bsz, hk, nrow, half)
    xv = vc.reshape(bsz, hk, nrow, half)
    x_spec = pl.BlockSpec((1, 1, nrow, half), lambda b, h: (b, h, 0, 0))
    o_spec = pl.BlockSpec((1, 1, nrow, dh), lambda b, h: (b, h, 0, 0))
    full = lambda shape: pl.BlockSpec(shape, lambda b, h: (0,) * len(shape))
    return pl.pallas_call(
        _cmp_kernel,
        grid=(bsz, hk),
        in_specs=[x_spec, x_spec, full((2, half)), full((2, half)),
                  full((2, half, hidden)), full((hidden, dh)),
                  full((2, half, hidden)), full((hidden, dh))],
        out_specs=[o_spec, o_spec],
        out_shape=[jax.ShapeDtypeStruct((bsz, hk, nrow, dh), _F32)] * 2,
        compiler_params=pltpu.CompilerParams(dimension_semantics=("parallel", "parallel"),
                                             vmem_limit_bytes=VMEM_LIMIT),
        name="nsa_cmp",
    )(xk, xv, pos_k.reshape(2, half), pos_v.reshape(2, half),
      k_w1.reshape(2, half, hidden).astype(_BF16), k_w2.astype(_BF16),
      v_w1.reshape(2, half, hidden).astype(_BF16), v_w2.astype(_BF16))


def _nsa_kernel(q_ref, kc_ref, vct_ref, cst_ref, ks_ref, vst_ref, kw_ref, vwt_ref, gt_ref, o_ref,
                kaug_ref, acc_ref, *, qt, kt, n_sel):
    grp, dh = NSA_GROUP, NSA_HEAD_DIM
    lanes = grp * qt
    t_all = ks_ref.shape[2]
    ns = t_all // NSA_SEL_LEN
    nc = kc_ref.shape[2]
    i = pl.program_id(2)
    s0 = i * qt

    @pl.when(i == 0)
    def _():
        rblk = lax.broadcasted_iota(jnp.int32, (t_all, ns), 0) // NSA_SEL_LEN
        cblk = lax.broadcasted_iota(jnp.int32, (t_all, ns), 1)
        kaug_ref[:, 0:dh] = ks_ref[0, 0]
        kaug_ref[:, dh:dh + ns] = jnp.where(rblk == cblk, 1.0, 0.0).astype(_BF16)

    lane = lax.broadcasted_iota(jnp.int32, (1, lanes), 1)
    tq = s0 + (lane & (qt - 1))
    qs = (q_ref[0, 0, 0] * (dh ** -0.5)).astype(_BF16)

    sc = jnp.dot(kc_ref[0, 0], qs, preferred_element_type=_F32)
    cend = lax.broadcasted_iota(jnp.int32, (nc, 1), 0) * NSA_CMP_STRIDE + (NSA_CMP_LEN - 1)
    valid = cend <= tq
    scm = jnp.where(valid, sc, NEG_BIG)
    mc = jnp.max(scm, axis=0, keepdims=True)
    ec = jnp.where(valid, jnp.exp(scm - mc), 0.0)
    lc = jnp.sum(ec, axis=0, keepdims=True)
    pc = ec / jnp.where(lc > 0.0, lc, 1.0)
    o_cmp = jnp.dot(vct_ref[0, 0], pc.astype(_BF16), preferred_element_type=_F32)

    psum = pc[:, 0:qt]
    for g in range(1, grp):
        psum = psum + pc[:, g * qt:(g + 1) * qt]
    p_hi = psum.astype(_BF16)
    p_lo = (psum - p_hi.astype(_F32)).astype(_BF16)
    cst = cst_ref[...]
    imp = (jnp.dot(cst, p_hi, preferred_element_type=_F32)
           + jnp.dot(cst, p_lo, preferred_element_type=_F32))
    blk = lax.broadcasted_iota(jnp.int32, (ns, qt), 0)
    tq1 = s0 + lax.broadcasted_iota(jnp.int32, (ns, qt), 1)
    cur = tq1 // NSA_SEL_LEN
    visible = blk * NSA_SEL_LEN <= tq1
    forced = (blk == 0) | (blk == cur) | (blk == cur - 1)
    imp = jnp.where(visible, jnp.where(forced, NSA_FORCE_SCORE, imp), -1.0)
    rank = jnp.zeros((ns, qt), _F32)
    for j in range(ns):
        rowj = imp[j:j + 1, :]
        ge = jnp.where(rowj >= imp, 1.0, 0.0)
        gt = jnp.where(rowj > imp, 1.0, 0.0)
        rank = rank + jnp.where(blk > j, ge, gt)
    bias = jnp.where(rank < n_sel, 0.0, NEG_BIG).astype(_BF16)
    qaug = jnp.concatenate([qs, jnp.concatenate([bias] * grp, axis=1)], axis=0)

    def tile_update(s, v_tile, m, l):
        m_new = jnp.maximum(m, jnp.max(s, axis=0, keepdims=True))
        a = jnp.exp(m - m_new)
        p = jnp.exp(s - m_new)
        acc_ref[...] = a * acc_ref[...] + jnp.dot(v_tile, p.astype(_BF16),
                                                  preferred_element_type=_F32)
        return m_new, a * l + jnp.sum(p, axis=0, keepdims=True)

    m0 = jnp.full((1, lanes), M_INIT, _F32)
    l0 = jnp.zeros((1, lanes), _F32)
    kidx = lax.broadcasted_iota(jnp.int32, (kt, 1), 0)

    acc_ref[...] = jnp.zeros_like(acc_ref)

    def sel_body(j, carry):
        k_tile = kaug_ref[pl.ds(pl.multiple_of(j * kt, kt), kt), :]
        s = jnp.dot(k_tile, qaug, preferred_element_type=_F32)
        return tile_update(s, vst_ref[0, 0, j], *carry)

    m, l = lax.fori_loop(0, i, sel_body, (m0, l0))
    k_tile = kaug_ref[pl.ds(pl.multiple_of(i * kt, kt), kt), :]
    s = jnp.dot(k_tile, qaug, preferred_element_type=_F32)
    s = jnp.where(i * kt + kidx <= tq, s, NEG_BIG)
    m, l = tile_update(s, vst_ref[0, 0, i], m, l)
    o_sel = acc_ref[...] / l

    acc_ref[...] = jnp.zeros_like(acc_ref)

    def win_body(j, carry):
        k_tile = kw_ref[0, 0, pl.ds(pl.multiple_of(j * kt, kt), kt), :]
        s = jnp.dot(k_tile, qs, preferred_element_type=_F32)
        rel = tq - (j * kt + kidx)
        s = jnp.where((rel >= 0) & (rel < NSA_WINDOW), s, NEG_BIG)
        return tile_update(s, vwt_ref[0, 0, j], *carry)

    first = jnp.maximum(i - (NSA_WINDOW + kt - 1) // kt, 0)
    m, l = lax.fori_loop(first, i + 1, win_body, (m0, l0))
    o_win = acc_ref[...] / l

    gate = _sigmoid(gt_ref[0, 0, 0])
    o_ref[0, 0, 0] = gate[0:1, :] * o_cmp + gate[1:2, :] * o_sel + gate[2:3, :] * o_win


def _nsa_attention(qt_arr, k_cmp, v_cmp_t, cst, ks, vs_t, kw, vw_t, gates_t):
    bsz, hk, nq, dh, lanes = qt_arr.shape
    qt = lanes // NSA_GROUP
    kt = qt
    t = ks.shape[2]
    ns = t // NSA_SEL_LEN
    nc = k_cmp.shape[2]
    ntile = t // kt
    per_head = lambda shape: pl.BlockSpec((1, 1) + shape, lambda b, h, i: (b, h) + (0,) * len(shape))
    per_tile = lambda shape: pl.BlockSpec((1, 1, 1) + shape,
                                          lambda b, h, i: (b, h, i) + (0,) * len(shape))
    return pl.pallas_call(
        functools.partial(_nsa_kernel, qt=qt, kt=kt, n_sel=min(NSA_TOPN, ns)),
        grid=(bsz, hk, nq),
        in_specs=[per_tile((dh, lanes)),
                  per_head((nc, dh)), per_head((dh, nc)),
                  pl.BlockSpec((ns, nc), lambda b, h, i: (0, 0)),
                  per_head((t, dh)), per_head((ntile, dh, kt)),
                  per_head((t, dh)), per_head((ntile, dh, kt)),
                  per_tile((3, lanes))],
        out_specs=per_tile((dh, lanes)),
        out_shape=jax.ShapeDtypeStruct((bsz, hk, nq, dh, lanes), _F32),
        scratch_shapes=[pltpu.VMEM((t, dh + ns), _BF16),
                        pltpu.VMEM((dh, lanes), _F32)],
        compiler_params=pltpu.CompilerParams(
            dimension_semantics=("parallel", "parallel", "arbitrary"),
            vmem_limit_bytes=VMEM_LIMIT),
        name="nsa_attn",
    )(qt_arr, k_cmp, v_cmp_t, cst, ks, vs_t, kw, vw_t, gates_t)


def _cmp_to_sel_t(t):
    ncmp = (t - NSA_CMP_LEN) // NSA_CMP_STRIDE + 1
    c_start = np.arange(ncmp) * NSA_CMP_STRIDE
    s_start = np.arange(t // NSA_SEL_LEN) * NSA_SEL_LEN
    overlap = np.clip(np.minimum(c_start[:, None] + NSA_CMP_LEN, s_start[None, :] + NSA_SEL_LEN)
                      - np.maximum(c_start[:, None], s_start[None, :]), 0, None) / NSA_CMP_LEN
    out = np.zeros((t // NSA_SEL_LEN, t // NSA_CMP_STRIDE), np.float32)
    out[:, :ncmp] = overlap.T
    return jnp.asarray(out, _BF16)


def _nsa(nq, nkv, small, pos_k, pos_v, k_w1, k_w2, v_w1, v_w2):
    bsz, t, _ = nq.shape
    hk, grp, dh = NSA_KV_HEADS, NSA_GROUP, NSA_HEAD_DIM
    qt = min(NSA_QT, t)
    nqt = t // qt
    kt = qt

    def kv_heads(idx):
        w = NSA_KV_WIDTH
        return nkv[:, :, idx * w:(idx + 1) * w].reshape(bsz, t, hk, dh).transpose(0, 2, 1, 3)

    def tiles_t(x):
        return x.reshape(bsz, hk, t // kt, kt, dh).transpose(0, 1, 2, 4, 3).astype(_BF16)

    k_cmp, v_cmp = _nsa_compress(kv_heads(0), kv_heads(1), pos_k, pos_v, k_w1, k_w2, v_w1, v_w2)
    q_t = (nq.reshape(bsz, nqt, qt, hk, grp, dh).transpose(0, 3, 1, 5, 4, 2)
           .reshape(bsz, hk, nqt, dh, grp * qt))
    gates_t = (small[:, :, 8:8 + 3 * NSA_Q_HEADS].reshape(bsz, nqt, qt, hk, grp, 3)
               .transpose(0, 3, 1, 5, 4, 2).reshape(bsz, hk, nqt, 3, grp * qt))
    o_t = _nsa_attention(q_t, k_cmp.astype(_BF16), v_cmp.transpose(0, 1, 3, 2).astype(_BF16),
                         _cmp_to_sel_t(t), kv_heads(2).astype(_BF16), tiles_t(kv_heads(3)),
                         kv_heads(4).astype(_BF16), tiles_t(kv_heads(5)), gates_t)
    return (o_t.reshape(bsz, hk, nqt, dh, grp, qt).transpose(0, 2, 5, 1, 4, 3)
            .reshape(bsz, t, NSA_WIDTH))


def _out_ln_kernel(x_ref, a_ref, b_ref, wa_ref, wb_ref, g_ref, beta_ref, o_ref):
    mix = (jnp.dot(a_ref[...].astype(_BF16), wa_ref[...], preferred_element_type=_F32)
           + jnp.dot(b_ref[...].astype(_BF16), wb_ref[...], preferred_element_type=_F32))
    o_ref[...] = _layer_norm(DN_ALPHA * x_ref[...] + mix, g_ref[...], beta_ref[...])


def _out_ln(x, oa, ob, wa, wb, g, b):
    n, d = x.shape
    tm = min(PROJ_ROWS, n)
    row = lambda width: pl.BlockSpec((tm, width), lambda i: (i, 0))
    return pl.pallas_call(
        _out_ln_kernel,
        grid=(n // tm,),
        in_specs=[row(d), row(oa.shape[1]), row(ob.shape[1]), _const_spec(wa.shape),
                  _const_spec(wb.shape), _const_spec((1, d)), _const_spec((1, d))],
        out_specs=row(d),
        out_shape=jax.ShapeDtypeStruct((n, d), _F32),
        compiler_params=pltpu.CompilerParams(dimension_semantics=("parallel",),
                                             vmem_limit_bytes=VMEM_LIMIT),
        name="out_ln",
    )(x, oa, ob, wa, wb, g, b)


_QKV = 3 * GDN_WIDTH
_SPLITS = ((0, _QKV), (_QKV, GDN_WIDTH), (_QKV + GDN_WIDTH, NSA_WIDTH),
           (_QKV + GDN_WIDTH + NSA_WIDTH, 6 * NSA_KV_WIDTH),
           (_QKV + GDN_WIDTH + NSA_WIDTH + 6 * NSA_KV_WIDTH, LANES))


def _reorder_w_in(w):
    d = w.shape[0]
    big = 4 * GDN_WIDTH
    nsa0 = big + 2 * GDN_HEADS
    nsa1 = nsa0 + NSA_WIDTH + 6 * NSA_KV_WIDTH
    small = jnp.concatenate([w[:, big:nsa0], w[:, nsa1:]], axis=1)
    pad = jnp.zeros((d, LANES - small.shape[1]), w.dtype)
    return jnp.concatenate([w[:, :big], w[:, nsa0:nsa1], small, pad], axis=1).astype(_BF16)


def kernel(x, ln1_g, ln1_b, ffn1_wg, ffn1_wu, ffn1_wd, w_in, gdn_conv_w, gdn_a_log, gdn_dt_bias,
           gdn_norm_w, nsa_cmp_pos_k, nsa_cmp_pos_v, nsa_cmp_k_w1, nsa_cmp_k_w2, nsa_cmp_v_w1,
           nsa_cmp_v_w2, w_out, ln2_g, ln2_b, ffn2_wg, ffn2_wu, ffn2_wd, ln3_g, ln3_b):
    bsz, t, d = x.shape
    n = bsz * t
    h = x.reshape(n, d)
    for i in range(DEPTH):
        wgu, wd = _ffn_weights(ffn1_wg[i], ffn1_wu[i], ffn1_wd[i])
        h = _ffn_ln(h, wgu, wd, ln1_g[i].reshape(1, d), ln1_b[i].reshape(1, d))
        qkv, z, nq, nkv, small = _in_proj(h, _reorder_w_in(w_in[i]), _SPLITS)
        small = small.reshape(bsz, t, LANES)
        o_gdn = _gdn(qkv.reshape(bsz, t, _QKV), z.reshape(bsz, t, GDN_WIDTH), small,
                     gdn_conv_w[i], gdn_a_log[i], gdn_dt_bias[i], gdn_norm_w[i])
        o_nsa = _nsa(nq.reshape(bsz, t, NSA_WIDTH), nkv.reshape(bsz, t, 6 * NSA_KV_WIDTH), small,
                     nsa_cmp_pos_k[i], nsa_cmp_pos_v[i], nsa_cmp_k_w1[i], nsa_cmp_k_w2[i],
                     nsa_cmp_v_w1[i], nsa_cmp_v_w2[i])
        wo = w_out[i].astype(_BF16)
        h = _out_ln(h, o_gdn.reshape(n, GDN_WIDTH), o_nsa.reshape(n, NSA_WIDTH),
                    wo[:GDN_WIDTH], wo[GDN_WIDTH:], ln2_g[i].reshape(1, d), ln2_b[i].reshape(1, d))
        wgu, wd = _ffn_weights(ffn2_wg[i], ffn2_wu[i], ffn2_wd[i])
        h = _ffn_ln(h, wgu, wd, ln3_g[i].reshape(1, d), ln3_b[i].reshape(1, d))
    return h.reshape(bsz, t, d)
```

```python
import functools

import numpy as np
import jax
import jax.numpy as jnp
from jax import lax
from jax.experimental import pallas as pl
from jax.experimental.pallas import tpu as pltpu

_F32 = jnp.float32
_BF16 = jnp.bfloat16

GDN_HEADS = 4
GDN_HEAD_DIM = 128
GDN_WIDTH = GDN_HEADS * GDN_HEAD_DIM
GDN_CONV = 4
GDN_CHUNK = 64

NSA_Q_HEADS = 8
NSA_KV_HEADS = 2
NSA_GROUP = NSA_Q_HEADS // NSA_KV_HEADS
NSA_HEAD_DIM = 64
NSA_WIDTH = NSA_Q_HEADS * NSA_HEAD_DIM
NSA_KV_WIDTH = NSA_KV_HEADS * NSA_HEAD_DIM
NSA_CMP_LEN = 32
NSA_CMP_STRIDE = 16
NSA_SEL_LEN = 64
NSA_TOPN = 16
NSA_WINDOW = 512
NSA_FORCE_SCORE = 1e4

LN_EPS = 1e-5
RMS_EPS = 1e-6
L2_EPS = 1e-6
DEPTH = 1
DN_ALPHA = (2 * DEPTH) ** 0.25

NEG_BIG = -(2.0 ** 100)
M_INIT = -3.0e38

LANES = 128
VMEM_LIMIT = 56 * 1024 * 1024

FFN_ROWS = 512
FFN_CHUNK = 256
PROJ_ROWS = 512
GDN_ROWS = 256
NSA_QT = 256
NSA_KT = 256


def _sigmoid(x):
    return 1.0 / (1.0 + jnp.exp(-x))


def _softplus(x):
    return jnp.maximum(x, 0.0) + jnp.log(1.0 + jnp.exp(-jnp.abs(x)))


def _mm(a, b):
    return jnp.dot(a.astype(_BF16), b.astype(_BF16), preferred_element_type=_F32)


def _mm_nt(a, b):
    return lax.dot_general(a.astype(_BF16), b.astype(_BF16), (((1,), (1,)), ((), ())),
                           preferred_element_type=_F32)


def _mm_tn(a, b):
    return lax.dot_general(a.astype(_BF16), b.astype(_BF16), (((0,), (0,)), ((), ())),
                           preferred_element_type=_F32)


def _split3(x):
    hi = x.astype(_BF16)
    r = x - hi.astype(_F32)
    mid = r.astype(_BF16)
    lo = (r - mid.astype(_F32)).astype(_BF16)
    return hi, mid, lo


def _layer_norm(y, g, b):
    mu = jnp.mean(y, axis=-1, keepdims=True)
    yc = y - mu
    var = jnp.mean(yc * yc, axis=-1, keepdims=True)
    return yc * lax.rsqrt(var + LN_EPS) * g + b


def _ffn_ln_kernel(x_ref, wg_ref, wu_ref, wd_ref, g_ref, b_ref, o_ref, h_ref):
    x = x_ref[...]
    xb = x.astype(_BF16)
    d_ff = wg_ref.shape[1]
    for c0 in range(0, d_ff, FFN_CHUNK):
        gate = jnp.dot(xb, wg_ref[:, c0:c0 + FFN_CHUNK], preferred_element_type=_F32)
        up = jnp.dot(xb, wu_ref[:, c0:c0 + FFN_CHUNK], preferred_element_type=_F32)
        h_ref[:, c0:c0 + FFN_CHUNK] = (gate * _sigmoid(gate) * up).astype(_BF16)
    y = DN_ALPHA * x + 0.5 * jnp.dot(h_ref[...], wd_ref[...], preferred_element_type=_F32)
    o_ref[...] = _layer_norm(y, g_ref[...], b_ref[...])


def _const_spec(shape):
    zeros = (0,) * len(shape)
    return pl.BlockSpec(shape, lambda *_: zeros, pipeline_mode=pl.Buffered(1))


def _ffn_ln(x, wg, wu, wd, g, b):
    n, d = x.shape
    d_ff = wg.shape[1]
    tm = min(FFN_ROWS, n)
    return pl.pallas_call(
        _ffn_ln_kernel,
        grid=(n // tm,),
        in_specs=[pl.BlockSpec((tm, d), lambda i: (i, 0)),
                  _const_spec((d, d_ff)), _const_spec((d, d_ff)), _const_spec((d_ff, d)),
                  _const_spec((1, d)), _const_spec((1, d))],
        out_specs=pl.BlockSpec((tm, d), lambda i: (i, 0)),
        out_shape=jax.ShapeDtypeStruct((n, d), _F32),
        scratch_shapes=[pltpu.VMEM((tm, d_ff), _BF16)],
        compiler_params=pltpu.CompilerParams(dimension_semantics=("parallel",),
                                             vmem_limit_bytes=VMEM_LIMIT),
        name="ffn_ln",
    )(x, wg.astype(_BF16), wu.astype(_BF16), wd.astype(_BF16), g, b)


_QKV = 3 * GDN_WIDTH
_NAT_KV = _QKV + GDN_WIDTH + LANES
_T_VS = NSA_WIDTH
_T_VW = _T_VS + NSA_KV_WIDTH
_T_GATE = _T_VW + NSA_KV_WIDTH
_T_ROWS = _T_GATE + 32
Q_SCALE = NSA_HEAD_DIM ** -0.5 * 1.4426950408889634


def _proj_kernel(x_ref, wn_ref, wt_ref, qkv_ref, z_ref, small_ref, kc_ref, vc_ref, ks_ref, kw_ref,
                 qt_ref, vst_ref, vwt_ref, gt_ref, *, qt):
    hk_n, grp, dh = NSA_KV_HEADS, NSA_GROUP, NSA_HEAD_DIM
    xb = x_ref[...].astype(_BF16)
    tm = xb.shape[0]
    qkv_ref[...] = jnp.dot(xb, wn_ref[:, 0:_QKV], preferred_element_type=_F32)
    z_ref[...] = jnp.dot(xb, wn_ref[:, _QKV:_QKV + GDN_WIDTH], preferred_element_type=_F32)
    small_ref[...] = jnp.dot(xb, wn_ref[:, _QKV + GDN_WIDTH:_NAT_KV], preferred_element_type=_F32)
    kv = jnp.dot(xb, wn_ref[:, _NAT_KV:_NAT_KV + 4 * NSA_KV_WIDTH], preferred_element_type=_F32)
    for hk in range(hk_n):
        for idx, ref in enumerate((kc_ref, vc_ref, ks_ref, kw_ref)):
            c0 = idx * NSA_KV_WIDTH + hk * dh
            ref[0, hk] = kv[:, c0:c0 + dh].astype(ref.dtype)
    yt = lax.dot_general(wt_ref[...], xb, (((1,), (1,)), ((), ())),
                         preferred_element_type=_F32)
    for hk in range(hk_n):
        for ii in range(tm // qt):
            cols = slice(ii * qt, (ii + 1) * qt)
            for g in range(grp):
                r0 = (hk * grp + g) * dh
                qt_ref[0, hk, ii, :, g * qt:(g + 1) * qt] = (yt[r0:r0 + dh, cols] * Q_SCALE).astype(_BF16)
                for c in range(3):
                    r = _T_GATE + (hk * grp + g) * 3 + c
                    gt_ref[0, hk, ii, c:c + 1, g * qt:(g + 1) * qt] = yt[r:r + 1, cols]
            vst_ref[0, hk, ii] = yt[_T_VS + hk * dh:_T_VS + (hk + 1) * dh, cols].astype(_BF16)
            vwt_ref[0, hk, ii] = yt[_T_VW + hk * dh:_T_VW + (hk + 1) * dh, cols].astype(_BF16)


def _proj_weights(w):
    d = w.shape[0]
    sizes = (GDN_WIDTH,) * 4 + (GDN_HEADS,) * 2 + (NSA_WIDTH,) + (NSA_KV_WIDTH,) * 6 + (3 * NSA_Q_HEADS,)
    off = np.concatenate([[0], np.cumsum(sizes)])
    gq, gk, gv, gz, gb, ga, nq, kc, vc, ks, vs, kw, vw, gate = [
        w[:, int(off[i]):int(off[i + 1])] for i in range(len(sizes))]
    small = jnp.concatenate([gb, ga, jnp.zeros((d, LANES - 2 * GDN_HEADS), w.dtype)], axis=1)
    w_nat = jnp.concatenate([gq, gk, gv, gz, small, kc, vc, ks, kw], axis=1)
    w_t = jnp.concatenate([nq, vs, vw, gate, jnp.zeros((d, _T_ROWS - _T_GATE - gate.shape[1]), w.dtype)],
                          axis=1).T
    return w_nat.astype(_BF16), w_t.astype(_BF16)


def _in_proj(x, w_nat, w_t, bsz, t):
    n, d = x.shape
    hk, grp, dh = NSA_KV_HEADS, NSA_GROUP, NSA_HEAD_DIM
    tm = min(PROJ_ROWS, t)
    qt = min(NSA_QT, t)
    nt = t // tm
    row = lambda width: pl.BlockSpec((tm, width), lambda r: (r, 0))
    head = pl.BlockSpec((1, hk, tm, dh), lambda r: (r // nt, 0, r % nt, 0))
    tile = lambda rows, lanes: pl.BlockSpec((1, hk, tm // qt, rows, lanes),
                                            lambda r: (r // nt, 0, r % nt, 0, 0))
    sds = jax.ShapeDtypeStruct
    return pl.pallas_call(
        functools.partial(_proj_kernel, qt=qt),
        grid=(n // tm,),
        in_specs=[pl.BlockSpec((tm, d), lambda r: (r, 0)), _const_spec(w_nat.shape),
                  _const_spec(w_t.shape)],
        out_specs=[row(_QKV), row(GDN_WIDTH), row(LANES), head, head, head, head,
                   tile(dh, grp * qt), tile(dh, qt), tile(dh, qt), tile(3, grp * qt)],
        out_shape=[sds((n, _QKV), _F32), sds((n, GDN_WIDTH), _F32), sds((n, LANES), _F32),
                   sds((bsz, hk, t, dh), _F32), sds((bsz, hk, t, dh), _F32),
                   sds((bsz, hk, t, dh), _BF16), sds((bsz, hk, t, dh), _BF16),
                   sds((bsz, hk, t // qt, dh, grp * qt), _BF16),
                   sds((bsz, hk, t // qt, dh, qt), _BF16), sds((bsz, hk, t // qt, dh, qt), _BF16),
                   sds((bsz, hk, t // qt, 3, grp * qt), _F32)],
        compiler_params=pltpu.CompilerParams(dimension_semantics=("parallel",),
                                             vmem_limit_bytes=VMEM_LIMIT),
        name="in_proj",
    )(x, w_nat, w_t)


def _bdot(a, b):
    return jnp.dot(a, b, preferred_element_type=_F32)


def _unit_lower_inverses(lmats):
    c = lmats[0].shape[0]
    eye = jnp.where(lax.broadcasted_iota(jnp.int32, (c, c), 0)
                    == lax.broadcasted_iota(jnp.int32, (c, c), 1), 1.0, 0.0)
    rs = [eye - l for l in lmats]
    pb = [l.astype(_BF16) for l in lmats]
    ps = [_bdot(l, l) for l in pb]
    span = 2
    while True:
        pb = [p.astype(_BF16) for p in ps]
        rs = [r + _bdot(r.astype(_BF16), p) for r, p in zip(rs, pb)]
        span *= 2
        if span >= c:
            return rs
        ps = [_bdot(p, p) for p in pb]


def _gdn_kernel(qkv_ref, z_ref, bac_ref, bar_ref, cw_ref, hpr_ref, hpc_ref, nw_ref, o_ref,
                xs_ref, act_ref, s_ref, *, tb):
    nh, hd, c = GDN_HEADS, GDN_HEAD_DIM, GDN_CHUNK
    width = 3 * nh * hd

    @pl.when(pl.program_id(1) == 0)
    def _():
        xs_ref[0:8, :] = jnp.zeros((8, width), _F32)
        s_ref[...] = jnp.zeros_like(s_ref)

    xs_ref[8:8 + tb, :] = qkv_ref[0]
    conv = cw_ref[0:1, :] * xs_ref[5:5 + tb, :]
    for j in range(1, GDN_CONV):
        conv = conv + cw_ref[j:j + 1, :] * xs_ref[5 + j:5 + j + tb, :]
    act_ref[...] = conv * _sigmoid(conv)
    xs_ref[0:8, :] = xs_ref[tb:tb + 8, :]

    row = lax.broadcasted_iota(jnp.int32, (c, c), 0)
    col = lax.broadcasted_iota(jnp.int32, (c, c), 1)
    causal = row >= col
    strict = row > col
    tril = jnp.where(causal, 1.0, 0.0).astype(_BF16)
    triu = jnp.where(row <= col, 1.0, 0.0).astype(_BF16)

    nchunk = tb // c
    qg_l, qb_l, kb_l, kbeta_l, rhs_l, kd_l, decay_l, eg_l = [], [], [], [], [], [], [], []
    for ci in range(nchunk):
        r0 = ci * c
        bac = bac_ref[0, r0:r0 + c, :]
        beta_all = _sigmoid(bac)
        g_all = -jnp.exp(hpr_ref[0:1, :]) * _softplus(bac + hpr_ref[1:2, :])
        gc_all = sum(_bdot(tril, part) for part in _split3(g_all))
        bar = bar_ref[0, ci]
        gr_all = -jnp.exp(hpc_ref[:, 0:1]) * _softplus(bar + hpc_ref[:, 1:2])
        gcr_all = sum(_bdot(part, triu) for part in _split3(gr_all))
        for h in range(nh):
            q = act_ref[r0:r0 + c, h * hd:(h + 1) * hd]
            k = act_ref[r0:r0 + c, (nh + h) * hd:(nh + h + 1) * hd]
            v = act_ref[r0:r0 + c, (2 * nh + h) * hd:(2 * nh + h + 1) * hd]
            q = q * lax.rsqrt(jnp.sum(q * q, axis=-1, keepdims=True) + L2_EPS) * (hd ** -0.5)
            k = k * lax.rsqrt(jnp.sum(k * k, axis=-1, keepdims=True) + L2_EPS)
            beta = beta_all[:, h:h + 1]
            gcc = gc_all[:, nh + h:nh + h + 1]
            gcr = gcr_all[nh + h:nh + h + 1, :]
            glast = gc_all[c - 1:c, nh + h:nh + h + 1]
            decay_l.append(jnp.exp(jnp.where(causal, gcc - gcr, NEG_BIG)))
            kbeta = k * beta
            egc = jnp.exp(gcc)
            qg_l.append(q * egc)
            qb_l.append(q.astype(_BF16))
            kb_l.append(k.astype(_BF16))
            kbeta_l.append(kbeta.astype(_BF16))
            rhs_l.append(jnp.concatenate([kbeta * egc, v * beta], axis=1).astype(_BF16))
            kd_l.append((k * jnp.exp(glast - gcc)).astype(_BF16))
            eg_l.append(jnp.exp(glast))

    nt = (((1,), (1,)), ((), ()))
    tn = (((0,), (0,)), ((), ()))
    kk_l = [lax.dot_general(a, b, nt, preferred_element_type=_F32) for a, b in zip(kbeta_l, kb_l)]
    qk_l = [lax.dot_general(a, b, nt, preferred_element_type=_F32) for a, b in zip(qb_l, kb_l)]
    lmat_l = [jnp.where(strict, kk * d, 0.0) for kk, d in zip(kk_l, decay_l)]
    tinv_l = _unit_lower_inverses(lmat_l)
    wu_l = [_bdot(t.astype(_BF16), r).astype(_BF16) for t, r in zip(tinv_l, rhs_l)]
    qkb_l = [(qk * d).astype(_BF16) for qk, d in zip(qk_l, decay_l)]
    pn_l = [lax.dot_general(kd, wu, tn, preferred_element_type=_F32) for kd, wu in zip(kd_l, wu_l)]
    ab_l = [_bdot(qk, wu) for qk, wu in zip(qkb_l, wu_l)]
    a_l = [(qg - ab[:, :hd]).astype(_BF16) for qg, ab in zip(qg_l, ab_l)]
    p_l = [pn[:, :hd].astype(_BF16) for pn in pn_l]

    states = [s_ref[h] for h in range(nh)]
    for ci in range(nchunk):
        r0 = ci * c
        sb = [s.astype(_BF16) for s in states]
        outs = [_bdot(a_l[ci * nh + h], sb[h]) + ab_l[ci * nh + h][:, hd:] for h in range(nh)]
        states = [states[h] * eg_l[ci * nh + h] + pn_l[ci * nh + h][:, hd:]
                  - _bdot(p_l[ci * nh + h], sb[h]) for h in range(nh)]
        for h in range(nh):
            o = outs[h]
            zz = z_ref[0, r0:r0 + c, h * hd:(h + 1) * hd]
            o = (o * lax.rsqrt(jnp.mean(o * o, axis=-1, keepdims=True) + RMS_EPS) * nw_ref[...]
                 * (zz * _sigmoid(zz)))
            o_ref[0, r0:r0 + c, h * hd:(h + 1) * hd] = o
    for h in range(nh):
        s_ref[h] = states[h]


def _gdn(qkv, z, small, conv_w, a_log, dt_bias, norm_w):
    bsz, t, width = qkv.shape
    nh, c = GDN_HEADS, GDN_CHUNK
    tb = min(GDN_ROWS, t)
    ba_rows = small[:, :, :8].reshape(bsz, t // c, c, 8).transpose(0, 1, 3, 2)
    hp_row = jnp.zeros((8, LANES), _F32)
    hp_row = hp_row.at[0, nh:2 * nh].set(a_log).at[1, nh:2 * nh].set(dt_bias)
    hp_col = jnp.zeros((8, LANES), _F32)
    hp_col = hp_col.at[nh:2 * nh, 0].set(a_log).at[nh:2 * nh, 1].set(dt_bias)
    return pl.pallas_call(
        functools.partial(_gdn_kernel, tb=tb),
        grid=(bsz, t // tb),
        in_specs=[pl.BlockSpec((1, tb, width), lambda b, i: (b, i, 0)),
                  pl.BlockSpec((1, tb, GDN_WIDTH), lambda b, i: (b, i, 0)),
                  pl.BlockSpec((1, tb, LANES), lambda b, i: (b, i, 0)),
                  pl.BlockSpec((1, tb // c, 8, c), lambda b, i: (b, i, 0, 0)),
                  pl.BlockSpec((GDN_CONV, width), lambda b, i: (0, 0)),
                  pl.BlockSpec((8, LANES), lambda b, i: (0, 0)),
                  pl.BlockSpec((8, LANES), lambda b, i: (0, 0)),
                  pl.BlockSpec((1, GDN_HEAD_DIM), lambda b, i: (0, 0))],
        out_specs=pl.BlockSpec((1, tb, GDN_WIDTH), lambda b, i: (b, i, 0)),
        out_shape=jax.ShapeDtypeStruct((bsz, t, GDN_WIDTH), _F32),
        scratch_shapes=[pltpu.VMEM((tb + 8, width), _F32),
                        pltpu.VMEM((tb, width), _F32),
                        pltpu.VMEM((nh, GDN_HEAD_DIM, GDN_HEAD_DIM), _F32)],
        compiler_params=pltpu.CompilerParams(dimension_semantics=("parallel", "arbitrary"),
                                             vmem_limit_bytes=VMEM_LIMIT),
        name="gdn",
    )(qkv, z, small, ba_rows, conv_w, hp_row, hp_col, norm_w.reshape(1, -1))


def _gelu_tanh(x):
    return 0.5 * x * (1.0 + jnp.tanh(0.7978845608028654 * (x + 0.044715 * (x * x * x))))


def _cmp_kernel(xk_ref, xv_ref, pk_ref, pv_ref, wk1_ref, wk2_ref, wv1_ref, wv2t_ref, ok_ref, ovt_ref):
    def hidden(x, pos_ref, w1_ref):
        n = x.shape[0]
        ya = _mm(x + pos_ref[0:1, :], w1_ref[0])
        yb = _mm(x + pos_ref[1:2, :], w1_ref[1])
        hid = ya + pltpu.roll(yb, n - 1, 0)
        return _gelu_tanh(hid).astype(_BF16)

    ok_ref[0, 0] = _bdot(hidden(xk_ref[0, 0], pk_ref, wk1_ref), wk2_ref[...]).astype(_BF16)
    ovt_ref[0, 0] = lax.dot_general(wv2t_ref[...], hidden(xv_ref[0, 0], pv_ref, wv1_ref),
                                    (((1,), (1,)), ((), ())),
                                    preferred_element_type=_F32).astype(_BF16)


def _nsa_compress(kc, vc, pos_k, pos_v, k_w1, k_w2, v_w1, v_w2):
    bsz, hk, t, dh = kc.shape
    half = NSA_CMP_STRIDE * dh
    nrow = t // NSA_CMP_STRIDE
    hidden = k_w1.shape[1]
    xk = kc.reshape(bsz, hk, nrow, half)
    xv = vc.reshape(bsz, hk, nrow, half)
    x_spec = pl.BlockSpec((1, 1, nrow, half), lambda b, h: (b, h, 0, 0))
    full = lambda shape: pl.BlockSpec(shape, lambda b, h: (0,) * len(shape))
    return pl.pallas_call(
        _cmp_kernel,
        grid=(bsz, hk),
        in_specs=[x_spec, x_spec, full((2, half)), full((2, half)),
                  full((2, half, hidden)), full((hidden, dh)),
                  full((2, half, hidden)), full((dh, hidden))],
        out_specs=[pl.BlockSpec((1, 1, nrow, dh), lambda b, h: (b, h, 0, 0)),
                   pl.BlockSpec((1, 1, dh, nrow), lambda b, h: (b, h, 0, 0))],
        out_shape=[jax.ShapeDtypeStruct((bsz, hk, nrow, dh), _BF16),
                   jax.ShapeDtypeStruct((bsz, hk, dh, nrow), _BF16)],
        compiler_params=pltpu.CompilerParams(dimension_semantics=("parallel", "parallel"),
                                             vmem_limit_bytes=VMEM_LIMIT),
        name="nsa_cmp",
    )(xk, xv, pos_k.reshape(2, half), pos_v.reshape(2, half),
      k_w1.reshape(2, half, hidden).astype(_BF16), k_w2.astype(_BF16),
      v_w1.reshape(2, half, hidden).astype(_BF16), v_w2.T.astype(_BF16))


def _nsa_kernel(q_ref, kc_ref, vct_ref, cst_ref, ks_ref, vst_ref, kw_ref, vwt_ref, gt_ref, o_ref,
                kaug_ref, acc_ref, s_buf, p_buf, *, qt, kt, n_sel):
    assert qt == kt
    grp, dh = NSA_GROUP, NSA_HEAD_DIM
    lanes = grp * qt
    t_all = ks_ref.shape[2]
    ns = t_all // NSA_SEL_LEN
    nc = kc_ref.shape[2]
    i = pl.program_id(2)
    s0 = i * qt

    @pl.when(i == 0)
    def _():
        rblk = lax.broadcasted_iota(jnp.int32, (t_all, ns), 0) // NSA_SEL_LEN
        cblk = lax.broadcasted_iota(jnp.int32, (t_all, ns), 1)
        kaug_ref[:, 0:dh] = ks_ref[0, 0]
        kaug_ref[:, dh:dh + ns] = jnp.where(rblk == cblk, 1.0, 0.0).astype(_BF16)

    lane = lax.broadcasted_iota(jnp.int32, (1, lanes), 1)
    tq = s0 + (lane & (qt - 1))
    qs = q_ref[0, 0, 0]

    sc = jnp.dot(kc_ref[0, 0], qs, preferred_element_type=_F32)
    cend = lax.broadcasted_iota(jnp.int32, (nc, 1), 0) * NSA_CMP_STRIDE + (NSA_CMP_LEN - 1)
    valid = cend <= tq
    scm = jnp.where(valid, sc, NEG_BIG)
    mc = jnp.max(scm, axis=0, keepdims=True)
    ec = jnp.where(valid, jnp.exp2(scm - mc), 0.0)
    lc = jnp.sum(ec, axis=0, keepdims=True)
    pc = ec / jnp.where(lc > 0.0, lc, 1.0)
    o_cmp = jnp.dot(vct_ref[0, 0], pc.astype(_BF16), preferred_element_type=_F32)

    psum = pc[:, 0:qt]
    for g in range(1, grp):
        psum = psum + pc[:, g * qt:(g + 1) * qt]
    p_hi = psum.astype(_BF16)
    p_lo = (psum - p_hi.astype(_F32)).astype(_BF16)
    cst = cst_ref[...]
    imp = (jnp.dot(cst, p_hi, preferred_element_type=_F32)
           + jnp.dot(cst, p_lo, preferred_element_type=_F32))
    blk = lax.broadcasted_iota(jnp.int32, (ns, qt), 0)
    tq1 = s0 + lax.broadcasted_iota(jnp.int32, (ns, qt), 1)
    cur = tq1 // NSA_SEL_LEN
    visible = blk * NSA_SEL_LEN <= tq1
    forced = (blk == 0) | (blk == cur) | (blk == cur - 1)
    imp = jnp.where(visible, jnp.where(forced, NSA_FORCE_SCORE, imp), -1.0)
    sub = lax.broadcasted_iota(jnp.int32, (8, qt), 0)
    imp_rows = [imp[8 * r:8 * r + 8, :] for r in range(ns // 8)]
    ranks = [jnp.zeros((8, qt), _F32) for _ in imp_rows]
    for j in range(ns):
        rowj = imp[j:j + 1, :]
        for r, rows in enumerate(imp_rows):
            if 8 * r > j:
                beats = jnp.where(rowj >= rows, 1.0, 0.0)
            elif 8 * r + 7 <= j:
                beats = jnp.where(rowj > rows, 1.0, 0.0)
            else:
                beats = jnp.where(sub + 8 * r > j, jnp.where(rowj >= rows, 1.0, 0.0),
                                  jnp.where(rowj > rows, 1.0, 0.0))
            ranks[r] = ranks[r] + beats
    rank = jnp.concatenate(ranks, axis=0)
    bias = jnp.where(rank < n_sel, 0.0, NEG_BIG).astype(_BF16)
    qaug = jnp.concatenate([qs, jnp.concatenate([bias] * grp, axis=1)], axis=0)

    m0 = jnp.full((1, lanes), M_INIT, _F32)
    l0 = jnp.zeros((1, lanes), _F32)
    kidx = lax.broadcasted_iota(jnp.int32, (kt, 1), 0)

    def qk_tile(j):
        k_tile = kaug_ref[pl.ds(pl.multiple_of(j * kt, kt), kt), :]
        return jnp.dot(k_tile, qaug, preferred_element_type=_F32)

    def softmax_step(s, m, l):
        m_new = jnp.maximum(m, jnp.max(s, axis=0, keepdims=True))
        a = jnp.exp2(m - m_new)
        p = jnp.exp2(s - m_new)
        return m_new, a * l + jnp.sum(p, axis=0, keepdims=True), a, p.astype(_BF16)

    acc_ref[...] = jnp.zeros_like(acc_ref)
    p_buf[1] = jnp.zeros((kt, lanes), _BF16)
    s_buf[0] = qk_tile(0)

    def sel_step(j, slot, carry):
        m, l, a_prev = carry
        s_buf[1 - slot] = qk_tile(j + 1)
        pv = jnp.dot(vst_ref[0, 0, jnp.maximum(j - 1, 0)], p_buf[1 - slot],
                     preferred_element_type=_F32)
        m, l, a, p = softmax_step(s_buf[slot], m, l)
        p_buf[slot] = p
        acc_ref[...] = a_prev * acc_ref[...] + pv
        return m, l, a

    def sel_pair(jj, carry):
        return sel_step(2 * jj + 1, 1, sel_step(2 * jj, 0, carry))

    carry = lax.fori_loop(0, i // 2, sel_pair, (m0, l0, jnp.ones((1, lanes), _F32)))
    m, l, a_prev = lax.cond((i & 1) == 1, lambda c: sel_step(i - 1, 0, c), lambda c: c, carry)
    cur = i & 1
    pv = jnp.dot(vst_ref[0, 0, jnp.maximum(i - 1, 0)], p_buf[1 - cur], preferred_element_type=_F32)
    acc = a_prev * acc_ref[...] + pv
    s = jnp.where(tq - i * kt >= kidx, s_buf[cur], NEG_BIG)
    m, l, a, p = softmax_step(s, m, l)
    acc = a * acc + jnp.dot(vst_ref[0, 0, i], p, preferred_element_type=_F32)
    o_sel = acc / l

    nwt = (NSA_WINDOW + kt - 1) // kt + 1
    s_win, j_win = [], []
    for d in range(nwt):
        back = nwt - 1 - d
        j = i - back
        jc = jnp.maximum(j, 0)
        k_tile = kw_ref[0, 0, pl.ds(pl.multiple_of(jc * kt, kt), kt), :]
        s = jnp.dot(k_tile, qs, preferred_element_type=_F32)
        rel = (tq - j * kt + jnp.where(j >= 0, 0, 2 * NSA_WINDOW + 2 * kt)) - kidx
        if back == 0:
            s = jnp.where(rel >= 0, s, NEG_BIG)
        elif back * kt + kt - 1 >= NSA_WINDOW:
            s = jnp.where(rel < NSA_WINDOW, s, NEG_BIG)
        else:
            s = s + jnp.where(j >= 0, 0.0, NEG_BIG)
        s_win.append(s)
        j_win.append(jc)
    mw = s_win[0].max(axis=0, keepdims=True)
    for s in s_win[1:]:
        mw = jnp.maximum(mw, s.max(axis=0, keepdims=True))
    lw = jnp.zeros((1, lanes), _F32)
    o_win = jnp.zeros((dh, lanes), _F32)
    for s, jc in zip(s_win, j_win):
        p = jnp.exp2(s - mw)
        lw = lw + jnp.sum(p, axis=0, keepdims=True)
        o_win = o_win + jnp.dot(vwt_ref[0, 0, jc], p.astype(_BF16), preferred_element_type=_F32)
    o_win = o_win / lw

    gate = _sigmoid(gt_ref[0, 0, 0])
    out_t = gate[0:1, :] * o_cmp + gate[1:2, :] * o_sel + gate[2:3, :] * o_win
    o_ref[0] = jnp.concatenate([out_t[:, g * qt:(g + 1) * qt] for g in range(grp)], axis=0).T


def _nsa_attention(qt_arr, k_cmp, v_cmp_t, cst, ks, vs_t, kw, vw_t, gates_t):
    bsz, hk, nq, dh, lanes = qt_arr.shape
    qt = lanes // NSA_GROUP
    kt = qt
    t = ks.shape[2]
    ns = t // NSA_SEL_LEN
    nc = k_cmp.shape[2]
    ntile = t // kt
    per_head = lambda shape: pl.BlockSpec((1, 1) + shape, lambda b, h, i: (b, h) + (0,) * len(shape))
    per_tile = lambda shape: pl.BlockSpec((1, 1, 1) + shape,
                                          lambda b, h, i: (b, h, i) + (0,) * len(shape))
    return pl.pallas_call(
        functools.partial(_nsa_kernel, qt=qt, kt=kt, n_sel=min(NSA_TOPN, ns)),
        grid=(bsz, hk, nq),
        in_specs=[per_tile((dh, lanes)),
                  per_head((nc, dh)), per_head((dh, nc)),
                  pl.BlockSpec((ns, nc), lambda b, h, i: (0, 0)),
                  per_head((t, dh)), per_head((ntile, dh, kt)),
                  per_head((t, dh)), per_head((ntile, dh, kt)),
                  per_tile((3, lanes))],
        out_specs=pl.BlockSpec((1, qt, NSA_GROUP * dh), lambda b, h, i: (b, i, h)),
        out_shape=jax.ShapeDtypeStruct((bsz, t, hk * NSA_GROUP * dh), _F32),
        scratch_shapes=[pltpu.VMEM((t, dh + ns), _BF16),
                        pltpu.VMEM((dh, lanes), _F32),
                        pltpu.VMEM((2, kt, lanes), _F32),
                        pltpu.VMEM((2, kt, lanes), _BF16)],
        compiler_params=pltpu.CompilerParams(
            dimension_semantics=("parallel", "parallel", "arbitrary"),
            vmem_limit_bytes=VMEM_LIMIT),
        name="nsa_attn",
    )(qt_arr, k_cmp, v_cmp_t, cst, ks, vs_t, kw, vw_t, gates_t)


def _cmp_to_sel_t(t):
    ncmp = (t - NSA_CMP_LEN) // NSA_CMP_STRIDE + 1
    c_start = np.arange(ncmp) * NSA_CMP_STRIDE
    s_start = np.arange(t // NSA_SEL_LEN) * NSA_SEL_LEN
    overlap = np.clip(np.minimum(c_start[:, None] + NSA_CMP_LEN, s_start[None, :] + NSA_SEL_LEN)
                      - np.maximum(c_start[:, None], s_start[None, :]), 0, None) / NSA_CMP_LEN
    out = np.zeros((t // NSA_SEL_LEN, t // NSA_CMP_STRIDE), np.float32)
    out[:, :ncmp] = overlap.T
    return jnp.asarray(out, _BF16)


def _nsa(q_t, kc, vc, ks, kw, vs_t, vw_t, gates_t, pos_k, pos_v, k_w1, k_w2, v_w1, v_w2):
    k_cmp, v_cmp_t = _nsa_compress(kc, vc, pos_k, pos_v, k_w1, k_w2, v_w1, v_w2)
    return _nsa_attention(q_t, k_cmp, v_cmp_t, _cmp_to_sel_t(ks.shape[2]), ks, vs_t, kw, vw_t, gates_t)


def _out_ln_kernel(x_ref, a_ref, b_ref, wa_ref, wb_ref, g_ref, beta_ref, o_ref):
    mix = (jnp.dot(a_ref[...].astype(_BF16), wa_ref[...], preferred_element_type=_F32)
           + jnp.dot(b_ref[...].astype(_BF16), wb_ref[...], preferred_element_type=_F32))
    o_ref[...] = _layer_norm(DN_ALPHA * x_ref[...] + mix, g_ref[...], beta_ref[...])


def _out_ln(x, oa, ob, wa, wb, g, b):
    n, d = x.shape
    tm = min(PROJ_ROWS, n)
    row = lambda width: pl.BlockSpec((tm, width), lambda i: (i, 0))
    return pl.pallas_call(
        _out_ln_kernel,
        grid=(n // tm,),
        in_specs=[row(d), row(oa.shape[1]), row(ob.shape[1]), _const_spec(wa.shape),
                  _const_spec(wb.shape), _const_spec((1, d)), _const_spec((1, d))],
        out_specs=row(d),
        out_shape=jax.ShapeDtypeStruct((n, d), _F32),
        compiler_params=pltpu.CompilerParams(dimension_semantics=("parallel",),
                                             vmem_limit_bytes=VMEM_LIMIT),
        name="out_ln",
    )(x, oa, ob, wa, wb, g, b)


def kernel(x, ln1_g, ln1_b, ffn1_wg, ffn1_wu, ffn1_wd, w_in, gdn_conv_w, gdn_a_log, gdn_dt_bias,
           gdn_norm_w, nsa_cmp_pos_k, nsa_cmp_pos_v, nsa_cmp_k_w1, nsa_cmp_k_w2, nsa_cmp_v_w1,
           nsa_cmp_v_w2, w_out, ln2_g, ln2_b, ffn2_wg, ffn2_wu, ffn2_wd, ln3_g, ln3_b):
    bsz, t, d = x.shape
    n = bsz * t
    h = x.reshape(n, d)
    for i in range(DEPTH):
        h = _ffn_ln(h, ffn1_wg[i], ffn1_wu[i], ffn1_wd[i], ln1_g[i].reshape(1, d), ln1_b[i].reshape(1, d))
        w_nat, w_t = _proj_weights(w_in[i])
        qkv, z, small, kc, vc, ks, kw, q_t, vs_t, vw_t, gates_t = _in_proj(h, w_nat, w_t, bsz, t)
        o_gdn = _gdn(qkv.reshape(bsz, t, _QKV), z.reshape(bsz, t, GDN_WIDTH),
                     small.reshape(bsz, t, LANES),
                     gdn_conv_w[i], gdn_a_log[i], gdn_dt_bias[i], gdn_norm_w[i])
        o_nsa = _nsa(q_t, kc, vc, ks, kw, vs_t, vw_t, gates_t,
                     nsa_cmp_pos_k[i], nsa_cmp_pos_v[i], nsa_cmp_k_w1[i], nsa_cmp_k_w2[i],
                     nsa_cmp_v_w1[i], nsa_cmp_v_w2[i])
        wo = w_out[i].astype(_BF16)
        h = _out_ln(h, o_gdn.reshape(n, GDN_WIDTH), o_nsa.reshape(n, NSA_WIDTH),
                    wo[:GDN_WIDTH], wo[GDN_WIDTH:], ln2_g[i].reshape(1, d), ln2_b[i].reshape(1, d))
        h = _ffn_ln(h, ffn2_wg[i], ffn2_wu[i], ffn2_wd[i], ln3_g[i].reshape(1, d), ln3_b[i].reshape(1, d))
    return h.reshape(bsz, t, d)
```

```python
import functools

import numpy as np
import jax
import jax.numpy as jnp
from jax import lax
from jax.experimental import pallas as pl
from jax.experimental.pallas import tpu as pltpu

_F32 = jnp.float32
_BF16 = jnp.bfloat16

GDN_HEADS = 4
GDN_HEAD_DIM = 128
GDN_WIDTH = GDN_HEADS * GDN_HEAD_DIM
GDN_CONV = 4
GDN_CHUNK = 64

NSA_Q_HEADS = 8
NSA_KV_HEADS = 2
NSA_GROUP = NSA_Q_HEADS // NSA_KV_HEADS
NSA_HEAD_DIM = 64
NSA_WIDTH = NSA_Q_HEADS * NSA_HEAD_DIM
NSA_KV_WIDTH = NSA_KV_HEADS * NSA_HEAD_DIM
NSA_CMP_LEN = 32
NSA_CMP_STRIDE = 16
NSA_SEL_LEN = 64
NSA_TOPN = 16
NSA_WINDOW = 512
NSA_FORCE_SCORE = 1e4

LN_EPS = 1e-5
RMS_EPS = 1e-6
L2_EPS = 1e-6
DEPTH = 1
DN_ALPHA = (2 * DEPTH) ** 0.25

NEG_BIG = -(2.0 ** 100)
M_INIT = -3.0e38

LANES = 128
VMEM_LIMIT = 56 * 1024 * 1024

FFN_ROWS = 512
FFN_CHUNK = 256
PROJ_ROWS = 512
GDN_ROWS = 256
NSA_QT = 256
NSA_KT = 256


def _sigmoid(x):
    return 0.5 * jnp.tanh(0.5 * x) + 0.5


def _silu(x):
    h = 0.5 * x
    return h * jnp.tanh(h) + h


def _softplus(x):
    return jnp.maximum(x, 0.0) + jnp.log(1.0 + jnp.exp(-jnp.abs(x)))


def _mm(a, b):
    return jnp.dot(a.astype(_BF16), b.astype(_BF16), preferred_element_type=_F32)


def _split3(x):
    hi = x.astype(_BF16)
    r = x - hi.astype(_F32)
    mid = r.astype(_BF16)
    lo = (r - mid.astype(_F32)).astype(_BF16)
    return hi, mid, lo


def _layer_norm(y, g, b):
    mu = jnp.mean(y, axis=-1, keepdims=True)
    yc = y - mu
    var = jnp.mean(yc * yc, axis=-1, keepdims=True)
    return yc * lax.rsqrt(var + LN_EPS) * g + b


def _ffn_ln_kernel(x_ref, wg_ref, wu_ref, wd_ref, g_ref, b_ref, o_ref, h_ref):
    x = x_ref[...]
    xb = x.astype(_BF16)
    d_ff = wg_ref.shape[1]
    for c0 in range(0, d_ff, FFN_CHUNK):
        gate = jnp.dot(xb, wg_ref[:, c0:c0 + FFN_CHUNK], preferred_element_type=_F32)
        up = jnp.dot(xb, wu_ref[:, c0:c0 + FFN_CHUNK], preferred_element_type=_F32)
        h_ref[:, c0:c0 + FFN_CHUNK] = (_silu(gate) * up).astype(_BF16)
    y = DN_ALPHA * x + 0.5 * jnp.dot(h_ref[...], wd_ref[...], preferred_element_type=_F32)
    o_ref[...] = _layer_norm(y, g_ref[...], b_ref[...])


def _const_spec(shape):
    zeros = (0,) * len(shape)
    return pl.BlockSpec(shape, lambda *_: zeros, pipeline_mode=pl.Buffered(1))


def _ffn_ln(x, wg, wu, wd, g, b):
    n, d = x.shape
    d_ff = wg.shape[1]
    tm = min(FFN_ROWS, n)
    return pl.pallas_call(
        _ffn_ln_kernel,
        grid=(n // tm,),
        in_specs=[pl.BlockSpec((tm, d), lambda i: (i, 0)),
                  _const_spec((d, d_ff)), _const_spec((d, d_ff)), _const_spec((d_ff, d)),
                  _const_spec((1, d)), _const_spec((1, d))],
        out_specs=pl.BlockSpec((tm, d), lambda i: (i, 0)),
        out_shape=jax.ShapeDtypeStruct((n, d), _F32),
        scratch_shapes=[pltpu.VMEM((tm, d_ff), _BF16)],
        compiler_params=pltpu.CompilerParams(dimension_semantics=("parallel",),
                                             vmem_limit_bytes=VMEM_LIMIT),
        name="ffn_ln",
    )(x, wg.astype(_BF16), wu.astype(_BF16), wd.astype(_BF16), g, b)


_QKV = 3 * GDN_WIDTH
_NAT_KV = _QKV + GDN_WIDTH + LANES
_T_VS = NSA_WIDTH
_T_VW = _T_VS + NSA_KV_WIDTH
_T_GATE = _T_VW + NSA_KV_WIDTH
_T_ROWS = _T_GATE + 32
Q_SCALE = NSA_HEAD_DIM ** -0.5 * 1.4426950408889634


def _proj_kernel(x_ref, wn_ref, wt_ref, qkv_ref, z_ref, small_ref, kc_ref, vc_ref, ks_ref, kw_ref,
                 qt_ref, vst_ref, vwt_ref, gt_ref, *, qt):
    hk_n, grp, dh = NSA_KV_HEADS, NSA_GROUP, NSA_HEAD_DIM
    xb = x_ref[...].astype(_BF16)
    tm = xb.shape[0]
    qkv_ref[...] = jnp.dot(xb, wn_ref[:, 0:_QKV], preferred_element_type=_F32)
    z_ref[...] = jnp.dot(xb, wn_ref[:, _QKV:_QKV + GDN_WIDTH], preferred_element_type=_F32)
    small_ref[...] = jnp.dot(xb, wn_ref[:, _QKV + GDN_WIDTH:_NAT_KV], preferred_element_type=_F32)
    kv = jnp.dot(xb, wn_ref[:, _NAT_KV:_NAT_KV + 4 * NSA_KV_WIDTH], preferred_element_type=_F32)
    for hk in range(hk_n):
        for idx, ref in enumerate((kc_ref, vc_ref, ks_ref, kw_ref)):
            c0 = idx * NSA_KV_WIDTH + hk * dh
            ref[0, hk] = kv[:, c0:c0 + dh].astype(ref.dtype)
    yt = lax.dot_general(wt_ref[...], xb, (((1,), (1,)), ((), ())),
                         preferred_element_type=_F32)
    for hk in range(hk_n):
        for ii in range(tm // qt):
            cols = slice(ii * qt, (ii + 1) * qt)
            for g in range(grp):
                r0 = (hk * grp + g) * dh
                qt_ref[0, hk, ii, :, g * qt:(g + 1) * qt] = (yt[r0:r0 + dh, cols] * Q_SCALE).astype(_BF16)
                for c in range(3):
                    r = _T_GATE + (hk * grp + g) * 3 + c
                    gt_ref[0, hk, ii, c:c + 1, g * qt:(g + 1) * qt] = yt[r:r + 1, cols]
            vst_ref[0, hk, ii] = yt[_T_VS + hk * dh:_T_VS + (hk + 1) * dh, cols].astype(_BF16)
            vwt_ref[0, hk, ii] = yt[_T_VW + hk * dh:_T_VW + (hk + 1) * dh, cols].astype(_BF16)


def _proj_weights(w):
    d = w.shape[0]
    sizes = (GDN_WIDTH,) * 4 + (GDN_HEADS,) * 2 + (NSA_WIDTH,) + (NSA_KV_WIDTH,) * 6 + (3 * NSA_Q_HEADS,)
    off = np.concatenate([[0], np.cumsum(sizes)])
    gq, gk, gv, gz, gb, ga, nq, kc, vc, ks, vs, kw, vw, gate = [
        w[:, int(off[i]):int(off[i + 1])] for i in range(len(sizes))]
    small = jnp.concatenate([gb, ga, jnp.zeros((d, LANES - 2 * GDN_HEADS), w.dtype)], axis=1)
    w_nat = jnp.concatenate([gq, gk, gv, gz, small, kc, vc, ks, kw], axis=1)
    w_t = jnp.concatenate([nq, vs, vw, gate, jnp.zeros((d, _T_ROWS - _T_GATE - gate.shape[1]), w.dtype)],
                          axis=1).T
    return w_nat.astype(_BF16), w_t.astype(_BF16)


def _in_proj(x, w_nat, w_t, bsz, t):
    n, d = x.shape
    hk, grp, dh = NSA_KV_HEADS, NSA_GROUP, NSA_HEAD_DIM
    tm = min(PROJ_ROWS, t)
    qt = min(NSA_QT, t)
    nt = t // tm
    row = lambda width: pl.BlockSpec((tm, width), lambda r: (r, 0))
    head = pl.BlockSpec((1, hk, tm, dh), lambda r: (r // nt, 0, r % nt, 0))
    tile = lambda rows, lanes: pl.BlockSpec((1, hk, tm // qt, rows, lanes),
                                            lambda r: (r // nt, 0, r % nt, 0, 0))
    sds = jax.ShapeDtypeStruct
    return pl.pallas_call(
        functools.partial(_proj_kernel, qt=qt),
        grid=(n // tm,),
        in_specs=[pl.BlockSpec((tm, d), lambda r: (r, 0)), _const_spec(w_nat.shape),
                  _const_spec(w_t.shape)],
        out_specs=[row(_QKV), row(GDN_WIDTH), row(LANES), head, head, head, head,
                   tile(dh, grp * qt), tile(dh, qt), tile(dh, qt), tile(3, grp * qt)],
        out_shape=[sds((n, _QKV), _F32), sds((n, GDN_WIDTH), _F32), sds((n, LANES), _F32),
                   sds((bsz, hk, t, dh), _F32), sds((bsz, hk, t, dh), _F32),
                   sds((bsz, hk, t, dh), _BF16), sds((bsz, hk, t, dh), _BF16),
                   sds((bsz, hk, t // qt, dh, grp * qt), _BF16),
                   sds((bsz, hk, t // qt, dh, qt), _BF16), sds((bsz, hk, t // qt, dh, qt), _BF16),
                   sds((bsz, hk, t // qt, 3, grp * qt), _F32)],
        compiler_params=pltpu.CompilerParams(dimension_semantics=("parallel",),
                                             vmem_limit_bytes=VMEM_LIMIT),
        name="in_proj",
    )(x, w_nat, w_t)


def _bdot(a, b):
    return jnp.dot(a, b, preferred_element_type=_F32)


def _unit_lower_inverses(lmats):
    c = lmats[0].shape[0]
    eye = jnp.where(lax.broadcasted_iota(jnp.int32, (c, c), 0)
                    == lax.broadcasted_iota(jnp.int32, (c, c), 1), 1.0, 0.0)
    rs = [eye - l for l in lmats]
    pb = [l.astype(_BF16) for l in lmats]
    ps = [_bdot(l, l) for l in pb]
    span = 2
    while True:
        pb = [p.astype(_BF16) for p in ps]
        rs = [r + _bdot(r.astype(_BF16), p) for r, p in zip(rs, pb)]
        span *= 2
        if span >= c:
            return rs
        ps = [_bdot(p, p) for p in pb]


def _gdn_kernel(qkv_ref, z_ref, bac_ref, bar_ref, cw_ref, hpr_ref, hpc_ref, nw_ref, o_ref,
                xs_ref, act_ref, s_ref, *, tb):
    nh, hd, c = GDN_HEADS, GDN_HEAD_DIM, GDN_CHUNK
    width = 3 * nh * hd

    @pl.when(pl.program_id(1) == 0)
    def _():
        xs_ref[0:8, :] = jnp.zeros((8, width), _F32)
        s_ref[...] = jnp.zeros_like(s_ref)

    x_cur = qkv_ref[0]
    xs_ref[8:8 + tb, :] = x_cur
    xs = xs_ref[...]
    conv = cw_ref[GDN_CONV - 1:GDN_CONV, :] * x_cur
    for d in range(1, GDN_CONV):
        conv = conv + cw_ref[GDN_CONV - 1 - d:GDN_CONV - d, :] * pltpu.roll(xs, d, 0)[8:, :]
    act_ref[...] = _silu(conv)
    xs_ref[0:8, :] = x_cur[tb - 8:, :]

    row = lax.broadcasted_iota(jnp.int32, (c, c), 0)
    col = lax.broadcasted_iota(jnp.int32, (c, c), 1)
    causal = row >= col
    strict = row > col
    tril = jnp.where(causal, 1.0, 0.0).astype(_BF16)
    triu = jnp.where(row <= col, 1.0, 0.0).astype(_BF16)

    nchunk = tb // c
    qg_l, qb_l, kb_l, kbeta_l, rhs_l, kd_l, decay_l, eg_l = [], [], [], [], [], [], [], []
    for ci in range(nchunk):
        r0 = ci * c
        bac = bac_ref[0, r0:r0 + c, :]
        beta_all = _sigmoid(bac)
        g_all = -jnp.exp(hpr_ref[0:1, :]) * _softplus(bac + hpr_ref[1:2, :])
        gc_all = sum(_bdot(tril, part) for part in _split3(g_all))
        bar = bar_ref[0, ci]
        gr_all = -jnp.exp(hpc_ref[:, 0:1]) * _softplus(bar + hpc_ref[:, 1:2])
        gcr_all = sum(_bdot(part, triu) for part in _split3(gr_all))
        for h in range(nh):
            q = act_ref[r0:r0 + c, h * hd:(h + 1) * hd]
            k = act_ref[r0:r0 + c, (nh + h) * hd:(nh + h + 1) * hd]
            v = act_ref[r0:r0 + c, (2 * nh + h) * hd:(2 * nh + h + 1) * hd]
            q = q * lax.rsqrt(jnp.sum(q * q, axis=-1, keepdims=True) + L2_EPS) * (hd ** -0.5)
            k = k * lax.rsqrt(jnp.sum(k * k, axis=-1, keepdims=True) + L2_EPS)
            beta = beta_all[:, h:h + 1]
            gcc = gc_all[:, nh + h:nh + h + 1]
            gcr = gcr_all[nh + h:nh + h + 1, :]
            glast = gc_all[c - 1:c, nh + h:nh + h + 1]
            decay_l.append(jnp.exp(jnp.where(causal, gcc - gcr, NEG_BIG)))
            kbeta = k * beta
            egc = jnp.exp(gcc)
            qg_l.append(q * egc)
            qb_l.append(q.astype(_BF16))
            kb_l.append(k.astype(_BF16))
            kbeta_l.append(kbeta.astype(_BF16))
            rhs_l.append(jnp.concatenate([kbeta * egc, v * beta], axis=1).astype(_BF16))
            kd_l.append((k * jnp.exp(glast - gcc)).astype(_BF16))
            eg_l.append(jnp.exp(glast))

    nt = (((1,), (1,)), ((), ()))
    tn = (((0,), (0,)), ((), ()))
    kk_l = [lax.dot_general(a, b, nt, preferred_element_type=_F32) for a, b in zip(kbeta_l, kb_l)]
    qk_l = [lax.dot_general(a, b, nt, preferred_element_type=_F32) for a, b in zip(qb_l, kb_l)]
    lmat_l = [jnp.where(strict, kk * d, 0.0) for kk, d in zip(kk_l, decay_l)]
    tinv_l = _unit_lower_inverses(lmat_l)
    wu_l = [_bdot(t.astype(_BF16), r).astype(_BF16) for t, r in zip(tinv_l, rhs_l)]
    qkb_l = [(qk * d).astype(_BF16) for qk, d in zip(qk_l, decay_l)]
    pn_l = [lax.dot_general(kd, wu, tn, preferred_element_type=_F32) for kd, wu in zip(kd_l, wu_l)]
    ab_l = [_bdot(qk, wu) for qk, wu in zip(qkb_l, wu_l)]
    a_l = [(qg - ab[:, :hd]).astype(_BF16) for qg, ab in zip(qg_l, ab_l)]
    p_l = [pn[:, :hd].astype(_BF16) for pn in pn_l]

    states = [s_ref[h] for h in range(nh)]
    for ci in range(nchunk):
        r0 = ci * c
        sb = [s.astype(_BF16) for s in states]
        outs = [_bdot(a_l[ci * nh + h], sb[h]) + ab_l[ci * nh + h][:, hd:] for h in range(nh)]
        states = [states[h] * eg_l[ci * nh + h] + pn_l[ci * nh + h][:, hd:]
                  - _bdot(p_l[ci * nh + h], sb[h]) for h in range(nh)]
        for h in range(nh):
            o = outs[h]
            zz = z_ref[0, r0:r0 + c, h * hd:(h + 1) * hd]
            o = (o * lax.rsqrt(jnp.mean(o * o, axis=-1, keepdims=True) + RMS_EPS) * nw_ref[...]
                 * _silu(zz))
            o_ref[0, r0:r0 + c, h * hd:(h + 1) * hd] = o
    for h in range(nh):
        s_ref[h] = states[h]


def _gdn(qkv, z, small, conv_w, a_log, dt_bias, norm_w):
    bsz, t, width = qkv.shape
    nh, c = GDN_HEADS, GDN_CHUNK
    tb = min(GDN_ROWS, t)
    ba_rows = small[:, :, :8].reshape(bsz, t // c, c, 8).transpose(0, 1, 3, 2)
    hp_row = jnp.zeros((8, LANES), _F32)
    hp_row = hp_row.at[0, nh:2 * nh].set(a_log).at[1, nh:2 * nh].set(dt_bias)
    hp_col = jnp.zeros((8, LANES), _F32)
    hp_col = hp_col.at[nh:2 * nh, 0].set(a_log).at[nh:2 * nh, 1].set(dt_bias)
    return pl.pallas_call(
        functools.partial(_gdn_kernel, tb=tb),
        grid=(bsz, t // tb),
        in_specs=[pl.BlockSpec((1, tb, width), lambda b, i: (b, i, 0)),
                  pl.BlockSpec((1, tb, GDN_WIDTH), lambda b, i: (b, i, 0)),
                  pl.BlockSpec((1, tb, LANES), lambda b, i: (b, i, 0)),
                  pl.BlockSpec((1, tb // c, 8, c), lambda b, i: (b, i, 0, 0)),
                  pl.BlockSpec((GDN_CONV, width), lambda b, i: (0, 0)),
                  pl.BlockSpec((8, LANES), lambda b, i: (0, 0)),
                  pl.BlockSpec((8, LANES), lambda b, i: (0, 0)),
                  pl.BlockSpec((1, GDN_HEAD_DIM), lambda b, i: (0, 0))],
        out_specs=pl.BlockSpec((1, tb, GDN_WIDTH), lambda b, i: (b, i, 0)),
        out_shape=jax.ShapeDtypeStruct((bsz, t, GDN_WIDTH), _F32),
        scratch_shapes=[pltpu.VMEM((tb + 8, width), _F32),
                        pltpu.VMEM((tb, width), _F32),
                        pltpu.VMEM((nh, GDN_HEAD_DIM, GDN_HEAD_DIM), _F32)],
        compiler_params=pltpu.CompilerParams(dimension_semantics=("parallel", "arbitrary"),
                                             vmem_limit_bytes=VMEM_LIMIT),
        name="gdn",
    )(qkv, z, small, ba_rows, conv_w, hp_row, hp_col, norm_w.reshape(1, -1))


def _gelu_tanh(x):
    return 0.5 * x * (1.0 + jnp.tanh(0.7978845608028654 * (x + 0.044715 * (x * x * x))))


def _cmp_kernel(xk_ref, xv_ref, pk_ref, pv_ref, wk1_ref, wk2_ref, wv1_ref, wv2t_ref, ok_ref, ovt_ref):
    def hidden(x, pos_ref, w1_ref):
        n = x.shape[0]
        ya = _mm(x + pos_ref[0:1, :], w1_ref[0])
        yb = _mm(x + pos_ref[1:2, :], w1_ref[1])
        hid = ya + pltpu.roll(yb, n - 1, 0)
        return _gelu_tanh(hid).astype(_BF16)

    ok_ref[0, 0] = _bdot(hidden(xk_ref[0, 0], pk_ref, wk1_ref), wk2_ref[...]).astype(_BF16)
    ovt_ref[0, 0] = lax.dot_general(wv2t_ref[...], hidden(xv_ref[0, 0], pv_ref, wv1_ref),
                                    (((1,), (1,)), ((), ())),
                                    preferred_element_type=_F32).astype(_BF16)


def _nsa_compress(kc, vc, pos_k, pos_v, k_w1, k_w2, v_w1, v_w2):
    bsz, hk, t, dh = kc.shape
    half = NSA_CMP_STRIDE * dh
    nrow = t // NSA_CMP_STRIDE
    hidden = k_w1.shape[1]
    xk = kc.reshape(bsz, hk, nrow, half)
    xv = vc.reshape(bsz, hk, nrow, half)
    x_spec = pl.BlockSpec((1, 1, nrow, half), lambda b, h: (b, h, 0, 0))
    full = lambda shape: pl.BlockSpec(shape, lambda b, h: (0,) * len(shape))
    return pl.pallas_call(
        _cmp_kernel,
        grid=(bsz, hk),
        in_specs=[x_spec, x_spec, full((2, half)), full((2, half)),
                  full((2, half, hidden)), full((hidden, dh)),
                  full((2, half, hidden)), full((dh, hidden))],
        out_specs=[pl.BlockSpec((1, 1, nrow, dh), lambda b, h: (b, h, 0, 0)),
                   pl.BlockSpec((1, 1, dh, nrow), lambda b, h: (b, h, 0, 0))],
        out_shape=[jax.ShapeDtypeStruct((bsz, hk, nrow, dh), _BF16),
                   jax.ShapeDtypeStruct((bsz, hk, dh, nrow), _BF16)],
        compiler_params=pltpu.CompilerParams(dimension_semantics=("parallel", "parallel"),
                                             vmem_limit_bytes=VMEM_LIMIT),
        name="nsa_cmp",
    )(xk, xv, pos_k.reshape(2, half), pos_v.reshape(2, half),
      k_w1.reshape(2, half, hidden).astype(_BF16), k_w2.astype(_BF16),
      v_w1.reshape(2, half, hidden).astype(_BF16), v_w2.T.astype(_BF16))


def _nsa_kernel(q_ref, kc_ref, vct_ref, cst_ref, ks_ref, vst_ref, kw_ref, vwt_ref, gt_ref, o_ref,
                kaug_ref, acc_ref, s_buf, p_buf, bias_ref, *, qt, kt, n_sel):
    assert qt == kt
    grp, dh = NSA_GROUP, NSA_HEAD_DIM
    lanes = grp * qt
    t_all = ks_ref.shape[2]
    ns = t_all // NSA_SEL_LEN
    nc = kc_ref.shape[2]
    i = pl.program_id(2)
    s0 = i * qt

    @pl.when(i == 0)
    def _():
        rblk = lax.broadcasted_iota(jnp.int32, (t_all, ns), 0) // NSA_SEL_LEN
        cblk = lax.broadcasted_iota(jnp.int32, (t_all, ns), 1)
        kaug_ref[:, 0:dh] = ks_ref[0, 0]
        kaug_ref[:, dh:dh + ns] = jnp.where(rblk == cblk, 1.0, 0.0).astype(_BF16)
        krow = lax.broadcasted_iota(jnp.int32, (kt, lanes), 0)
        tl = lax.broadcasted_iota(jnp.int32, (kt, lanes), 1) & (qt - 1)
        far_off = ((NSA_WINDOW + kt - 1) // kt) * kt - NSA_WINDOW
        bias_ref[0] = jnp.where(krow <= tl, 0.0, NEG_BIG)
        bias_ref[1] = jnp.where(krow > tl + far_off, 0.0, NEG_BIG)
        bias_ref[2] = jnp.full((kt, lanes), NEG_BIG, _F32)

    lane = lax.broadcasted_iota(jnp.int32, (1, lanes), 1)
    tq = s0 + (lane & (qt - 1))
    qs = q_ref[0, 0, 0]

    sc = jnp.dot(kc_ref[0, 0], qs, preferred_element_type=_F32)
    cend = lax.broadcasted_iota(jnp.int32, (nc, 1), 0) * NSA_CMP_STRIDE + (NSA_CMP_LEN - 1)
    valid = cend <= tq
    scm = jnp.where(valid, sc, NEG_BIG)
    mc = jnp.max(scm, axis=0, keepdims=True)
    ec = jnp.where(valid, jnp.exp2(scm - mc), 0.0)
    lc = jnp.sum(ec, axis=0, keepdims=True)
    pc = ec / jnp.where(lc > 0.0, lc, 1.0)
    o_cmp = jnp.dot(vct_ref[0, 0], pc.astype(_BF16), preferred_element_type=_F32)

    psum = pc[:, 0:qt]
    for g in range(1, grp):
        psum = psum + pc[:, g * qt:(g + 1) * qt]
    p_hi = psum.astype(_BF16)
    p_lo = (psum - p_hi.astype(_F32)).astype(_BF16)
    cst = cst_ref[...]
    imp = (jnp.dot(cst, p_hi, preferred_element_type=_F32)
           + jnp.dot(cst, p_lo, preferred_element_type=_F32))
    blk = lax.broadcasted_iota(jnp.int32, (ns, qt), 0)
    tq1 = s0 + lax.broadcasted_iota(jnp.int32, (ns, qt), 1)
    cur = tq1 // NSA_SEL_LEN
    visible = blk * NSA_SEL_LEN <= tq1
    forced = (blk == 0) | (blk == cur) | (blk == cur - 1)
    imp = jnp.where(visible, jnp.where(forced, NSA_FORCE_SCORE, imp), -1.0)
    sub = lax.broadcasted_iota(jnp.int32, (8, qt), 0)
    imp_rows = [imp[8 * r:8 * r + 8, :] for r in range(ns // 8)]
    ranks = [jnp.zeros((8, qt), _F32) for _ in imp_rows]
    for j in range(ns):
        rowj = imp[j:j + 1, :]
        for r, rows in enumerate(imp_rows):
            if 8 * r > j:
                beats = jnp.where(rowj >= rows, 1.0, 0.0)
            elif 8 * r + 7 <= j:
                beats = jnp.where(rowj > rows, 1.0, 0.0)
            else:
                beats = jnp.where(sub + 8 * r > j, jnp.where(rowj >= rows, 1.0, 0.0),
                                  jnp.where(rowj > rows, 1.0, 0.0))
            ranks[r] = ranks[r] + beats
    rank = jnp.concatenate(ranks, axis=0)
    bias = jnp.where(rank < n_sel, 0.0, NEG_BIG).astype(_BF16)
    qaug = jnp.concatenate([qs, jnp.concatenate([bias] * grp, axis=1)], axis=0)

    m0 = jnp.full((1, lanes), M_INIT, _F32)
    ones_rows = jnp.where(lax.broadcasted_iota(jnp.int32, (acc_ref.shape[0] - dh, kt), 0) == 0,
                          1.0, 0.0).astype(_BF16)

    def v_aug(v_tile):
        return jnp.concatenate([v_tile, ones_rows], axis=0)

    def window_branch():
        nwt = (NSA_WINDOW + kt - 1) // kt + 1
        s_win, j_win = [], []
        for d in range(nwt):
            back = nwt - 1 - d
            j = i - back
            jc = jnp.maximum(j, 0)
            k_tile = kw_ref[0, 0, pl.ds(pl.multiple_of(jc * kt, kt), kt), :]
            s = jnp.dot(k_tile, qs, preferred_element_type=_F32)
            if back == 0:
                s = s + bias_ref[0]
            elif back == nwt - 1:
                s = s + bias_ref[jnp.where(j >= 0, 1, 2)]
            else:
                s = s + jnp.where(j >= 0, 0.0, NEG_BIG)
            s_win.append(s)
            j_win.append(jc)
        mw = s_win[0].max(axis=0, keepdims=True)
        for s in s_win[1:]:
            mw = jnp.maximum(mw, s.max(axis=0, keepdims=True))
        acc_w = jnp.zeros(acc_ref.shape, _F32)
        for s, jc in zip(s_win, j_win):
            acc_w = acc_w + jnp.dot(v_aug(vwt_ref[0, 0, jc]), jnp.exp2((s - mw).astype(_BF16)),
                                    preferred_element_type=_F32)
        return acc_w[0:dh] / acc_w[dh:dh + 1]

    o_win = window_branch()

    def qk_tile(j):
        k_tile = kaug_ref[pl.ds(pl.multiple_of(j * kt, kt), kt), :]
        return jnp.dot(k_tile, qaug, preferred_element_type=_F32)

    def softmax_step(s, m):
        m_new = jnp.maximum(m, jnp.max(s, axis=0, keepdims=True))
        return m_new, jnp.exp2(m - m_new), jnp.exp2((s - m_new).astype(_BF16))

    acc_ref[...] = jnp.zeros_like(acc_ref)
    p_buf[1] = jnp.zeros((kt, lanes), _BF16)
    s_buf[0] = qk_tile(0)

    def sel_step(j, slot, carry):
        m, a_prev = carry
        s_buf[1 - slot] = qk_tile(j + 1)
        pv = jnp.dot(v_aug(vst_ref[0, 0, jnp.maximum(j - 1, 0)]), p_buf[1 - slot],
                     preferred_element_type=_F32)
        m, a, p = softmax_step(s_buf[slot], m)
        p_buf[slot] = p
        acc_ref[...] = a_prev * acc_ref[...] + pv
        return m, a

    def sel_pair(jj, carry):
        return sel_step(2 * jj + 1, 1, sel_step(2 * jj, 0, carry))

    carry = lax.fori_loop(0, i // 2, sel_pair, (m0, jnp.ones((1, lanes), _F32)))
    m, a_prev = lax.cond((i & 1) == 1, lambda c: sel_step(i - 1, 0, c), lambda c: c, carry)
    cur = i & 1
    pv = jnp.dot(v_aug(vst_ref[0, 0, jnp.maximum(i - 1, 0)]), p_buf[1 - cur],
                 preferred_element_type=_F32)
    acc = a_prev * acc_ref[...] + pv
    s = s_buf[cur] + bias_ref[0]
    m, a, p = softmax_step(s, m)
    acc = a * acc + jnp.dot(v_aug(vst_ref[0, 0, i]), p, preferred_element_type=_F32)
    o_sel = acc[0:dh] / acc[dh:dh + 1]

    gate = _sigmoid(gt_ref[0, 0, 0])
    out_t = gate[0:1, :] * o_cmp + gate[1:2, :] * o_sel + gate[2:3, :] * o_win
    o_ref[0] = jnp.concatenate([out_t[:, g * qt:(g + 1) * qt] for g in range(grp)], axis=0).T


def _nsa_attention(qt_arr, k_cmp, v_cmp_t, cst, ks, vs_t, kw, vw_t, gates_t):
    bsz, hk, nq, dh, lanes = qt_arr.shape
    qt = lanes // NSA_GROUP
    kt = qt
    t = ks.shape[2]
    ns = t // NSA_SEL_LEN
    nc = k_cmp.shape[2]
    ntile = t // kt
    per_head = lambda shape: pl.BlockSpec((1, 1) + shape, lambda b, h, i: (b, h) + (0,) * len(shape))
    per_tile = lambda shape: pl.BlockSpec((1, 1, 1) + shape,
                                          lambda b, h, i: (b, h, i) + (0,) * len(shape))
    return pl.pallas_call(
        functools.partial(_nsa_kernel, qt=qt, kt=kt, n_sel=min(NSA_TOPN, ns)),
        grid=(bsz, hk, nq),
        in_specs=[per_tile((dh, lanes)),
                  per_head((nc, dh)), per_head((dh, nc)),
                  pl.BlockSpec((ns, nc), lambda b, h, i: (0, 0)),
                  per_head((t, dh)), per_head((ntile, dh, kt)),
                  per_head((t, dh)), per_head((ntile, dh, kt)),
                  per_tile((3, lanes))],
        out_specs=pl.BlockSpec((1, qt, NSA_GROUP * dh), lambda b, h, i: (b, i, h)),
        out_shape=jax.ShapeDtypeStruct((bsz, t, hk * NSA_GROUP * dh), _F32),
        scratch_shapes=[pltpu.VMEM((t, dh + ns), _BF16),
                        pltpu.VMEM((dh + 16, lanes), _F32),
                        pltpu.VMEM((2, kt, lanes), _F32),
                        pltpu.VMEM((2, kt, lanes), _BF16),
                        pltpu.VMEM((3, kt, lanes), _F32)],
        compiler_params=pltpu.CompilerParams(
            dimension_semantics=("parallel", "parallel", "arbitrary"),
            vmem_limit_bytes=VMEM_LIMIT),
        name="nsa_attn",
    )(qt_arr, k_cmp, v_cmp_t, cst, ks, vs_t, kw, vw_t, gates_t)


def _cmp_to_sel_t(t):
    ncmp = (t - NSA_CMP_LEN) // NSA_CMP_STRIDE + 1
    c_start = np.arange(ncmp) * NSA_CMP_STRIDE
    s_start = np.arange(t // NSA_SEL_LEN) * NSA_SEL_LEN
    overlap = np.clip(np.minimum(c_start[:, None] + NSA_CMP_LEN, s_start[None, :] + NSA_SEL_LEN)
                      - np.maximum(c_start[:, None], s_start[None, :]), 0, None) / NSA_CMP_LEN
    out = np.zeros((t // NSA_SEL_LEN, t // NSA_CMP_STRIDE), np.float32)
    out[:, :ncmp] = overlap.T
    return jnp.asarray(out, _BF16)


def _nsa(q_t, kc, vc, ks, kw, vs_t, vw_t, gates_t, pos_k, pos_v, k_w1, k_w2, v_w1, v_w2):
    k_cmp, v_cmp_t = _nsa_compress(kc, vc, pos_k, pos_v, k_w1, k_w2, v_w1, v_w2)
    return _nsa_attention(q_t, k_cmp, v_cmp_t, _cmp_to_sel_t(ks.shape[2]), ks, vs_t, kw, vw_t, gates_t)


def _out_ln_kernel(x_ref, a_ref, b_ref, wa_ref, wb_ref, g_ref, beta_ref, o_ref):
    mix = (jnp.dot(a_ref[...].astype(_BF16), wa_ref[...], preferred_element_type=_F32)
           + jnp.dot(b_ref[...].astype(_BF16), wb_ref[...], preferred_element_type=_F32))
    o_ref[...] = _layer_norm(DN_ALPHA * x_ref[...] + mix, g_ref[...], beta_ref[...])


def _out_ln(x, oa, ob, wa, wb, g, b):
    n, d = x.shape
    tm = min(PROJ_ROWS, n)
    row = lambda width: pl.BlockSpec((tm, width), lambda i: (i, 0))
    return pl.pallas_call(
        _out_ln_kernel,
        grid=(n // tm,),
        in_specs=[row(d), row(oa.shape[1]), row(ob.shape[1]), _const_spec(wa.shape),
                  _const_spec(wb.shape), _const_spec((1, d)), _const_spec((1, d))],
        out_specs=row(d),
        out_shape=jax.ShapeDtypeStruct((n, d), _F32),
        compiler_params=pltpu.CompilerParams(dimension_semantics=("parallel",),
                                             vmem_limit_bytes=VMEM_LIMIT),
        name="out_ln",
    )(x, oa, ob, wa, wb, g, b)


def kernel(x, ln1_g, ln1_b, ffn1_wg, ffn1_wu, ffn1_wd, w_in, gdn_conv_w, gdn_a_log, gdn_dt_bias,
           gdn_norm_w, nsa_cmp_pos_k, nsa_cmp_pos_v, nsa_cmp_k_w1, nsa_cmp_k_w2, nsa_cmp_v_w1,
           nsa_cmp_v_w2, w_out, ln2_g, ln2_b, ffn2_wg, ffn2_wu, ffn2_wd, ln3_g, ln3_b):
    bsz, t, d = x.shape
    n = bsz * t
    h = x.reshape(n, d)
    for i in range(DEPTH):
        h = _ffn_ln(h, ffn1_wg[i], ffn1_wu[i], ffn1_wd[i], ln1_g[i].reshape(1, d), ln1_b[i].reshape(1, d))
        w_nat, w_t = _proj_weights(w_in[i])
        qkv, z, small, kc, vc, ks, kw, q_t, vs_t, vw_t, gates_t = _in_proj(h, w_nat, w_t, bsz, t)
        o_gdn = _gdn(qkv.reshape(bsz, t, _QKV), z.reshape(bsz, t, GDN_WIDTH),
                     small.reshape(bsz, t, LANES),
                     gdn_conv_w[i], gdn_a_log[i], gdn_dt_bias[i], gdn_norm_w[i])
        o_nsa = _nsa(q_t, kc, vc, ks, kw, vs_t, vw_t, gates_t,
                     nsa_cmp_pos_k[i], nsa_cmp_pos_v[i], nsa_cmp_k_w1[i], nsa_cmp_k_w2[i],
                     nsa_cmp_v_w1[i], nsa_cmp_v_w2[i])
        wo = w_out[i].astype(_BF16)
        h = _out_ln(h, o_gdn.reshape(n, GDN_WIDTH), o_nsa.reshape(n, NSA_WIDTH),
                    wo[:GDN_WIDTH], wo[GDN_WIDTH:], ln2_g[i].reshape(1, d), ln2_b[i].reshape(1, d))
        h = _ffn_ln(h, ffn2_wg[i], ffn2_wu[i], ffn2_wd[i], ln3_g[i].reshape(1, d), ln3_b[i].reshape(1, d))
    return h.reshape(bsz, t, d)
```

```python
import functools

import numpy as np
import jax
import jax.numpy as jnp
from jax import lax
from jax.experimental import pallas as pl
from jax.experimental.pallas import tpu as pltpu

_F32 = jnp.float32
_BF16 = jnp.bfloat16

GDN_HEADS = 4
GDN_HEAD_DIM = 128
GDN_WIDTH = GDN_HEADS * GDN_HEAD_DIM
GDN_CONV = 4
GDN_CHUNK = 64

NSA_Q_HEADS = 8
NSA_KV_HEADS = 2
NSA_GROUP = NSA_Q_HEADS // NSA_KV_HEADS
NSA_HEAD_DIM = 64
NSA_WIDTH = NSA_Q_HEADS * NSA_HEAD_DIM
NSA_KV_WIDTH = NSA_KV_HEADS * NSA_HEAD_DIM
NSA_CMP_LEN = 32
NSA_CMP_STRIDE = 16
NSA_SEL_LEN = 64
NSA_TOPN = 16
NSA_WINDOW = 512
NSA_FORCE_SCORE = 1e4

LN_EPS = 1e-5
RMS_EPS = 1e-6
L2_EPS = 1e-6
DEPTH = 1
DN_ALPHA = (2 * DEPTH) ** 0.25

NEG_BIG = -(2.0 ** 100)
M_INIT = -3.0e38

LANES = 128
VMEM_LIMIT = 56 * 1024 * 1024

FFN_ROWS = 512
FFN_CHUNK = 256
PROJ_ROWS = 512
GDN_ROWS = 256
NSA_QT = 256
NSA_KT = 256


def _sigmoid(x):
    return 0.5 * jnp.tanh(0.5 * x) + 0.5


def _silu(x):
    h = 0.5 * x
    return h * jnp.tanh(h) + h


def _softplus(x):
    return jnp.maximum(x, 0.0) + jnp.log(1.0 + jnp.exp(-jnp.abs(x)))


def _mm(a, b):
    return jnp.dot(a.astype(_BF16), b.astype(_BF16), preferred_element_type=_F32)


def _split3(x):
    hi = x.astype(_BF16)
    r = x - hi.astype(_F32)
    mid = r.astype(_BF16)
    lo = (r - mid.astype(_F32)).astype(_BF16)
    return hi, mid, lo


def _layer_norm(y, g, b):
    mu = jnp.mean(y, axis=-1, keepdims=True)
    yc = y - mu
    var = jnp.mean(yc * yc, axis=-1, keepdims=True)
    return yc * lax.rsqrt(var + LN_EPS) * g + b


def _ffn_ln_kernel(x_ref, wg_ref, wu_ref, wd_ref, g_ref, b_ref, o_ref, h_ref):
    _ffn_ln_body(x_ref[...], wg_ref, wu_ref, wd_ref, g_ref, b_ref, o_ref, h_ref)


def _ffn_ln_body(x, wg_ref, wu_ref, wd_ref, g_ref, b_ref, o_ref, h_ref):
    xb = x.astype(_BF16)
    d_ff = wg_ref.shape[1]
    for c0 in range(0, d_ff, FFN_CHUNK):
        gate = jnp.dot(xb, wg_ref[:, c0:c0 + FFN_CHUNK], preferred_element_type=_F32)
        up = jnp.dot(xb, wu_ref[:, c0:c0 + FFN_CHUNK], preferred_element_type=_F32)
        h_ref[:, c0:c0 + FFN_CHUNK] = (_silu(gate) * up).astype(_BF16)
    y = DN_ALPHA * x + 0.5 * jnp.dot(h_ref[...], wd_ref[...], preferred_element_type=_F32)
    o_ref[...] = _layer_norm(y, g_ref[...], b_ref[...])


def _const_spec(shape):
    zeros = (0,) * len(shape)
    return pl.BlockSpec(shape, lambda *_: zeros, pipeline_mode=pl.Buffered(1))


def _ffn_ln(x, wg, wu, wd, g, b):
    n, d = x.shape
    d_ff = wg.shape[1]
    tm = min(FFN_ROWS, n)
    return pl.pallas_call(
        _ffn_ln_kernel,
        grid=(n // tm,),
        in_specs=[pl.BlockSpec((tm, d), lambda i: (i, 0)),
                  _const_spec((d, d_ff)), _const_spec((d, d_ff)), _const_spec((d_ff, d)),
                  _const_spec((1, d)), _const_spec((1, d))],
        out_specs=pl.BlockSpec((tm, d), lambda i: (i, 0)),
        out_shape=jax.ShapeDtypeStruct((n, d), _F32),
        scratch_shapes=[pltpu.VMEM((tm, d_ff), _BF16)],
        compiler_params=pltpu.CompilerParams(dimension_semantics=("parallel",),
                                             vmem_limit_bytes=VMEM_LIMIT),
        name="ffn_ln",
    )(x, wg.astype(_BF16), wu.astype(_BF16), wd.astype(_BF16), g, b)


_QKV = 3 * GDN_WIDTH
_NAT_KV = _QKV + GDN_WIDTH + LANES
_T_VS = NSA_WIDTH
_T_VW = _T_VS + NSA_KV_WIDTH
_T_GATE = _T_VW + NSA_KV_WIDTH
_T_ROWS = _T_GATE + 32
Q_SCALE = NSA_HEAD_DIM ** -0.5 * 1.4426950408889634


def _proj_kernel(x_ref, xh_ref, wn_ref, wt_ref, cw_ref, qkv_ref, z_ref, small_ref, kc_ref, vc_ref,
                 ks_ref, kw_ref, qt_ref, vst_ref, vwt_ref, gt_ref, *, qt, tiles_per_seq):
    hk_n, grp, dh = NSA_KV_HEADS, NSA_GROUP, NSA_HEAD_DIM
    xb = x_ref[...].astype(_BF16)
    tm = xb.shape[0]
    halo = xh_ref.shape[0]
    at_start = pl.program_id(0) % tiles_per_seq == 0
    xh = jnp.where(at_start, 0.0, xh_ref[...]).astype(_BF16)
    xcat = jnp.concatenate([xh, xb], axis=0)

    def conv_chunk(c0, width):
        cols = slice(c0, c0 + width)
        pre = jnp.dot(xcat, wn_ref[:, cols], preferred_element_type=_F32)
        conv = cw_ref[GDN_CONV - 1:GDN_CONV, cols] * pre[halo:, :]
        for d in range(1, GDN_CONV):
            conv = conv + cw_ref[GDN_CONV - 1 - d:GDN_CONV - d, cols] * pltpu.roll(pre, d, 0)[halo:, :]
        qkv_ref[:, cols] = _silu(conv)

    def nat_cols(ref, c0, width, r0=0):
        ref[:, r0:r0 + width] = jnp.dot(xb, wn_ref[:, c0:c0 + width], preferred_element_type=_F32)

    def kv_heads():
        kv = jnp.dot(xb, wn_ref[:, _NAT_KV:_NAT_KV + 4 * NSA_KV_WIDTH], preferred_element_type=_F32)
        for hk in range(hk_n):
            for idx, ref in enumerate((kc_ref, vc_ref, ks_ref, kw_ref)):
                c0 = idx * NSA_KV_WIDTH + hk * dh
                ref[0, hk] = kv[:, c0:c0 + dh].astype(ref.dtype)

    def q_heads(hk):
        r0 = hk * grp * dh
        yt = lax.dot_general(wt_ref[r0:r0 + grp * dh, :], xb, (((1,), (1,)), ((), ())),
                             preferred_element_type=_F32)
        for ii in range(tm // qt):
            for g in range(grp):
                qt_ref[0, hk, ii, :, g * qt:(g + 1) * qt] = (
                    yt[g * dh:(g + 1) * dh, ii * qt:(ii + 1) * qt] * Q_SCALE).astype(_BF16)

    def v_gates():
        yt = lax.dot_general(wt_ref[_T_VS:_T_ROWS, :], xb, (((1,), (1,)), ((), ())),
                             preferred_element_type=_F32)
        for hk in range(hk_n):
            for ii in range(tm // qt):
                cols = slice(ii * qt, (ii + 1) * qt)
                vst_ref[0, hk, ii] = yt[hk * dh:(hk + 1) * dh, cols].astype(_BF16)
                vwt_ref[0, hk, ii] = yt[NSA_KV_WIDTH + hk * dh:NSA_KV_WIDTH + (hk + 1) * dh,
                                        cols].astype(_BF16)
                for g in range(grp):
                    for c in range(3):
                        r = 2 * NSA_KV_WIDTH + (hk * grp + g) * 3 + c
                        gt_ref[0, hk, ii, c:c + 1, g * qt:(g + 1) * qt] = yt[r:r + 1, cols]

    step = 2 * LANES
    others = [lambda: nat_cols(z_ref, _QKV, step), lambda: nat_cols(z_ref, _QKV + step, step, step),
              lambda: (nat_cols(small_ref, _QKV + GDN_WIDTH, LANES), kv_heads()),
              lambda: q_heads(0), lambda: q_heads(1), v_gates]
    assert _QKV // step == len(others) and GDN_WIDTH == 2 * step and hk_n == 2
    for k, other in enumerate(others):
        conv_chunk(k * step, step)
        other()


def _proj_weights(w):
    d = w.shape[0]
    sizes = (GDN_WIDTH,) * 4 + (GDN_HEADS,) * 2 + (NSA_WIDTH,) + (NSA_KV_WIDTH,) * 6 + (3 * NSA_Q_HEADS,)
    off = np.concatenate([[0], np.cumsum(sizes)])
    gq, gk, gv, gz, gb, ga, nq, kc, vc, ks, vs, kw, vw, gate = [
        w[:, int(off[i]):int(off[i + 1])] for i in range(len(sizes))]
    small = jnp.concatenate([gb, ga, jnp.zeros((d, LANES - 2 * GDN_HEADS), w.dtype)], axis=1)
    w_nat = jnp.concatenate([gq, gk, gv, gz, small, kc, vc, ks, kw], axis=1)
    w_t = jnp.concatenate([nq, vs, vw, gate, jnp.zeros((d, _T_ROWS - _T_GATE - gate.shape[1]), w.dtype)],
                          axis=1).T
    return w_nat.astype(_BF16), w_t.astype(_BF16)


def _in_proj(x, w_nat, w_t, conv_w, bsz, t):
    n, d = x.shape
    hk, grp, dh = NSA_KV_HEADS, NSA_GROUP, NSA_HEAD_DIM
    tm = min(PROJ_ROWS, t)
    qt = min(NSA_QT, t)
    nt = t // tm
    halo = 16
    halo_spec = pl.BlockSpec((halo, d), lambda r: (jnp.maximum(r * (tm // halo) - 1, 0), 0))
    row = lambda width: pl.BlockSpec((tm, width), lambda r: (r, 0))
    head = pl.BlockSpec((1, hk, tm, dh), lambda r: (r // nt, 0, r % nt, 0))
    tile = lambda rows, lanes: pl.BlockSpec((1, hk, tm // qt, rows, lanes),
                                            lambda r: (r // nt, 0, r % nt, 0, 0))
    sds = jax.ShapeDtypeStruct
    return pl.pallas_call(
        functools.partial(_proj_kernel, qt=qt, tiles_per_seq=nt),
        grid=(n // tm,),
        in_specs=[pl.BlockSpec((tm, d), lambda r: (r, 0)), halo_spec, _const_spec(w_nat.shape),
                  _const_spec(w_t.shape), _const_spec(conv_w.shape)],
        out_specs=[row(_QKV), row(GDN_WIDTH), row(LANES), head, head, head, head,
                   tile(dh, grp * qt), tile(dh, qt), tile(dh, qt), tile(3, grp * qt)],
        out_shape=[sds((n, _QKV), _F32), sds((n, GDN_WIDTH), _F32), sds((n, LANES), _F32),
                   sds((bsz, hk, t, dh), _F32), sds((bsz, hk, t, dh), _F32),
                   sds((bsz, hk, t, dh), _BF16), sds((bsz, hk, t, dh), _BF16),
                   sds((bsz, hk, t // qt, dh, grp * qt), _BF16),
                   sds((bsz, hk, t // qt, dh, qt), _BF16), sds((bsz, hk, t // qt, dh, qt), _BF16),
                   sds((bsz, hk, t // qt, 3, grp * qt), _F32)],
        compiler_params=pltpu.CompilerParams(dimension_semantics=("parallel",),
                                             vmem_limit_bytes=VMEM_LIMIT),
        name="in_proj",
    )(x, x, w_nat, w_t, conv_w)


def _bdot(a, b):
    return jnp.dot(a, b, preferred_element_type=_F32)


def _unit_lower_inverses(lmats):
    c = lmats[0].shape[0]
    eye = jnp.where(lax.broadcasted_iota(jnp.int32, (c, c), 0)
                    == lax.broadcasted_iota(jnp.int32, (c, c), 1), 1.0, 0.0)
    rs = [eye - l for l in lmats]
    pb = [l.astype(_BF16) for l in lmats]
    ps = [_bdot(l, l) for l in pb]
    span = 2
    while True:
        pb = [p.astype(_BF16) for p in ps]
        rs = [r + _bdot(r.astype(_BF16), p) for r, p in zip(rs, pb)]
        span *= 2
        if span >= c:
            return rs
        ps = [_bdot(p, p) for p in pb]


def _gdn_kernel(act_ref, z_ref, bac_ref, bar_ref, hpr_ref, hpc_ref, nw_ref, o_ref, s_ref, *, tb):
    nh, hd, c = GDN_HEADS, GDN_HEAD_DIM, GDN_CHUNK

    @pl.when(pl.program_id(1) == 0)
    def _():
        s_ref[...] = jnp.zeros_like(s_ref)

    row = lax.broadcasted_iota(jnp.int32, (c, c), 0)
    col = lax.broadcasted_iota(jnp.int32, (c, c), 1)
    causal = row >= col
    strict = row > col
    tril = jnp.where(causal, 1.0, 0.0).astype(_BF16)
    triu = jnp.where(row <= col, 1.0, 0.0).astype(_BF16)

    nchunk = tb // c
    qg_l, qb_l, kb_l, kbeta_l, rhs_l, kd_l, decay_l, eg_l = [], [], [], [], [], [], [], []
    for ci in range(nchunk):
        r0 = ci * c
        bac = bac_ref[0, r0:r0 + c, :]
        beta_all = _sigmoid(bac)
        g_all = -jnp.exp(hpr_ref[0:1, :]) * _softplus(bac + hpr_ref[1:2, :])
        gc_all = sum(_bdot(tril, part) for part in _split3(g_all))
        bar = bar_ref[0, ci]
        gr_all = -jnp.exp(hpc_ref[:, 0:1]) * _softplus(bar + hpc_ref[:, 1:2])
        gcr_all = sum(_bdot(part, triu) for part in _split3(gr_all))
        for h in range(nh):
            q = act_ref[0, r0:r0 + c, h * hd:(h + 1) * hd]
            k = act_ref[0, r0:r0 + c, (nh + h) * hd:(nh + h + 1) * hd]
            v = act_ref[0, r0:r0 + c, (2 * nh + h) * hd:(2 * nh + h + 1) * hd]
            q = q * lax.rsqrt(jnp.sum(q * q, axis=-1, keepdims=True) + L2_EPS) * (hd ** -0.5)
            k = k * lax.rsqrt(jnp.sum(k * k, axis=-1, keepdims=True) + L2_EPS)
            beta = beta_all[:, h:h + 1]
            gcc = gc_all[:, nh + h:nh + h + 1]
            gcr = gcr_all[nh + h:nh + h + 1, :]
            glast = gc_all[c - 1:c, nh + h:nh + h + 1]
            decay_l.append(jnp.exp(jnp.where(causal, gcc - gcr, NEG_BIG)))
            kbeta = k * beta
            egc = jnp.exp(gcc)
            qg_l.append(q * egc)
            qb_l.append(q.astype(_BF16))
            kb_l.append(k.astype(_BF16))
            kbeta_l.append(kbeta.astype(_BF16))
            rhs_l.append(jnp.concatenate([kbeta * egc, v * beta], axis=1).astype(_BF16))
            kd_l.append((k * jnp.exp(glast - gcc)).astype(_BF16))
            eg_l.append(jnp.exp(glast))

    nt = (((1,), (1,)), ((), ()))
    tn = (((0,), (0,)), ((), ()))
    kk_l = [lax.dot_general(a, b, nt, preferred_element_type=_F32) for a, b in zip(kbeta_l, kb_l)]
    qk_l = [lax.dot_general(a, b, nt, preferred_element_type=_F32) for a, b in zip(qb_l, kb_l)]
    lmat_l = [jnp.where(strict, kk * d, 0.0) for kk, d in zip(kk_l, decay_l)]
    tinv_l = _unit_lower_inverses(lmat_l)
    wu_l = [_bdot(t.astype(_BF16), r).astype(_BF16) for t, r in zip(tinv_l, rhs_l)]
    qkb_l = [(qk * d).astype(_BF16) for qk, d in zip(qk_l, decay_l)]
    pn_l = [lax.dot_general(kd, wu, tn, preferred_element_type=_F32) for kd, wu in zip(kd_l, wu_l)]
    ab_l = [_bdot(qk, wu) for qk, wu in zip(qkb_l, wu_l)]
    a_l = [(qg - ab[:, :hd]).astype(_BF16) for qg, ab in zip(qg_l, ab_l)]
    p_l = [pn[:, :hd].astype(_BF16) for pn in pn_l]

    states = [s_ref[h] for h in range(nh)]
    for ci in range(nchunk):
        r0 = ci * c
        sb = [s.astype(_BF16) for s in states]
        outs = [_bdot(a_l[ci * nh + h], sb[h]) + ab_l[ci * nh + h][:, hd:] for h in range(nh)]
        states = [states[h] * eg_l[ci * nh + h] + pn_l[ci * nh + h][:, hd:]
                  - _bdot(p_l[ci * nh + h], sb[h]) for h in range(nh)]
        for h in range(nh):
            o = outs[h]
            zz = z_ref[0, r0:r0 + c, h * hd:(h + 1) * hd]
            o = (o * lax.rsqrt(jnp.mean(o * o, axis=-1, keepdims=True) + RMS_EPS) * nw_ref[...]
                 * _silu(zz))
            o_ref[0, r0:r0 + c, h * hd:(h + 1) * hd] = o
    for h in range(nh):
        s_ref[h] = states[h]


def _gdn(qkv, z, small, a_log, dt_bias, norm_w):
    bsz, t, width = qkv.shape
    nh, c = GDN_HEADS, GDN_CHUNK
    tb = min(GDN_ROWS, t)
    ba_rows = small[:, :, :8].reshape(bsz, t // c, c, 8).transpose(0, 1, 3, 2)
    hp_row = jnp.zeros((8, LANES), _F32)
    hp_row = hp_row.at[0, nh:2 * nh].set(a_log).at[1, nh:2 * nh].set(dt_bias)
    hp_col = jnp.zeros((8, LANES), _F32)
    hp_col = hp_col.at[nh:2 * nh, 0].set(a_log).at[nh:2 * nh, 1].set(dt_bias)
    return pl.pallas_call(
        functools.partial(_gdn_kernel, tb=tb),
        grid=(bsz, t // tb),
        in_specs=[pl.BlockSpec((1, tb, width), lambda b, i: (b, i, 0)),
                  pl.BlockSpec((1, tb, GDN_WIDTH), lambda b, i: (b, i, 0)),
                  pl.BlockSpec((1, tb, LANES), lambda b, i: (b, i, 0)),
                  pl.BlockSpec((1, tb // c, 8, c), lambda b, i: (b, i, 0, 0)),
                  pl.BlockSpec((8, LANES), lambda b, i: (0, 0)),
                  pl.BlockSpec((8, LANES), lambda b, i: (0, 0)),
                  pl.BlockSpec((1, GDN_HEAD_DIM), lambda b, i: (0, 0))],
        out_specs=pl.BlockSpec((1, tb, GDN_WIDTH), lambda b, i: (b, i, 0)),
        out_shape=jax.ShapeDtypeStruct((bsz, t, GDN_WIDTH), _F32),
        scratch_shapes=[pltpu.VMEM((nh, GDN_HEAD_DIM, GDN_HEAD_DIM), _F32)],
        compiler_params=pltpu.CompilerParams(dimension_semantics=("parallel", "arbitrary"),
                                             vmem_limit_bytes=VMEM_LIMIT),
        name="gdn",
    )(qkv, z, small, ba_rows, hp_row, hp_col, norm_w.reshape(1, -1))


def _gelu_tanh(x):
    return 0.5 * x * (1.0 + jnp.tanh(0.7978845608028654 * (x + 0.044715 * (x * x * x))))


def _cmp_kernel(xk_ref, xv_ref, pk_ref, pv_ref, wk1_ref, wk2_ref, wv1_ref, wv2t_ref, ok_ref, ovt_ref):
    def hidden(x, pos_ref, w1_ref):
        n = x.shape[0]
        ya = _mm(x + pos_ref[0:1, :], w1_ref[0])
        yb = _mm(x + pos_ref[1:2, :], w1_ref[1])
        hid = ya + pltpu.roll(yb, n - 1, 0)
        return _gelu_tanh(hid).astype(_BF16)

    ok_ref[0, 0] = _bdot(hidden(xk_ref[0, 0], pk_ref, wk1_ref), wk2_ref[...]).astype(_BF16)
    ovt_ref[0, 0] = lax.dot_general(wv2t_ref[...], hidden(xv_ref[0, 0], pv_ref, wv1_ref),
                                    (((1,), (1,)), ((), ())),
                                    preferred_element_type=_F32).astype(_BF16)


def _nsa_compress(kc, vc, pos_k, pos_v, k_w1, k_w2, v_w1, v_w2):
    bsz, hk, t, dh = kc.shape
    half = NSA_CMP_STRIDE * dh
    nrow = t // NSA_CMP_STRIDE
    hidden = k_w1.shape[1]
    xk = kc.reshape(bsz, hk, nrow, half)
    xv = vc.reshape(bsz, hk, nrow, half)
    x_spec = pl.BlockSpec((1, 1, nrow, half), lambda b, h: (b, h, 0, 0))
    full = lambda shape: pl.BlockSpec(shape, lambda b, h: (0,) * len(shape))
    return pl.pallas_call(
        _cmp_kernel,
        grid=(bsz, hk),
        in_specs=[x_spec, x_spec, full((2, half)), full((2, half)),
                  full((2, half, hidden)), full((hidden, dh)),
                  full((2, half, hidden)), full((dh, hidden))],
        out_specs=[pl.BlockSpec((1, 1, nrow, dh), lambda b, h: (b, h, 0, 0)),
                   pl.BlockSpec((1, 1, dh, nrow), lambda b, h: (b, h, 0, 0))],
        out_shape=[jax.ShapeDtypeStruct((bsz, hk, nrow, dh), _BF16),
                   jax.ShapeDtypeStruct((bsz, hk, dh, nrow), _BF16)],
        compiler_params=pltpu.CompilerParams(dimension_semantics=("parallel", "parallel"),
                                             vmem_limit_bytes=VMEM_LIMIT),
        name="nsa_cmp",
    )(xk, xv, pos_k.reshape(2, half), pos_v.reshape(2, half),
      k_w1.reshape(2, half, hidden).astype(_BF16), k_w2.astype(_BF16),
      v_w1.reshape(2, half, hidden).astype(_BF16), v_w2.T.astype(_BF16))


def _nsa_kernel(q_ref, kc_ref, vct_ref, cst_ref, ks_ref, vst_ref, kw_ref, vwt_ref, gt_ref, o_ref,
                kaug_ref, acc_ref, s_buf, p_buf, bias_ref, *, qt, kt, n_sel):
    assert qt == kt
    grp, dh = NSA_GROUP, NSA_HEAD_DIM
    lanes = grp * qt
    t_all = ks_ref.shape[2]
    ns = t_all // NSA_SEL_LEN
    nc = kc_ref.shape[2]
    i = pl.program_id(2)
    s0 = i * qt

    @pl.when(i == 0)
    def _():
        rblk = lax.broadcasted_iota(jnp.int32, (t_all, ns), 0) // NSA_SEL_LEN
        cblk = lax.broadcasted_iota(jnp.int32, (t_all, ns), 1)
        kaug_ref[:, 0:dh] = ks_ref[0, 0]
        kaug_ref[:, dh:dh + ns] = jnp.where(rblk == cblk, 1.0, 0.0).astype(_BF16)
        krow = lax.broadcasted_iota(jnp.int32, (kt, lanes), 0)
        tl = lax.broadcasted_iota(jnp.int32, (kt, lanes), 1) & (qt - 1)
        far_off = ((NSA_WINDOW + kt - 1) // kt) * kt - NSA_WINDOW
        bias_ref[0] = jnp.where(krow <= tl, 0.0, NEG_BIG)
        bias_ref[1] = jnp.where(krow > tl + far_off, 0.0, NEG_BIG)
        bias_ref[2] = jnp.full((kt, lanes), NEG_BIG, _F32)

    lane = lax.broadcasted_iota(jnp.int32, (1, lanes), 1)
    tq = s0 + (lane & (qt - 1))
    qs = q_ref[0, 0, 0]

    sc = jnp.dot(kc_ref[0, 0], qs, preferred_element_type=_F32)
    cend = lax.broadcasted_iota(jnp.int32, (nc, 1), 0) * NSA_CMP_STRIDE + (NSA_CMP_LEN - 1)
    valid = cend <= tq
    scm = jnp.where(valid, sc, NEG_BIG)
    mc = jnp.max(scm, axis=0, keepdims=True)
    ec = jnp.where(valid, jnp.exp2(scm - mc), 0.0)
    lc = jnp.sum(ec, axis=0, keepdims=True)
    pc = ec / jnp.where(lc > 0.0, lc, 1.0)
    o_cmp = jnp.dot(vct_ref[0, 0], pc.astype(_BF16), preferred_element_type=_F32)

    psum = pc[:, 0:qt]
    for g in range(1, grp):
        psum = psum + pc[:, g * qt:(g + 1) * qt]
    p_hi = psum.astype(_BF16)
    p_lo = (psum - p_hi.astype(_F32)).astype(_BF16)
    cst = cst_ref[...]
    imp = (jnp.dot(cst, p_hi, preferred_element_type=_F32)
           + jnp.dot(cst, p_lo, preferred_element_type=_F32))
    blk = lax.broadcasted_iota(jnp.int32, (ns, qt), 0)
    tq1 = s0 + lax.broadcasted_iota(jnp.int32, (ns, qt), 1)
    cur = tq1 // NSA_SEL_LEN
    visible = blk * NSA_SEL_LEN <= tq1
    forced = (blk == 0) | (blk == cur) | (blk == cur - 1)
    imp = jnp.where(visible, jnp.where(forced, NSA_FORCE_SCORE, imp), -1.0)
    sub = lax.broadcasted_iota(jnp.int32, (8, qt), 0)
    imp_rows = [imp[8 * r:8 * r + 8, :] for r in range(ns // 8)]
    ranks = [jnp.zeros((8, qt), _F32) for _ in imp_rows]
    for j in range(ns):
        rowj = imp[j:j + 1, :]
        for r, rows in enumerate(imp_rows):
            if 8 * r > j:
                beats = jnp.where(rowj >= rows, 1.0, 0.0)
            elif 8 * r + 7 <= j:
                beats = jnp.where(rowj > rows, 1.0, 0.0)
            else:
                beats = jnp.where(sub + 8 * r > j, jnp.where(rowj >= rows, 1.0, 0.0),
                                  jnp.where(rowj > rows, 1.0, 0.0))
            ranks[r] = ranks[r] + beats
    rank = jnp.concatenate(ranks, axis=0)
    bias = jnp.where(rank < n_sel, 0.0, NEG_BIG).astype(_BF16)
    qaug = jnp.concatenate([qs, jnp.concatenate([bias] * grp, axis=1)], axis=0)

    m0 = jnp.full((1, lanes), M_INIT, _F32)
    ones_rows = jnp.where(lax.broadcasted_iota(jnp.int32, (acc_ref.shape[0] - dh, kt), 0) == 0,
                          1.0, 0.0).astype(_BF16)

    def v_aug(v_tile):
        return jnp.concatenate([v_tile, ones_rows], axis=0)

    def window_branch():
        nwt = (NSA_WINDOW + kt - 1) // kt + 1
        s_win, j_win = [], []
        for d in range(nwt):
            back = nwt - 1 - d
            j = i - back
            jc = jnp.maximum(j, 0)
            k_tile = kw_ref[0, 0, pl.ds(pl.multiple_of(jc * kt, kt), kt), :]
            s = jnp.dot(k_tile, qs, preferred_element_type=_F32)
            if back == 0:
                s = s + bias_ref[0]
            elif back == nwt - 1:
                s = s + bias_ref[jnp.where(j >= 0, 1, 2)]
            else:
                s = s + jnp.where(j >= 0, 0.0, NEG_BIG)
            s_win.append(s)
            j_win.append(jc)
        mw = s_win[0].max(axis=0, keepdims=True)
        for s in s_win[1:]:
            mw = jnp.maximum(mw, s.max(axis=0, keepdims=True))
        acc_w = jnp.zeros(acc_ref.shape, _F32)
        for s, jc in zip(s_win, j_win):
            acc_w = acc_w + jnp.dot(v_aug(vwt_ref[0, 0, jc]), jnp.exp2((s - mw).astype(_BF16)),
                                    preferred_element_type=_F32)
        return acc_w[0:dh] / acc_w[dh:dh + 1]

    o_win = window_branch()

    n_s, n_p = s_buf.shape[0], p_buf.shape[0]
    last_tile = t_all // kt - 1

    def qk_tile(j):
        start = pl.multiple_of(jnp.minimum(j, last_tile) * kt, kt)
        return jnp.dot(kaug_ref[pl.ds(start, kt), :], qaug, preferred_element_type=_F32)

    def softmax_step(s, m):
        m_new = jnp.maximum(m, jnp.max(s, axis=0, keepdims=True))
        return m_new, jnp.exp2(m - m_new), jnp.exp2((s - m_new).astype(_BF16))

    acc_ref[...] = jnp.zeros_like(acc_ref)
    p_buf[n_p - 1] = jnp.zeros((kt, lanes), _BF16)
    s_buf[0] = qk_tile(0)
    s_buf[1] = qk_tile(1)

    def sel_step(j, c, carry):
        m, a_prev = carry
        s_buf[(c + 2) % n_s] = qk_tile(j + 2)
        pv = jnp.dot(v_aug(vst_ref[0, 0, jnp.maximum(j - 1, 0)]), p_buf[(c - 1) % n_p],
                     preferred_element_type=_F32)
        m, a, p = softmax_step(s_buf[c], m)
        p_buf[c % n_p] = p
        acc_ref[...] = a_prev * acc_ref[...] + pv
        return m, a

    def sel_round(jj, carry):
        for c in range(n_s):
            carry = sel_step(n_s * jj + c, c, carry)
        return carry

    carry = lax.fori_loop(0, i // n_s, sel_round, (m0, jnp.ones((1, lanes), _F32)))
    base = (i // n_s) * n_s
    rem = i - base
    carry = lax.cond(rem >= 2, lambda x: sel_step(base + 1, 1, sel_step(base, 0, x)), lambda x: x,
                     carry)
    carry = lax.cond(rem == 1, functools.partial(sel_step, base, 0), lambda x: x, carry)
    carry = lax.cond(rem == 3, functools.partial(sel_step, base + 2, 2), lambda x: x, carry)
    m, a_prev = carry
    pv = jnp.dot(v_aug(vst_ref[0, 0, jnp.maximum(i - 1, 0)]), p_buf[(i + n_p - 1) % n_p],
                 preferred_element_type=_F32)
    acc = a_prev * acc_ref[...] + pv
    s = s_buf[i % n_s] + bias_ref[0]
    m, a, p = softmax_step(s, m)
    acc = a * acc + jnp.dot(v_aug(vst_ref[0, 0, i]), p, preferred_element_type=_F32)
    o_sel = acc[0:dh] / acc[dh:dh + 1]

    gate = _sigmoid(gt_ref[0, 0, 0])
    out_t = gate[0:1, :] * o_cmp + gate[1:2, :] * o_sel + gate[2:3, :] * o_win
    o_ref[0] = jnp.concatenate([out_t[:, g * qt:(g + 1) * qt] for g in range(grp)], axis=0).T


def _nsa_attention(qt_arr, k_cmp, v_cmp_t, cst, ks, vs_t, kw, vw_t, gates_t):
    bsz, hk, nq, dh, lanes = qt_arr.shape
    qt = lanes // NSA_GROUP
    kt = qt
    t = ks.shape[2]
    ns = t // NSA_SEL_LEN
    nc = k_cmp.shape[2]
    ntile = t // kt
    per_head = lambda shape: pl.BlockSpec((1, 1) + shape, lambda b, h, i: (b, h) + (0,) * len(shape))
    per_tile = lambda shape: pl.BlockSpec((1, 1, 1) + shape,
                                          lambda b, h, i: (b, h, i) + (0,) * len(shape))
    return pl.pallas_call(
        functools.partial(_nsa_kernel, qt=qt, kt=kt, n_sel=min(NSA_TOPN, ns)),
        grid=(bsz, hk, nq),
        in_specs=[per_tile((dh, lanes)),
                  per_head((nc, dh)), per_head((dh, nc)),
                  pl.BlockSpec((ns, nc), lambda b, h, i: (0, 0)),
                  per_head((t, dh)), per_head((ntile, dh, kt)),
                  per_head((t, dh)), per_head((ntile, dh, kt)),
                  per_tile((3, lanes))],
        out_specs=pl.BlockSpec((1, qt, NSA_GROUP * dh), lambda b, h, i: (b, i, h)),
        out_shape=jax.ShapeDtypeStruct((bsz, t, hk * NSA_GROUP * dh), _F32),
        scratch_shapes=[pltpu.VMEM((t, dh + ns), _BF16),
                        pltpu.VMEM((dh + 16, lanes), _F32),
                        pltpu.VMEM((4, kt, lanes), _F32),
                        pltpu.VMEM((2, kt, lanes), _BF16),
                        pltpu.VMEM((3, kt, lanes), _F32)],
        compiler_params=pltpu.CompilerParams(
            dimension_semantics=("parallel", "parallel", "arbitrary"),
            vmem_limit_bytes=VMEM_LIMIT),
        name="nsa_attn",
    )(qt_arr, k_cmp, v_cmp_t, cst, ks, vs_t, kw, vw_t, gates_t)


def _cmp_to_sel_t(t):
    ncmp = (t - NSA_CMP_LEN) // NSA_CMP_STRIDE + 1
    c_start = np.arange(ncmp) * NSA_CMP_STRIDE
    s_start = np.arange(t // NSA_SEL_LEN) * NSA_SEL_LEN
    overlap = np.clip(np.minimum(c_start[:, None] + NSA_CMP_LEN, s_start[None, :] + NSA_SEL_LEN)
                      - np.maximum(c_start[:, None], s_start[None, :]), 0, None) / NSA_CMP_LEN
    out = np.zeros((t // NSA_SEL_LEN, t // NSA_CMP_STRIDE), np.float32)
    out[:, :ncmp] = overlap.T
    return jnp.asarray(out, _BF16)


def _nsa(q_t, kc, vc, ks, kw, vs_t, vw_t, gates_t, pos_k, pos_v, k_w1, k_w2, v_w1, v_w2):
    k_cmp, v_cmp_t = _nsa_compress(kc, vc, pos_k, pos_v, k_w1, k_w2, v_w1, v_w2)
    return _nsa_attention(q_t, k_cmp, v_cmp_t, _cmp_to_sel_t(ks.shape[2]), ks, vs_t, kw, vw_t, gates_t)


def _out_ffn_kernel(x_ref, a_ref, b_ref, wa_ref, wb_ref, g2_ref, b2_ref, wg_ref, wu_ref, wd_ref,
                    g3_ref, b3_ref, o_ref, h_ref):
    mix = (jnp.dot(a_ref[...].astype(_BF16), wa_ref[...], preferred_element_type=_F32)
           + jnp.dot(b_ref[...].astype(_BF16), wb_ref[...], preferred_element_type=_F32))
    x2 = _layer_norm(DN_ALPHA * x_ref[...] + mix, g2_ref[...], b2_ref[...])
    _ffn_ln_body(x2, wg_ref, wu_ref, wd_ref, g3_ref, b3_ref, o_ref, h_ref)


def _out_ffn(x, oa, ob, wa, wb, g2, b2, wg, wu, wd, g3, b3):
    n, d = x.shape
    d_ff = wg.shape[1]
    tm = min(FFN_ROWS, n)
    row = lambda width: pl.BlockSpec((tm, width), lambda i: (i, 0))
    return pl.pallas_call(
        _out_ffn_kernel,
        grid=(n // tm,),
        in_specs=[row(d), row(oa.shape[1]), row(ob.shape[1]), _const_spec(wa.shape),
                  _const_spec(wb.shape), _const_spec((1, d)), _const_spec((1, d)),
                  _const_spec((d, d_ff)), _const_spec((d, d_ff)), _const_spec((d_ff, d)),
                  _const_spec((1, d)), _const_spec((1, d))],
        out_specs=row(d),
        out_shape=jax.ShapeDtypeStruct((n, d), _F32),
        scratch_shapes=[pltpu.VMEM((tm, d_ff), _BF16)],
        compiler_params=pltpu.CompilerParams(dimension_semantics=("parallel",),
                                             vmem_limit_bytes=VMEM_LIMIT),
        name="out_ffn",
    )(x, oa, ob, wa, wb, g2, b2, wg.astype(_BF16), wu.astype(_BF16), wd.astype(_BF16), g3, b3)


def kernel(x, ln1_g, ln1_b, ffn1_wg, ffn1_wu, ffn1_wd, w_in, gdn_conv_w, gdn_a_log, gdn_dt_bias,
           gdn_norm_w, nsa_cmp_pos_k, nsa_cmp_pos_v, nsa_cmp_k_w1, nsa_cmp_k_w2, nsa_cmp_v_w1,
           nsa_cmp_v_w2, w_out, ln2_g, ln2_b, ffn2_wg, ffn2_wu, ffn2_wd, ln3_g, ln3_b):
    bsz, t, d = x.shape
    n = bsz * t
    h = x.reshape(n, d)
    for i in range(DEPTH):
        h = _ffn_ln(h, ffn1_wg[i], ffn1_wu[i], ffn1_wd[i], ln1_g[i].reshape(1, d), ln1_b[i].reshape(1, d))
        w_nat, w_t = _proj_weights(w_in[i])
        qkv, z, small, kc, vc, ks, kw, q_t, vs_t, vw_t, gates_t = _in_proj(
            h, w_nat, w_t, gdn_conv_w[i], bsz, t)
        o_gdn = _gdn(qkv.reshape(bsz, t, _QKV), z.reshape(bsz, t, GDN_WIDTH),
                     small.reshape(bsz, t, LANES), gdn_a_log[i], gdn_dt_bias[i], gdn_norm_w[i])
        o_nsa = _nsa(q_t, kc, vc, ks, kw, vs_t, vw_t, gates_t,
                     nsa_cmp_pos_k[i], nsa_cmp_pos_v[i], nsa_cmp_k_w1[i], nsa_cmp_k_w2[i],
                     nsa_cmp_v_w1[i], nsa_cmp_v_w2[i])
        wo = w_out[i].astype(_BF16)
        h = _out_ffn(h, o_gdn.reshape(n, GDN_WIDTH), o_nsa.reshape(n, NSA_WIDTH),
                     wo[:GDN_WIDTH], wo[GDN_WIDTH:], ln2_g[i].reshape(1, d), ln2_b[i].reshape(1, d),
                     ffn2_wg[i], ffn2_wu[i], ffn2_wd[i], ln3_g[i].reshape(1, d), ln3_b[i].reshape(1, d))
    return h.reshape(bsz, t, d)
```

```python
import functools

import numpy as np
import jax
import jax.numpy as jnp
from jax import lax
from jax.experimental import pallas as pl
from jax.experimental.pallas import tpu as pltpu

_F32 = jnp.float32
_BF16 = jnp.bfloat16

GDN_HEADS = 4
GDN_HEAD_DIM = 128
GDN_WIDTH = GDN_HEADS * GDN_HEAD_DIM
GDN_CONV = 4
GDN_CHUNK = 64

NSA_Q_HEADS = 8
NSA_KV_HEADS = 2
NSA_GROUP = NSA_Q_HEADS // NSA_KV_HEADS
NSA_HEAD_DIM = 64
NSA_WIDTH = NSA_Q_HEADS * NSA_HEAD_DIM
NSA_KV_WIDTH = NSA_KV_HEADS * NSA_HEAD_DIM
NSA_CMP_LEN = 32
NSA_CMP_STRIDE = 16
NSA_SEL_LEN = 64
NSA_TOPN = 16
NSA_WINDOW = 512
NSA_FORCE_SCORE = 1e4

LN_EPS = 1e-5
RMS_EPS = 1e-6
L2_EPS = 1e-6
DEPTH = 1
DN_ALPHA = (2 * DEPTH) ** 0.25

NEG_BIG = -(2.0 ** 100)
M_INIT = -3.0e38

LANES = 128
VMEM_LIMIT = 56 * 1024 * 1024

FFN_ROWS = 512
FFN_CHUNK = 256
PROJ_ROWS = 512
GDN_ROWS = 256
NSA_QT = 256
NSA_KT = 256


def _sigmoid(x):
    return 0.5 * jnp.tanh(0.5 * x) + 0.5


def _silu(x):
    h = 0.5 * x
    return h * jnp.tanh(h) + h


def _softplus(x):
    return jnp.maximum(x, 0.0) + jnp.log(1.0 + jnp.exp(-jnp.abs(x)))


def _mm(a, b):
    return jnp.dot(a.astype(_BF16), b.astype(_BF16), preferred_element_type=_F32)


def _split3(x):
    hi = x.astype(_BF16)
    r = x - hi.astype(_F32)
    mid = r.astype(_BF16)
    lo = (r - mid.astype(_F32)).astype(_BF16)
    return hi, mid, lo


def _layer_norm(y, g, b):
    mu = jnp.mean(y, axis=-1, keepdims=True)
    yc = y - mu
    var = jnp.mean(yc * yc, axis=-1, keepdims=True)
    return yc * lax.rsqrt(var + LN_EPS) * g + b


def _ffn_ln_kernel(x_ref, wg_ref, wu_ref, wd_ref, g_ref, b_ref, o_ref, h_ref, y_ref):
    @pl.when(pl.program_id(0) == 0)
    def _():
        y_ref[...] = jnp.zeros_like(y_ref)

    ln = _layer_norm_pieces(y_ref, g_ref, b_ref, o_ref, 8)
    x = x_ref[...]
    y_ref[...] = DN_ALPHA * x + 0.5 * _swiglu(x.astype(_BF16), wg_ref, wu_ref, wd_ref, h_ref,
                                              after=ln)


def _layer_norm_pieces(src, g_ref, b_ref, dst_ref, n_pieces):
    rows = src.shape[0] // n_pieces
    pieces = []
    for p in range(n_pieces):
        ln = _layer_norm(src[p * rows:(p + 1) * rows, :], g_ref[...], b_ref[...])
        dst_ref[p * rows:(p + 1) * rows, :] = ln
        pieces.append(ln)
    return pieces


def _zero_after(v, width):
    t = jnp.sum(v, axis=0, keepdims=True)
    folded = t[:, 0:width]
    for c0 in range(width, t.shape[1], width):
        folded = folded + t[:, c0:c0 + width]
    return folded * 0.0


def _swiglu(xb, wg_ref, wu_ref, wd_ref, h_ref, after=None):
    d_ff = wg_ref.shape[1]
    chunks = list(range(0, d_ff, FFN_CHUNK))
    for c0 in chunks:
        gate = jnp.dot(xb, wg_ref[:, c0:c0 + FFN_CHUNK], preferred_element_type=_F32)
        up = jnp.dot(xb, wu_ref[:, c0:c0 + FFN_CHUNK], preferred_element_type=_F32)
        k = c0 // FFN_CHUNK - 1
        if after is not None and 0 <= k < len(after):
            up = up + _zero_after(after[k], FFN_CHUNK)
        h_ref[:, c0:c0 + FFN_CHUNK] = (_silu(gate) * up).astype(_BF16)
    return jnp.dot(h_ref[...], wd_ref[...], preferred_element_type=_F32)


def _const_spec(shape):
    zeros = (0,) * len(shape)
    return pl.BlockSpec(shape, lambda *_: zeros, pipeline_mode=pl.Buffered(1))


def _ffn_ln(x, wg, wu, wd, g, b):
    n, d = x.shape
    d_ff = wg.shape[1]
    tm = min(FFN_ROWS, n)
    nt = n // tm
    return pl.pallas_call(
        _ffn_ln_kernel,
        grid=(nt + 1,),
        in_specs=[pl.BlockSpec((tm, d), lambda i: (jnp.minimum(i, nt - 1), 0)),
                  _const_spec((d, d_ff)), _const_spec((d, d_ff)), _const_spec((d_ff, d)),
                  _const_spec((1, d)), _const_spec((1, d))],
        out_specs=pl.BlockSpec((tm, d), lambda i: (jnp.maximum(i - 1, 0), 0)),
        out_shape=jax.ShapeDtypeStruct((n, d), _F32),
        scratch_shapes=[pltpu.VMEM((tm, d_ff), _BF16), pltpu.VMEM((tm, d), _F32)],
        compiler_params=pltpu.CompilerParams(dimension_semantics=("arbitrary",),
                                             vmem_limit_bytes=VMEM_LIMIT),
        name="ffn_ln",
    )(x, wg.astype(_BF16), wu.astype(_BF16), wd.astype(_BF16), g, b)


_QKV = 3 * GDN_WIDTH
_NAT_KV = _QKV + GDN_WIDTH + LANES
_T_VS = NSA_WIDTH
_T_VW = _T_VS + NSA_KV_WIDTH
_T_GATE = _T_VW + NSA_KV_WIDTH
_T_ROWS = _T_GATE + 32
Q_SCALE = NSA_HEAD_DIM ** -0.5 * 1.4426950408889634


def _proj_kernel(x_ref, xh_ref, wn_ref, wt_ref, cw_ref, qkv_ref, z_ref, small_ref, kc_ref, vc_ref,
                 ks_ref, kw_ref, qt_ref, vst_ref, vwt_ref, gt_ref, *, qt, tiles_per_seq):
    hk_n, grp, dh = NSA_KV_HEADS, NSA_GROUP, NSA_HEAD_DIM
    xb = x_ref[...].astype(_BF16)
    tm = xb.shape[0]
    halo = xh_ref.shape[0]
    at_start = pl.program_id(0) % tiles_per_seq == 0
    xh = jnp.where(at_start, 0.0, xh_ref[...]).astype(_BF16)
    xcat = jnp.concatenate([xh, xb], axis=0)

    def conv_chunk(c0, width):
        cols = slice(c0, c0 + width)
        pre = jnp.dot(xcat, wn_ref[:, cols], preferred_element_type=_F32)
        conv = cw_ref[GDN_CONV - 1:GDN_CONV, cols] * pre[halo:, :]
        for d in range(1, GDN_CONV):
            conv = conv + cw_ref[GDN_CONV - 1 - d:GDN_CONV - d, cols] * pltpu.roll(pre, d, 0)[halo:, :]
        qkv_ref[:, cols] = _silu(conv)

    def nat_cols(ref, c0, width, r0=0):
        ref[:, r0:r0 + width] = jnp.dot(xb, wn_ref[:, c0:c0 + width], preferred_element_type=_F32)

    def kv_heads():
        kv = jnp.dot(xb, wn_ref[:, _NAT_KV:_NAT_KV + 4 * NSA_KV_WIDTH], preferred_element_type=_F32)
        for hk in range(hk_n):
            for idx, ref in enumerate((kc_ref, vc_ref, ks_ref, kw_ref)):
                c0 = idx * NSA_KV_WIDTH + hk * dh
                ref[0, hk] = kv[:, c0:c0 + dh].astype(ref.dtype)

    def q_heads(hk):
        r0 = hk * grp * dh
        yt = lax.dot_general(wt_ref[r0:r0 + grp * dh, :], xb, (((1,), (1,)), ((), ())),
                             preferred_element_type=_F32)
        for ii in range(tm // qt):
            for g in range(grp):
                qt_ref[0, hk, ii, :, g * qt:(g + 1) * qt] = (
                    yt[g * dh:(g + 1) * dh, ii * qt:(ii + 1) * qt] * Q_SCALE).astype(_BF16)

    def v_gates():
        yt = lax.dot_general(wt_ref[_T_VS:_T_ROWS, :], xb, (((1,), (1,)), ((), ())),
                             preferred_element_type=_F32)
        for hk in range(hk_n):
            for ii in range(tm // qt):
                cols = slice(ii * qt, (ii + 1) * qt)
                vst_ref[0, hk, ii] = yt[hk * dh:(hk + 1) * dh, cols].astype(_BF16)
                vwt_ref[0, hk, ii] = yt[NSA_KV_WIDTH + hk * dh:NSA_KV_WIDTH + (hk + 1) * dh,
                                        cols].astype(_BF16)
                for g in range(grp):
                    for c in range(3):
                        r = 2 * NSA_KV_WIDTH + (hk * grp + g) * 3 + c
                        gt_ref[0, hk, ii, c:c + 1, g * qt:(g + 1) * qt] = yt[r:r + 1, cols]

    step = 2 * LANES
    others = [lambda: nat_cols(z_ref, _QKV, step), lambda: nat_cols(z_ref, _QKV + step, step, step),
              lambda: (nat_cols(small_ref, _QKV + GDN_WIDTH, LANES), kv_heads()),
              lambda: q_heads(0), lambda: q_heads(1), v_gates]
    assert _QKV // step == len(others) and GDN_WIDTH == 2 * step and hk_n == 2
    for k, other in enumerate(others):
        conv_chunk(k * step, step)
        other()


def _proj_weights(w):
    d = w.shape[0]
    sizes = (GDN_WIDTH,) * 4 + (GDN_HEADS,) * 2 + (NSA_WIDTH,) + (NSA_KV_WIDTH,) * 6 + (3 * NSA_Q_HEADS,)
    off = np.concatenate([[0], np.cumsum(sizes)])
    gq, gk, gv, gz, gb, ga, nq, kc, vc, ks, vs, kw, vw, gate = [
        w[:, int(off[i]):int(off[i + 1])] for i in range(len(sizes))]
    small = jnp.concatenate([gb, ga, jnp.zeros((d, LANES - 2 * GDN_HEADS), w.dtype)], axis=1)
    w_nat = jnp.concatenate([gq, gk, gv, gz, small, kc, vc, ks, kw], axis=1)
    w_t = jnp.concatenate([nq, vs, vw, gate, jnp.zeros((d, _T_ROWS - _T_GATE - gate.shape[1]), w.dtype)],
                          axis=1).T
    return w_nat.astype(_BF16), w_t.astype(_BF16)


def _in_proj(x, w_nat, w_t, conv_w, bsz, t):
    n, d = x.shape
    hk, grp, dh = NSA_KV_HEADS, NSA_GROUP, NSA_HEAD_DIM
    tm = min(PROJ_ROWS, t)
    qt = min(NSA_QT, t)
    nt = t // tm
    halo = 16
    halo_spec = pl.BlockSpec((halo, d), lambda r: (jnp.maximum(r * (tm // halo) - 1, 0), 0))
    row = lambda width: pl.BlockSpec((tm, width), lambda r: (r, 0))
    head = pl.BlockSpec((1, hk, tm, dh), lambda r: (r // nt, 0, r % nt, 0))
    tile = lambda rows, lanes: pl.BlockSpec((1, hk, tm // qt, rows, lanes),
                                            lambda r: (r // nt, 0, r % nt, 0, 0))
    sds = jax.ShapeDtypeStruct
    return pl.pallas_call(
        functools.partial(_proj_kernel, qt=qt, tiles_per_seq=nt),
        grid=(n // tm,),
        in_specs=[pl.BlockSpec((tm, d), lambda r: (r, 0)), halo_spec, _const_spec(w_nat.shape),
                  _const_spec(w_t.shape), _const_spec(conv_w.shape)],
        out_specs=[row(_QKV), row(GDN_WIDTH), row(LANES), head, head, head, head,
                   tile(dh, grp * qt), tile(dh, qt), tile(dh, qt), tile(3, grp * qt)],
        out_shape=[sds((n, _QKV), _F32), sds((n, GDN_WIDTH), _F32), sds((n, LANES), _F32),
                   sds((bsz, hk, t, dh), _F32), sds((bsz, hk, t, dh), _F32),
                   sds((bsz, hk, t, dh), _BF16), sds((bsz, hk, t, dh), _BF16),
                   sds((bsz, hk, t // qt, dh, grp * qt), _BF16),
                   sds((bsz, hk, t // qt, dh, qt), _BF16), sds((bsz, hk, t // qt, dh, qt), _BF16),
                   sds((bsz, hk, t // qt, 3, grp * qt), _F32)],
        compiler_params=pltpu.CompilerParams(dimension_semantics=("parallel",),
                                             vmem_limit_bytes=VMEM_LIMIT),
        name="in_proj",
    )(x, x, w_nat, w_t, conv_w)


def _bdot(a, b):
    return jnp.dot(a, b, preferred_element_type=_F32)


def _unit_lower_inverses(lmats):
    c = lmats[0].shape[0]
    eye = jnp.where(lax.broadcasted_iota(jnp.int32, (c, c), 0)
                    == lax.broadcasted_iota(jnp.int32, (c, c), 1), 1.0, 0.0)
    rs = [eye - l for l in lmats]
    pb = [l.astype(_BF16) for l in lmats]
    ps = [_bdot(l, l) for l in pb]
    span = 2
    while True:
        pb = [p.astype(_BF16) for p in ps]
        rs = [r + _bdot(r.astype(_BF16), p) for r, p in zip(rs, pb)]
        span *= 2
        if span >= c:
            return rs
        ps = [_bdot(p, p) for p in pb]


def _gdn_kernel(act_ref, z_ref, bac_ref, bar_ref, hpr_ref, hpc_ref, nw_ref, o_ref, s_ref, *, tb):
    nh, hd, c = GDN_HEADS, GDN_HEAD_DIM, GDN_CHUNK

    @pl.when(pl.program_id(1) == 0)
    def _():
        s_ref[...] = jnp.zeros_like(s_ref)

    row = lax.broadcasted_iota(jnp.int32, (c, c), 0)
    col = lax.broadcasted_iota(jnp.int32, (c, c), 1)
    causal = row >= col
    strict = row > col
    tril = jnp.where(causal, 1.0, 0.0).astype(_BF16)
    triu = jnp.where(row <= col, 1.0, 0.0).astype(_BF16)

    nchunk = tb // c
    qg_l, qb_l, kb_l, kbeta_l, rhs_l, kd_l, decay_l, eg_l = [], [], [], [], [], [], [], []
    for ci in range(nchunk):
        r0 = ci * c
        bac = bac_ref[0, r0:r0 + c, :]
        beta_all = _sigmoid(bac)
        g_all = -jnp.exp(hpr_ref[0:1, :]) * _softplus(bac + hpr_ref[1:2, :])
        gc_all = sum(_bdot(tril, part) for part in _split3(g_all))
        bar = bar_ref[0, ci]
        gr_all = -jnp.exp(hpc_ref[:, 0:1]) * _softplus(bar + hpc_ref[:, 1:2])
        gcr_all = sum(_bdot(part, triu) for part in _split3(gr_all))
        for h in range(nh):
            q = act_ref[0, r0:r0 + c, h * hd:(h + 1) * hd]
            k = act_ref[0, r0:r0 + c, (nh + h) * hd:(nh + h + 1) * hd]
            v = act_ref[0, r0:r0 + c, (2 * nh + h) * hd:(2 * nh + h + 1) * hd]
            q = q * lax.rsqrt(jnp.sum(q * q, axis=-1, keepdims=True) + L2_EPS) * (hd ** -0.5)
            k = k * lax.rsqrt(jnp.sum(k * k, axis=-1, keepdims=True) + L2_EPS)
            beta = beta_all[:, h:h + 1]
            gcc = gc_all[:, nh + h:nh + h + 1]
            gcr = gcr_all[nh + h:nh + h + 1, :]
            glast = gc_all[c - 1:c, nh + h:nh + h + 1]
            decay_l.append(jnp.exp(jnp.where(causal, gcc - gcr, NEG_BIG)))
            kbeta = k * beta
            egc = jnp.exp(gcc)
            qg_l.append(q * egc)
            qb_l.append(q.astype(_BF16))
            kb_l.append(k.astype(_BF16))
            kbeta_l.append(kbeta.astype(_BF16))
            rhs_l.append(jnp.concatenate([kbeta * egc, v * beta], axis=1).astype(_BF16))
            kd_l.append((k * jnp.exp(glast - gcc)).astype(_BF16))
            eg_l.append(jnp.exp(glast))

    nt = (((1,), (1,)), ((), ()))
    tn = (((0,), (0,)), ((), ()))
    kk_l = [lax.dot_general(a, b, nt, preferred_element_type=_F32) for a, b in zip(kbeta_l, kb_l)]
    qk_l = [lax.dot_general(a, b, nt, preferred_element_type=_F32) for a, b in zip(qb_l, kb_l)]
    lmat_l = [jnp.where(strict, kk * d, 0.0) for kk, d in zip(kk_l, decay_l)]
    tinv_l = _unit_lower_inverses(lmat_l)
    wu_l = [_bdot(t.astype(_BF16), r).astype(_BF16) for t, r in zip(tinv_l, rhs_l)]
    qkb_l = [(qk * d).astype(_BF16) for qk, d in zip(qk_l, decay_l)]
    pn_l = [lax.dot_general(kd, wu, tn, preferred_element_type=_F32) for kd, wu in zip(kd_l, wu_l)]
    ab_l = [_bdot(qk, wu) for qk, wu in zip(qkb_l, wu_l)]
    a_l = [(qg - ab[:, :hd]).astype(_BF16) for qg, ab in zip(qg_l, ab_l)]
    p_l = [pn[:, :hd].astype(_BF16) for pn in pn_l]

    states = [s_ref[h] for h in range(nh)]
    for ci in range(nchunk):
        r0 = ci * c
        sb = [s.astype(_BF16) for s in states]
        outs = [_bdot(a_l[ci * nh + h], sb[h]) + ab_l[ci * nh + h][:, hd:] for h in range(nh)]
        states = [states[h] * eg_l[ci * nh + h] + pn_l[ci * nh + h][:, hd:]
                  - _bdot(p_l[ci * nh + h], sb[h]) for h in range(nh)]
        for h in range(nh):
            o = outs[h]
            zz = z_ref[0, r0:r0 + c, h * hd:(h + 1) * hd]
            o = (o * lax.rsqrt(jnp.mean(o * o, axis=-1, keepdims=True) + RMS_EPS) * nw_ref[...]
                 * _silu(zz))
            o_ref[0, r0:r0 + c, h * hd:(h + 1) * hd] = o
    for h in range(nh):
        s_ref[h] = states[h]


def _gdn(qkv, z, small, a_log, dt_bias, norm_w):
    bsz, t, width = qkv.shape
    nh, c = GDN_HEADS, GDN_CHUNK
    tb = min(GDN_ROWS, t)
    ba_rows = small[:, :, :8].reshape(bsz, t // c, c, 8).transpose(0, 1, 3, 2)
    hp_row = jnp.zeros((8, LANES), _F32)
    hp_row = hp_row.at[0, nh:2 * nh].set(a_log).at[1, nh:2 * nh].set(dt_bias)
    hp_col = jnp.zeros((8, LANES), _F32)
    hp_col = hp_col.at[nh:2 * nh, 0].set(a_log).at[nh:2 * nh, 1].set(dt_bias)
    return pl.pallas_call(
        functools.partial(_gdn_kernel, tb=tb),
        grid=(bsz, t // tb),
        in_specs=[pl.BlockSpec((1, tb, width), lambda b, i: (b, i, 0)),
                  pl.BlockSpec((1, tb, GDN_WIDTH), lambda b, i: (b, i, 0)),
                  pl.BlockSpec((1, tb, LANES), lambda b, i: (b, i, 0)),
                  pl.BlockSpec((1, tb // c, 8, c), lambda b, i: (b, i, 0, 0)),
                  pl.BlockSpec((8, LANES), lambda b, i: (0, 0)),
                  pl.BlockSpec((8, LANES), lambda b, i: (0, 0)),
                  pl.BlockSpec((1, GDN_HEAD_DIM), lambda b, i: (0, 0))],
        out_specs=pl.BlockSpec((1, tb, GDN_WIDTH), lambda b, i: (b, i, 0)),
        out_shape=jax.ShapeDtypeStruct((bsz, t, GDN_WIDTH), _F32),
        scratch_shapes=[pltpu.VMEM((nh, GDN_HEAD_DIM, GDN_HEAD_DIM), _F32)],
        compiler_params=pltpu.CompilerParams(dimension_semantics=("parallel", "arbitrary"),
                                             vmem_limit_bytes=VMEM_LIMIT),
        name="gdn",
    )(qkv, z, small, ba_rows, hp_row, hp_col, norm_w.reshape(1, -1))


def _gelu_tanh(x):
    return 0.5 * x * (1.0 + jnp.tanh(0.7978845608028654 * (x + 0.044715 * (x * x * x))))


def _cmp_kernel(xk_ref, xv_ref, pk_ref, pv_ref, wk1_ref, wk2_ref, wv1_ref, wv2t_ref, ok_ref, ovt_ref):
    def hidden(x, pos_ref, w1_ref):
        n = x.shape[0]
        ya = _mm(x + pos_ref[0:1, :], w1_ref[0])
        yb = _mm(x + pos_ref[1:2, :], w1_ref[1])
        hid = ya + pltpu.roll(yb, n - 1, 0)
        return _gelu_tanh(hid).astype(_BF16)

    ok_ref[0, 0] = _bdot(hidden(xk_ref[0, 0], pk_ref, wk1_ref), wk2_ref[...]).astype(_BF16)
    ovt_ref[0, 0] = lax.dot_general(wv2t_ref[...], hidden(xv_ref[0, 0], pv_ref, wv1_ref),
                                    (((1,), (1,)), ((), ())),
                                    preferred_element_type=_F32).astype(_BF16)


def _nsa_compress(kc, vc, pos_k, pos_v, k_w1, k_w2, v_w1, v_w2):
    bsz, hk, t, dh = kc.shape
    half = NSA_CMP_STRIDE * dh
    nrow = t // NSA_CMP_STRIDE
    hidden = k_w1.shape[1]
    xk = kc.reshape(bsz, hk, nrow, half)
    xv = vc.reshape(bsz, hk, nrow, half)
    x_spec = pl.BlockSpec((1, 1, nrow, half), lambda b, h: (b, h, 0, 0))
    full = lambda shape: pl.BlockSpec(shape, lambda b, h: (0,) * len(shape))
    return pl.pallas_call(
        _cmp_kernel,
        grid=(bsz, hk),
        in_specs=[x_spec, x_spec, full((2, half)), full((2, half)),
                  full((2, half, hidden)), full((hidden, dh)),
                  full((2, half, hidden)), full((dh, hidden))],
        out_specs=[pl.BlockSpec((1, 1, nrow, dh), lambda b, h: (b, h, 0, 0)),
                   pl.BlockSpec((1, 1, dh, nrow), lambda b, h: (b, h, 0, 0))],
        out_shape=[jax.ShapeDtypeStruct((bsz, hk, nrow, dh), _BF16),
                   jax.ShapeDtypeStruct((bsz, hk, dh, nrow), _BF16)],
        compiler_params=pltpu.CompilerParams(dimension_semantics=("parallel", "parallel"),
                                             vmem_limit_bytes=VMEM_LIMIT),
        name="nsa_cmp",
    )(xk, xv, pos_k.reshape(2, half), pos_v.reshape(2, half),
      k_w1.reshape(2, half, hidden).astype(_BF16), k_w2.astype(_BF16),
      v_w1.reshape(2, half, hidden).astype(_BF16), v_w2.T.astype(_BF16))


def _nsa_kernel(q_ref, kc_ref, vct_ref, cst_ref, ks_ref, vst_ref, kw_ref, vwt_ref, gt_ref, o_ref,
                kaug_ref, acc_ref, s_buf, p_buf, bias_ref, *, qt, kt, n_sel):
    assert qt == kt
    grp, dh = NSA_GROUP, NSA_HEAD_DIM
    lanes = grp * qt
    t_all = ks_ref.shape[2]
    ns = t_all // NSA_SEL_LEN
    nc = kc_ref.shape[2]
    i = pl.program_id(2)
    s0 = i * qt

    @pl.when(i == 0)
    def _():
        rblk = lax.broadcasted_iota(jnp.int32, (t_all, ns), 0) // NSA_SEL_LEN
        cblk = lax.broadcasted_iota(jnp.int32, (t_all, ns), 1)
        kaug_ref[:, 0:dh] = ks_ref[0, 0]
        kaug_ref[:, dh:dh + ns] = jnp.where(rblk == cblk, 1.0, 0.0).astype(_BF16)
        krow = lax.broadcasted_iota(jnp.int32, (kt, lanes), 0)
        tl = lax.broadcasted_iota(jnp.int32, (kt, lanes), 1) & (qt - 1)
        far_off = ((NSA_WINDOW + kt - 1) // kt) * kt - NSA_WINDOW
        bias_ref[0] = jnp.where(krow <= tl, 0.0, NEG_BIG)
        bias_ref[1] = jnp.where(krow > tl + far_off, 0.0, NEG_BIG)
        bias_ref[2] = jnp.full((kt, lanes), NEG_BIG, _F32)

    lane = lax.broadcasted_iota(jnp.int32, (1, lanes), 1)
    tq = s0 + (lane & (qt - 1))
    qs = q_ref[0, 0, 0]

    sc = jnp.dot(kc_ref[0, 0], qs, preferred_element_type=_F32)
    cend = lax.broadcasted_iota(jnp.int32, (nc, 1), 0) * NSA_CMP_STRIDE + (NSA_CMP_LEN - 1)
    valid = cend <= tq
    scm = jnp.where(valid, sc, NEG_BIG)
    mc = jnp.max(scm, axis=0, keepdims=True)
    ec = jnp.where(valid, jnp.exp2(scm - mc), 0.0)
    lc = jnp.sum(ec, axis=0, keepdims=True)
    pc = ec / jnp.where(lc > 0.0, lc, 1.0)
    o_cmp = jnp.dot(vct_ref[0, 0], pc.astype(_BF16), preferred_element_type=_F32)

    psum = pc[:, 0:qt]
    for g in range(1, grp):
        psum = psum + pc[:, g * qt:(g + 1) * qt]
    p_hi = psum.astype(_BF16)
    p_lo = (psum - p_hi.astype(_F32)).astype(_BF16)
    cst = cst_ref[...]
    imp = (jnp.dot(cst, p_hi, preferred_element_type=_F32)
           + jnp.dot(cst, p_lo, preferred_element_type=_F32))
    blk = lax.broadcasted_iota(jnp.int32, (ns, qt), 0)
    tq1 = s0 + lax.broadcasted_iota(jnp.int32, (ns, qt), 1)
    cur = tq1 // NSA_SEL_LEN
    visible = blk * NSA_SEL_LEN <= tq1
    forced = (blk == 0) | (blk == cur) | (blk == cur - 1)
    imp = jnp.where(visible, jnp.where(forced, NSA_FORCE_SCORE, imp), -1.0)
    sub = lax.broadcasted_iota(jnp.int32, (8, qt), 0)
    imp_rows = [imp[8 * r:8 * r + 8, :] for r in range(ns // 8)]
    ranks = [jnp.zeros((8, qt), _F32) for _ in imp_rows]
    for j in range(ns):
        rowj = imp[j:j + 1, :]
        for r, rows in enumerate(imp_rows):
            if 8 * r > j:
                beats = jnp.where(rowj >= rows, 1.0, 0.0)
            elif 8 * r + 7 <= j:
                beats = jnp.where(rowj > rows, 1.0, 0.0)
            else:
                beats = jnp.where(sub + 8 * r > j, jnp.where(rowj >= rows, 1.0, 0.0),
                                  jnp.where(rowj > rows, 1.0, 0.0))
            ranks[r] = ranks[r] + beats
    rank = jnp.concatenate(ranks, axis=0)
    bias = jnp.where(rank < n_sel, 0.0, NEG_BIG).astype(_BF16)
    qaug = jnp.concatenate([qs, jnp.concatenate([bias] * grp, axis=1)], axis=0)

    m0 = jnp.full((1, lanes), M_INIT, _F32)
    ones_rows = jnp.where(lax.broadcasted_iota(jnp.int32, (acc_ref.shape[0] - dh, kt), 0) == 0,
                          1.0, 0.0).astype(_BF16)

    def v_aug(v_tile):
        return jnp.concatenate([v_tile, ones_rows], axis=0)

    def window_branch():
        nwt = (NSA_WINDOW + kt - 1) // kt + 1
        s_win, j_win = [], []
        for d in range(nwt):
            back = nwt - 1 - d
            j = i - back
            jc = jnp.maximum(j, 0)
            k_tile = kw_ref[0, 0, pl.ds(pl.multiple_of(jc * kt, kt), kt), :]
            s = jnp.dot(k_tile, qs, preferred_element_type=_F32)
            if back == 0:
                s = s + bias_ref[0]
            elif back == nwt - 1:
                s = s + bias_ref[jnp.where(j >= 0, 1, 2)]
            else:
                s = s + jnp.where(j >= 0, 0.0, NEG_BIG)
            s_win.append(s)
            j_win.append(jc)
        mw = s_win[0].max(axis=0, keepdims=True)
        for s in s_win[1:]:
            mw = jnp.maximum(mw, s.max(axis=0, keepdims=True))
        acc_w = jnp.zeros(acc_ref.shape, _F32)
        for s, jc in zip(s_win, j_win):
            acc_w = acc_w + jnp.dot(v_aug(vwt_ref[0, 0, jc]), jnp.exp2((s - mw).astype(_BF16)),
                                    preferred_element_type=_F32)
        return acc_w[0:dh] / acc_w[dh:dh + 1]

    o_win = window_branch()

    n_s, n_p = s_buf.shape[0], p_buf.shape[0]
    last_tile = t_all // kt - 1

    def qk_tile(j):
        start = pl.multiple_of(jnp.minimum(j, last_tile) * kt, kt)
        return jnp.dot(kaug_ref[pl.ds(start, kt), :], qaug, preferred_element_type=_F32)

    def softmax_step(s, m):
        m_new = jnp.maximum(m, jnp.max(s, axis=0, keepdims=True))
        return m_new, jnp.exp2(m - m_new), jnp.exp2((s - m_new).astype(_BF16))

    acc_ref[...] = jnp.zeros_like(acc_ref)
    p_buf[n_p - 1] = jnp.zeros((kt, lanes), _BF16)
    s_buf[0] = qk_tile(0)
    s_buf[1] = qk_tile(1)

    def sel_step(j, c, carry):
        m, a_prev = carry
        s_buf[(c + 2) % n_s] = qk_tile(j + 2)
        pv = jnp.dot(v_aug(vst_ref[0, 0, jnp.maximum(j - 1, 0)]), p_buf[(c - 1) % n_p],
                     preferred_element_type=_F32)
        m, a, p = softmax_step(s_buf[c], m)
        p_buf[c % n_p] = p
        acc_ref[...] = a_prev * acc_ref[...] + pv
        return m, a

    def sel_round(jj, carry):
        for c in range(n_s):
            carry = sel_step(n_s * jj + c, c, carry)
        return carry

    carry = lax.fori_loop(0, i // n_s, sel_round, (m0, jnp.ones((1, lanes), _F32)))
    base = (i // n_s) * n_s
    rem = i - base
    carry = lax.cond(rem >= 2, lambda x: sel_step(base + 1, 1, sel_step(base, 0, x)), lambda x: x,
                     carry)
    carry = lax.cond(rem == 1, functools.partial(sel_step, base, 0), lambda x: x, carry)
    carry = lax.cond(rem == 3, functools.partial(sel_step, base + 2, 2), lambda x: x, carry)
    m, a_prev = carry
    pv = jnp.dot(v_aug(vst_ref[0, 0, jnp.maximum(i - 1, 0)]), p_buf[(i + n_p - 1) % n_p],
                 preferred_element_type=_F32)
    acc = a_prev * acc_ref[...] + pv
    s = s_buf[i % n_s] + bias_ref[0]
    m, a, p = softmax_step(s, m)
    acc = a * acc + jnp.dot(v_aug(vst_ref[0, 0, i]), p, preferred_element_type=_F32)
    o_sel = acc[0:dh] / acc[dh:dh + 1]

    gate = _sigmoid(gt_ref[0, 0, 0])
    out_t = gate[0:1, :] * o_cmp + gate[1:2, :] * o_sel + gate[2:3, :] * o_win
    o_ref[0] = jnp.concatenate([out_t[:, g * qt:(g + 1) * qt] for g in range(grp)], axis=0).T


def _nsa_attention(qt_arr, k_cmp, v_cmp_t, cst, ks, vs_t, kw, vw_t, gates_t):
    bsz, hk, nq, dh, lanes = qt_arr.shape
    qt = lanes // NSA_GROUP
    kt = qt
    t = ks.shape[2]
    ns = t // NSA_SEL_LEN
    nc = k_cmp.shape[2]
    ntile = t // kt
    per_head = lambda shape: pl.BlockSpec((1, 1) + shape, lambda b, h, i: (b, h) + (0,) * len(shape))
    per_tile = lambda shape: pl.BlockSpec((1, 1, 1) + shape,
                                          lambda b, h, i: (b, h, i) + (0,) * len(shape))
    return pl.pallas_call(
        functools.partial(_nsa_kernel, qt=qt, kt=kt, n_sel=min(NSA_TOPN, ns)),
        grid=(bsz, hk, nq),
        in_specs=[per_tile((dh, lanes)),
                  per_head((nc, dh)), per_head((dh, nc)),
                  pl.BlockSpec((ns, nc), lambda b, h, i: (0, 0)),
                  per_head((t, dh)), per_head((ntile, dh, kt)),
                  per_head((t, dh)), per_head((ntile, dh, kt)),
                  per_tile((3, lanes))],
        out_specs=pl.BlockSpec((1, qt, NSA_GROUP * dh), lambda b, h, i: (b, i, h)),
        out_shape=jax.ShapeDtypeStruct((bsz, t, hk * NSA_GROUP * dh), _F32),
        scratch_shapes=[pltpu.VMEM((t, dh + ns), _BF16),
                        pltpu.VMEM((dh + 16, lanes), _F32),
                        pltpu.VMEM((4, kt, lanes), _F32),
                        pltpu.VMEM((2, kt, lanes), _BF16),
                        pltpu.VMEM((3, kt, lanes), _F32)],
        compiler_params=pltpu.CompilerParams(
            dimension_semantics=("parallel", "parallel", "arbitrary"),
            vmem_limit_bytes=VMEM_LIMIT),
        name="nsa_attn",
    )(qt_arr, k_cmp, v_cmp_t, cst, ks, vs_t, kw, vw_t, gates_t)


def _cmp_to_sel_t(t):
    ncmp = (t - NSA_CMP_LEN) // NSA_CMP_STRIDE + 1
    c_start = np.arange(ncmp) * NSA_CMP_STRIDE
    s_start = np.arange(t // NSA_SEL_LEN) * NSA_SEL_LEN
    overlap = np.clip(np.minimum(c_start[:, None] + NSA_CMP_LEN, s_start[None, :] + NSA_SEL_LEN)
                      - np.maximum(c_start[:, None], s_start[None, :]), 0, None) / NSA_CMP_LEN
    out = np.zeros((t // NSA_SEL_LEN, t // NSA_CMP_STRIDE), np.float32)
    out[:, :ncmp] = overlap.T
    return jnp.asarray(out, _BF16)


def _nsa(q_t, kc, vc, ks, kw, vs_t, vw_t, gates_t, pos_k, pos_v, k_w1, k_w2, v_w1, v_w2):
    k_cmp, v_cmp_t = _nsa_compress(kc, vc, pos_k, pos_v, k_w1, k_w2, v_w1, v_w2)
    return _nsa_attention(q_t, k_cmp, v_cmp_t, _cmp_to_sel_t(ks.shape[2]), ks, vs_t, kw, vw_t, gates_t)


def _out_ffn_kernel(x_ref, a_ref, b_ref, wa_ref, wb_ref, g2_ref, b2_ref, wg_ref, wu_ref, wd_ref,
                    g3_ref, b3_ref, o_ref, h_ref, y_ref):
    @pl.when(pl.program_id(0) == 0)
    def _():
        y_ref[...] = jnp.zeros_like(y_ref)

    ln3 = _layer_norm_pieces(y_ref, g3_ref, b3_ref, o_ref, 8)
    mix = (jnp.dot(a_ref[...].astype(_BF16), wa_ref[...], preferred_element_type=_F32)
           + jnp.dot(b_ref[...].astype(_BF16), wb_ref[...], preferred_element_type=_F32))
    x2 = _layer_norm(DN_ALPHA * x_ref[...] + mix, g2_ref[...], b2_ref[...])
    y_ref[...] = DN_ALPHA * x2 + 0.5 * _swiglu(x2.astype(_BF16), wg_ref, wu_ref, wd_ref, h_ref,
                                               after=ln3)


def _out_ffn(x, oa, ob, wa, wb, g2, b2, wg, wu, wd, g3, b3):
    n, d = x.shape
    d_ff = wg.shape[1]
    tm = min(FFN_ROWS, n)
    nt = n // tm
    row = lambda width: pl.BlockSpec((tm, width), lambda i: (jnp.minimum(i, nt - 1), 0))
    return pl.pallas_call(
        _out_ffn_kernel,
        grid=(nt + 1,),
        in_specs=[row(d), row(oa.shape[1]), row(ob.shape[1]), _const_spec(wa.shape),
                  _const_spec(wb.shape), _const_spec((1, d)), _const_spec((1, d)),
                  _const_spec((d, d_ff)), _const_spec((d, d_ff)), _const_spec((d_ff, d)),
                  _const_spec((1, d)), _const_spec((1, d))],
        out_specs=pl.BlockSpec((tm, d), lambda i: (jnp.maximum(i - 1, 0), 0)),
        out_shape=jax.ShapeDtypeStruct((n, d), _F32),
        scratch_shapes=[pltpu.VMEM((tm, d_ff), _BF16), pltpu.VMEM((tm, d), _F32)],
        compiler_params=pltpu.CompilerParams(dimension_semantics=("arbitrary",),
                                             vmem_limit_bytes=VMEM_LIMIT),
        name="out_ffn",
    )(x, oa, ob, wa, wb, g2, b2, wg.astype(_BF16), wu.astype(_BF16), wd.astype(_BF16), g3, b3)


def kernel(x, ln1_g, ln1_b, ffn1_wg, ffn1_wu, ffn1_wd, w_in, gdn_conv_w, gdn_a_log, gdn_dt_bias,
           gdn_norm_w, nsa_cmp_pos_k, nsa_cmp_pos_v, nsa_cmp_k_w1, nsa_cmp_k_w2, nsa_cmp_v_w1,
           nsa_cmp_v_w2, w_out, ln2_g, ln2_b, ffn2_wg, ffn2_wu, ffn2_wd, ln3_g, ln3_b):
    bsz, t, d = x.shape
    n = bsz * t
    h = x.reshape(n, d)
    for i in range(DEPTH):
        h = _ffn_ln(h, ffn1_wg[i], ffn1_wu[i], ffn1_wd[i], ln1_g[i].reshape(1, d), ln1_b[i].reshape(1, d))
        w_nat, w_t = _proj_weights(w_in[i])
        qkv, z, small, kc, vc, ks, kw, q_t, vs_t, vw_t, gates_t = _in_proj(
            h, w_nat, w_t, gdn_conv_w[i], bsz, t)
        o_gdn = _gdn(qkv.reshape(bsz, t, _QKV), z.reshape(bsz, t, GDN_WIDTH),
                     small.reshape(bsz, t, LANES), gdn_a_log[i], gdn_dt_bias[i], gdn_norm_w[i])
        o_nsa = _nsa(q_t, kc, vc, ks, kw, vs_t, vw_t, gates_t,
                     nsa_cmp_pos_k[i], nsa_cmp_pos_v[i], nsa_cmp_k_w1[i], nsa_cmp_k_w2[i],
                     nsa_cmp_v_w1[i], nsa_cmp_v_w2[i])
        wo = w_out[i].astype(_BF16)
        h = _out_ffn(h, o_gdn.reshape(n, GDN_WIDTH), o_nsa.reshape(n, NSA_WIDTH),
                     wo[:GDN_WIDTH], wo[GDN_WIDTH:], ln2_g[i].reshape(1, d), ln2_b[i].reshape(1, d),
                     ffn2_wg[i], ffn2_wu[i], ffn2_wd[i], ln3_g[i].reshape(1, d), ln3_b[i].reshape(1, d))
    return h.reshape(bsz, t, d)
```

```python
import functools

import numpy as np
import jax
import jax.numpy as jnp
from jax import lax
from jax.experimental import pallas as pl
from jax.experimental.pallas import tpu as pltpu

_F32 = jnp.float32
_BF16 = jnp.bfloat16

GDN_HEADS = 4
GDN_HEAD_DIM = 128
GDN_WIDTH = GDN_HEADS * GDN_HEAD_DIM
GDN_CONV = 4
GDN_CHUNK = 64

NSA_Q_HEADS = 8
NSA_KV_HEADS = 2
NSA_GROUP = NSA_Q_HEADS // NSA_KV_HEADS
NSA_HEAD_DIM = 64
NSA_WIDTH = NSA_Q_HEADS * NSA_HEAD_DIM
NSA_KV_WIDTH = NSA_KV_HEADS * NSA_HEAD_DIM
NSA_CMP_LEN = 32
NSA_CMP_STRIDE = 16
NSA_SEL_LEN = 64
NSA_TOPN = 16
NSA_WINDOW = 512
NSA_FORCE_SCORE = 1e4

LN_EPS = 1e-5
RMS_EPS = 1e-6
L2_EPS = 1e-6
DEPTH = 1
DN_ALPHA = (2 * DEPTH) ** 0.25

NEG_BIG = -(2.0 ** 100)
M_INIT = -3.0e38

LANES = 128
VMEM_LIMIT = 56 * 1024 * 1024

FFN_ROWS = 512
FFN_CHUNK = 256
PROJ_ROWS = 512
GDN_ROWS = 256
NSA_QT = 256
NSA_KT = 256


def _sigmoid(x):
    return 0.5 * jnp.tanh(0.5 * x) + 0.5


def _silu(x):
    h = 0.5 * x
    return h * jnp.tanh(h) + h


def _softplus(x):
    return jnp.maximum(x, 0.0) + jnp.log(1.0 + jnp.exp(-jnp.abs(x)))


def _mm(a, b):
    return jnp.dot(a.astype(_BF16), b.astype(_BF16), preferred_element_type=_F32)


def _split3(x):
    hi = x.astype(_BF16)
    r = x - hi.astype(_F32)
    mid = r.astype(_BF16)
    lo = (r - mid.astype(_F32)).astype(_BF16)
    return hi, mid, lo


def _layer_norm(y, g, b):
    mu = jnp.mean(y, axis=-1, keepdims=True)
    yc = y - mu
    var = jnp.mean(yc * yc, axis=-1, keepdims=True)
    return yc * lax.rsqrt(var + LN_EPS) * g + b


def _ffn_ln_kernel(x_ref, wg_ref, wu_ref, wd_ref, g_ref, b_ref, o_ref, h_ref, y_ref):
    @pl.when(pl.program_id(0) == 0)
    def _():
        y_ref[...] = jnp.zeros_like(y_ref)

    ln = _layer_norm_pieces(y_ref, g_ref, b_ref, o_ref, 8)
    x = x_ref[...]
    y_ref[...] = DN_ALPHA * x + 0.5 * _swiglu(x.astype(_BF16), wg_ref, wu_ref, wd_ref, h_ref,
                                              after=ln)


def _layer_norm_pieces(src, g_ref, b_ref, dst_ref, n_pieces):
    rows = src.shape[0] // n_pieces
    pieces = []
    for p in range(n_pieces):
        ln = _layer_norm(src[p * rows:(p + 1) * rows, :], g_ref[...], b_ref[...])
        dst_ref[p * rows:(p + 1) * rows, :] = ln
        pieces.append(ln)
    return pieces


def _zero_after(v, width):
    t = jnp.sum(v, axis=0, keepdims=True)
    folded = t[:, 0:width]
    for c0 in range(width, t.shape[1], width):
        folded = folded + t[:, c0:c0 + width]
    return folded * 0.0


def _swiglu(xb, wg_ref, wu_ref, wd_ref, h_ref, after=None):
    d_ff = wg_ref.shape[1]
    chunks = list(range(0, d_ff, FFN_CHUNK))
    for c0 in chunks:
        gate = jnp.dot(xb, wg_ref[:, c0:c0 + FFN_CHUNK], preferred_element_type=_F32)
        up = jnp.dot(xb, wu_ref[:, c0:c0 + FFN_CHUNK], preferred_element_type=_F32)
        k = c0 // FFN_CHUNK - 1
        if after is not None and 0 <= k < len(after):
            up = up + _zero_after(after[k], FFN_CHUNK)
        h_ref[:, c0:c0 + FFN_CHUNK] = (_silu(gate) * up).astype(_BF16)
    return jnp.dot(h_ref[...], wd_ref[...], preferred_element_type=_F32)


def _const_spec(shape):
    zeros = (0,) * len(shape)
    return pl.BlockSpec(shape, lambda *_: zeros, pipeline_mode=pl.Buffered(1))


def _ffn_ln(x, wg, wu, wd, g, b):
    n, d = x.shape
    d_ff = wg.shape[1]
    tm = min(FFN_ROWS, n)
    nt = n // tm
    return pl.pallas_call(
        _ffn_ln_kernel,
        grid=(nt + 1,),
        in_specs=[pl.BlockSpec((tm, d), lambda i: (jnp.minimum(i, nt - 1), 0)),
                  _const_spec((d, d_ff)), _const_spec((d, d_ff)), _const_spec((d_ff, d)),
                  _const_spec((1, d)), _const_spec((1, d))],
        out_specs=pl.BlockSpec((tm, d), lambda i: (jnp.maximum(i - 1, 0), 0)),
        out_shape=jax.ShapeDtypeStruct((n, d), _F32),
        scratch_shapes=[pltpu.VMEM((tm, d_ff), _BF16), pltpu.VMEM((tm, d), _F32)],
        compiler_params=pltpu.CompilerParams(dimension_semantics=("arbitrary",),
                                             vmem_limit_bytes=VMEM_LIMIT),
        name="ffn_ln",
    )(x, wg.astype(_BF16), wu.astype(_BF16), wd.astype(_BF16), g, b)


_QKV = 3 * GDN_WIDTH
_NAT_KV = _QKV + GDN_WIDTH + LANES
_T_VS = NSA_WIDTH
_T_VW = _T_VS + NSA_KV_WIDTH
_T_GATE = _T_VW + NSA_KV_WIDTH
_T_ROWS = _T_GATE + 32
Q_SCALE = NSA_HEAD_DIM ** -0.5 * 1.4426950408889634


def _proj_kernel(x_ref, xh_ref, wn_ref, wt_ref, cw_ref, qkv_ref, z_ref, small_ref, kc_ref, vc_ref,
                 ks_ref, kw_ref, qt_ref, vst_ref, vwt_ref, gt_ref, *, qt, tiles_per_seq):
    hk_n, grp, dh = NSA_KV_HEADS, NSA_GROUP, NSA_HEAD_DIM
    xb = x_ref[...].astype(_BF16)
    tm = xb.shape[0]
    halo = xh_ref.shape[0]
    at_start = pl.program_id(0) % tiles_per_seq == 0
    xh = jnp.where(at_start, 0.0, xh_ref[...]).astype(_BF16)
    xcat = jnp.concatenate([xh, xb], axis=0)

    def conv_chunk(c0, width):
        cols = slice(c0, c0 + width)
        pre = jnp.dot(xcat, wn_ref[:, cols], preferred_element_type=_F32)
        conv = cw_ref[GDN_CONV - 1:GDN_CONV, cols] * pre[halo:, :]
        for d in range(1, GDN_CONV):
            conv = conv + cw_ref[GDN_CONV - 1 - d:GDN_CONV - d, cols] * pltpu.roll(pre, d, 0)[halo:, :]
        qkv_ref[:, cols] = _silu(conv)

    def nat_cols(ref, c0, width, r0=0):
        ref[:, r0:r0 + width] = jnp.dot(xb, wn_ref[:, c0:c0 + width], preferred_element_type=_F32)

    def kv_heads():
        kv = jnp.dot(xb, wn_ref[:, _NAT_KV:_NAT_KV + 4 * NSA_KV_WIDTH], preferred_element_type=_F32)
        for hk in range(hk_n):
            for idx, ref in enumerate((kc_ref, vc_ref, ks_ref, kw_ref)):
                c0 = idx * NSA_KV_WIDTH + hk * dh
                ref[0, hk] = kv[:, c0:c0 + dh].astype(ref.dtype)

    def q_heads(hk):
        r0 = hk * grp * dh
        yt = lax.dot_general(wt_ref[r0:r0 + grp * dh, :], xb, (((1,), (1,)), ((), ())),
                             preferred_element_type=_F32)
        for ii in range(tm // qt):
            for g in range(grp):
                qt_ref[0, hk, ii, :, g * qt:(g + 1) * qt] = (
                    yt[g * dh:(g + 1) * dh, ii * qt:(ii + 1) * qt] * Q_SCALE).astype(_BF16)

    def v_gates():
        yt = lax.dot_general(wt_ref[_T_VS:_T_ROWS, :], xb, (((1,), (1,)), ((), ())),
                             preferred_element_type=_F32)
        for hk in range(hk_n):
            for ii in range(tm // qt):
                cols = slice(ii * qt, (ii + 1) * qt)
                vst_ref[0, hk, ii] = yt[hk * dh:(hk + 1) * dh, cols].astype(_BF16)
                vwt_ref[0, hk, ii] = yt[NSA_KV_WIDTH + hk * dh:NSA_KV_WIDTH + (hk + 1) * dh,
                                        cols].astype(_BF16)
                for g in range(grp):
                    for c in range(3):
                        r = 2 * NSA_KV_WIDTH + (hk * grp + g) * 3 + c
                        gt_ref[0, hk, ii, c:c + 1, g * qt:(g + 1) * qt] = yt[r:r + 1, cols]

    step = 2 * LANES
    others = [lambda: nat_cols(z_ref, _QKV, step), lambda: nat_cols(z_ref, _QKV + step, step, step),
              lambda: (nat_cols(small_ref, _QKV + GDN_WIDTH, LANES), kv_heads()),
              lambda: q_heads(0), lambda: q_heads(1), v_gates]
    assert _QKV // step == len(others) and GDN_WIDTH == 2 * step and hk_n == 2
    for k, other in enumerate(others):
        conv_chunk(k * step, step)
        other()


def _proj_weights(w):
    d = w.shape[0]
    sizes = (GDN_WIDTH,) * 4 + (GDN_HEADS,) * 2 + (NSA_WIDTH,) + (NSA_KV_WIDTH,) * 6 + (3 * NSA_Q_HEADS,)
    off = np.concatenate([[0], np.cumsum(sizes)])
    gq, gk, gv, gz, gb, ga, nq, kc, vc, ks, vs, kw, vw, gate = [
        w[:, int(off[i]):int(off[i + 1])] for i in range(len(sizes))]
    small = jnp.concatenate([gb, ga, jnp.zeros((d, LANES - 2 * GDN_HEADS), w.dtype)], axis=1)
    w_nat = jnp.concatenate([gq, gk, gv, gz, small, kc, vc, ks, kw], axis=1)
    w_t = jnp.concatenate([nq, vs, vw, gate, jnp.zeros((d, _T_ROWS - _T_GATE - gate.shape[1]), w.dtype)],
                          axis=1).T
    return w_nat.astype(_BF16), w_t.astype(_BF16)


def _in_proj(x, w_nat, w_t, conv_w, bsz, t):
    n, d = x.shape
    hk, grp, dh = NSA_KV_HEADS, NSA_GROUP, NSA_HEAD_DIM
    tm = min(PROJ_ROWS, t)
    qt = min(NSA_QT, t)
    nt = t // tm
    halo = 16
    halo_spec = pl.BlockSpec((halo, d), lambda r: (jnp.maximum(r * (tm // halo) - 1, 0), 0))
    row = lambda width: pl.BlockSpec((tm, width), lambda r: (r, 0))
    head = pl.BlockSpec((1, hk, tm, dh), lambda r: (r // nt, 0, r % nt, 0))
    tile = lambda rows, lanes: pl.BlockSpec((1, hk, tm // qt, rows, lanes),
                                            lambda r: (r // nt, 0, r % nt, 0, 0))
    sds = jax.ShapeDtypeStruct
    return pl.pallas_call(
        functools.partial(_proj_kernel, qt=qt, tiles_per_seq=nt),
        grid=(n // tm,),
        in_specs=[pl.BlockSpec((tm, d), lambda r: (r, 0)), halo_spec, _const_spec(w_nat.shape),
                  _const_spec(w_t.shape), _const_spec(conv_w.shape)],
        out_specs=[row(_QKV), row(GDN_WIDTH), row(LANES), head, head, head, head,
                   tile(dh, grp * qt), tile(dh, qt), tile(dh, qt), tile(3, grp * qt)],
        out_shape=[sds((n, _QKV), _F32), sds((n, GDN_WIDTH), _F32), sds((n, LANES), _F32),
                   sds((bsz, hk, t, dh), _F32), sds((bsz, hk, t, dh), _F32),
                   sds((bsz, hk, t, dh), _BF16), sds((bsz, hk, t, dh), _BF16),
                   sds((bsz, hk, t // qt, dh, grp * qt), _BF16),
                   sds((bsz, hk, t // qt, dh, qt), _BF16), sds((bsz, hk, t // qt, dh, qt), _BF16),
                   sds((bsz, hk, t // qt, 3, grp * qt), _F32)],
        compiler_params=pltpu.CompilerParams(dimension_semantics=("parallel",),
                                             vmem_limit_bytes=VMEM_LIMIT),
        name="in_proj",
    )(x, x, w_nat, w_t, conv_w)


def _bdot(a, b):
    return jnp.dot(a, b, preferred_element_type=_F32)


def _unit_lower_inverses(lmats):
    c = lmats[0].shape[0]
    eye = jnp.where(lax.broadcasted_iota(jnp.int32, (c, c), 0)
                    == lax.broadcasted_iota(jnp.int32, (c, c), 1), 1.0, 0.0)
    rs = [eye - l for l in lmats]
    pb = [l.astype(_BF16) for l in lmats]
    ps = [_bdot(l, l) for l in pb]
    span = 2
    while True:
        pb = [p.astype(_BF16) for p in ps]
        rs = [r + _bdot(r.astype(_BF16), p) for r, p in zip(rs, pb)]
        span *= 2
        if span >= c:
            return rs
        ps = [_bdot(p, p) for p in pb]


def _gdn_kernel(act_ref, z_ref, bac_ref, bar_ref, hpr_ref, hpc_ref, nw_ref, o_ref, s_ref, *, tb):
    nh, hd, c = GDN_HEADS, GDN_HEAD_DIM, GDN_CHUNK

    @pl.when(pl.program_id(1) == 0)
    def _():
        s_ref[...] = jnp.zeros_like(s_ref)

    row = lax.broadcasted_iota(jnp.int32, (c, c), 0)
    col = lax.broadcasted_iota(jnp.int32, (c, c), 1)
    causal = row >= col
    strict = row > col
    tril = jnp.where(causal, 1.0, 0.0).astype(_BF16)
    triu = jnp.where(row <= col, 1.0, 0.0).astype(_BF16)

    nchunk = tb // c
    qg_l, qb_l, kb_l, kbeta_l, rhs_l, kd_l, decay_l, eg_l = [], [], [], [], [], [], [], []
    for ci in range(nchunk):
        r0 = ci * c
        bac = bac_ref[0, r0:r0 + c, :]
        beta_all = _sigmoid(bac)
        g_all = -jnp.exp(hpr_ref[0:1, :]) * _softplus(bac + hpr_ref[1:2, :])
        gc_all = sum(_bdot(tril, part) for part in _split3(g_all))
        bar = bar_ref[0, ci]
        gr_all = -jnp.exp(hpc_ref[:, 0:1]) * _softplus(bar + hpc_ref[:, 1:2])
        gcr_all = sum(_bdot(part, triu) for part in _split3(gr_all))
        for h in range(nh):
            q = act_ref[0, r0:r0 + c, h * hd:(h + 1) * hd]
            k = act_ref[0, r0:r0 + c, (nh + h) * hd:(nh + h + 1) * hd]
            v = act_ref[0, r0:r0 + c, (2 * nh + h) * hd:(2 * nh + h + 1) * hd]
            q = q * lax.rsqrt(jnp.sum(q * q, axis=-1, keepdims=True) + L2_EPS) * (hd ** -0.5)
            k = k * lax.rsqrt(jnp.sum(k * k, axis=-1, keepdims=True) + L2_EPS)
            beta = beta_all[:, h:h + 1]
            gcc = gc_all[:, nh + h:nh + h + 1]
            gcr = gcr_all[nh + h:nh + h + 1, :]
            glast = gc_all[c - 1:c, nh + h:nh + h + 1]
            decay_l.append(jnp.exp(jnp.where(causal, gcc - gcr, NEG_BIG)))
            kbeta = k * beta
            egc = jnp.exp(gcc)
            qg_l.append(q * egc)
            qb_l.append(q.astype(_BF16))
            kb_l.append(k.astype(_BF16))
            kbeta_l.append(kbeta.astype(_BF16))
            rhs_l.append(jnp.concatenate([kbeta * egc, v * beta], axis=1).astype(_BF16))
            kd_l.append((k * jnp.exp(glast - gcc)).astype(_BF16))
            eg_l.append(jnp.exp(glast))

    nt = (((1,), (1,)), ((), ()))
    tn = (((0,), (0,)), ((), ()))
    kk_l = [lax.dot_general(a, b, nt, preferred_element_type=_F32) for a, b in zip(kbeta_l, kb_l)]
    qk_l = [lax.dot_general(a, b, nt, preferred_element_type=_F32) for a, b in zip(qb_l, kb_l)]
    lmat_l = [jnp.where(strict, kk * d, 0.0) for kk, d in zip(kk_l, decay_l)]
    tinv_l = _unit_lower_inverses(lmat_l)
    wu_l = [_bdot(t.astype(_BF16), r).astype(_BF16) for t, r in zip(tinv_l, rhs_l)]
    qkb_l = [(qk * d).astype(_BF16) for qk, d in zip(qk_l, decay_l)]
    pn_l = [lax.dot_general(kd, wu, tn, preferred_element_type=_F32) for kd, wu in zip(kd_l, wu_l)]
    ab_l = [_bdot(qk, wu) for qk, wu in zip(qkb_l, wu_l)]
    a_l = [(qg - ab[:, :hd]).astype(_BF16) for qg, ab in zip(qg_l, ab_l)]
    p_l = [pn[:, :hd].astype(_BF16) for pn in pn_l]

    states = [s_ref[h] for h in range(nh)]
    for ci in range(nchunk):
        r0 = ci * c
        sb = [s.astype(_BF16) for s in states]
        outs = [_bdot(a_l[ci * nh + h], sb[h]) + ab_l[ci * nh + h][:, hd:] for h in range(nh)]
        states = [states[h] * eg_l[ci * nh + h] + pn_l[ci * nh + h][:, hd:]
                  - _bdot(p_l[ci * nh + h], sb[h]) for h in range(nh)]
        for h in range(nh):
            o = outs[h]
            zz = z_ref[0, r0:r0 + c, h * hd:(h + 1) * hd]
            o = (o * lax.rsqrt(jnp.mean(o * o, axis=-1, keepdims=True) + RMS_EPS) * nw_ref[...]
                 * _silu(zz))
            o_ref[0, r0:r0 + c, h * hd:(h + 1) * hd] = o
    for h in range(nh):
        s_ref[h] = states[h]


def _gdn(qkv, z, small, a_log, dt_bias, norm_w):
    bsz, t, width = qkv.shape
    nh, c = GDN_HEADS, GDN_CHUNK
    tb = min(GDN_ROWS, t)
    ba_rows = small[:, :, :8].reshape(bsz, t // c, c, 8).transpose(0, 1, 3, 2)
    hp_row = jnp.zeros((8, LANES), _F32)
    hp_row = hp_row.at[0, nh:2 * nh].set(a_log).at[1, nh:2 * nh].set(dt_bias)
    hp_col = jnp.zeros((8, LANES), _F32)
    hp_col = hp_col.at[nh:2 * nh, 0].set(a_log).at[nh:2 * nh, 1].set(dt_bias)
    return pl.pallas_call(
        functools.partial(_gdn_kernel, tb=tb),
        grid=(bsz, t // tb),
        in_specs=[pl.BlockSpec((1, tb, width), lambda b, i: (b, i, 0)),
                  pl.BlockSpec((1, tb, GDN_WIDTH), lambda b, i: (b, i, 0)),
                  pl.BlockSpec((1, tb, LANES), lambda b, i: (b, i, 0)),
                  pl.BlockSpec((1, tb // c, 8, c), lambda b, i: (b, i, 0, 0)),
                  pl.BlockSpec((8, LANES), lambda b, i: (0, 0)),
                  pl.BlockSpec((8, LANES), lambda b, i: (0, 0)),
                  pl.BlockSpec((1, GDN_HEAD_DIM), lambda b, i: (0, 0))],
        out_specs=pl.BlockSpec((1, tb, GDN_WIDTH), lambda b, i: (b, i, 0)),
        out_shape=jax.ShapeDtypeStruct((bsz, t, GDN_WIDTH), _F32),
        scratch_shapes=[pltpu.VMEM((nh, GDN_HEAD_DIM, GDN_HEAD_DIM), _F32)],
        compiler_params=pltpu.CompilerParams(dimension_semantics=("parallel", "arbitrary"),
                                             vmem_limit_bytes=VMEM_LIMIT),
        name="gdn",
    )(qkv, z, small, ba_rows, hp_row, hp_col, norm_w.reshape(1, -1))


def _gelu_tanh(x):
    return 0.5 * x * (1.0 + jnp.tanh(0.7978845608028654 * (x + 0.044715 * (x * x * x))))


def _cmp_kernel(xk_ref, xv_ref, pk_ref, pv_ref, wk1_ref, wk2_ref, wv1_ref, wv2t_ref, ok_ref, ovt_ref):
    def hidden(x, pos_ref, w1_ref):
        n = x.shape[0]
        ya = _mm(x + pos_ref[0:1, :], w1_ref[0])
        yb = _mm(x + pos_ref[1:2, :], w1_ref[1])
        hid = ya + pltpu.roll(yb, n - 1, 0)
        return _gelu_tanh(hid).astype(_BF16)

    ok_ref[0, 0] = _bdot(hidden(xk_ref[0, 0], pk_ref, wk1_ref), wk2_ref[...]).astype(_BF16)
    ovt_ref[0, 0] = lax.dot_general(wv2t_ref[...], hidden(xv_ref[0, 0], pv_ref, wv1_ref),
                                    (((1,), (1,)), ((), ())),
                                    preferred_element_type=_F32).astype(_BF16)


def _nsa_compress(kc, vc, pos_k, pos_v, k_w1, k_w2, v_w1, v_w2):
    bsz, hk, t, dh = kc.shape
    half = NSA_CMP_STRIDE * dh
    nrow = t // NSA_CMP_STRIDE
    hidden = k_w1.shape[1]
    xk = kc.reshape(bsz, hk, nrow, half)
    xv = vc.reshape(bsz, hk, nrow, half)
    x_spec = pl.BlockSpec((1, 1, nrow, half), lambda b, h: (b, h, 0, 0))
    full = lambda shape: pl.BlockSpec(shape, lambda b, h: (0,) * len(shape))
    return pl.pallas_call(
        _cmp_kernel,
        grid=(bsz, hk),
        in_specs=[x_spec, x_spec, full((2, half)), full((2, half)),
                  full((2, half, hidden)), full((hidden, dh)),
                  full((2, half, hidden)), full((dh, hidden))],
        out_specs=[pl.BlockSpec((1, 1, nrow, dh), lambda b, h: (b, h, 0, 0)),
                   pl.BlockSpec((1, 1, dh, nrow), lambda b, h: (b, h, 0, 0))],
        out_shape=[jax.ShapeDtypeStruct((bsz, hk, nrow, dh), _BF16),
                   jax.ShapeDtypeStruct((bsz, hk, dh, nrow), _BF16)],
        compiler_params=pltpu.CompilerParams(dimension_semantics=("parallel", "parallel"),
                                             vmem_limit_bytes=VMEM_LIMIT),
        name="nsa_cmp",
    )(xk, xv, pos_k.reshape(2, half), pos_v.reshape(2, half),
      k_w1.reshape(2, half, hidden).astype(_BF16), k_w2.astype(_BF16),
      v_w1.reshape(2, half, hidden).astype(_BF16), v_w2.T.astype(_BF16))


def _nsa_kernel(q_ref, kc_ref, vct_ref, cst_ref, ks_ref, vst_ref, kw_ref, vwt_ref, gt_ref, o_ref,
                kaug_ref, acc_ref, s_buf, p_buf, bias_ref, *, qt, kt, n_sel):
    assert qt == kt
    grp, dh = NSA_GROUP, NSA_HEAD_DIM
    heads = range(q_ref.shape[1])
    lanes = grp * qt
    t_all = ks_ref.shape[2]
    ns = t_all // NSA_SEL_LEN
    nc = kc_ref.shape[2]
    i = pl.program_id(1)
    s0 = i * qt

    @pl.when(i == 0)
    def _():
        rblk = lax.broadcasted_iota(jnp.int32, (t_all, ns), 0) // NSA_SEL_LEN
        cblk = lax.broadcasted_iota(jnp.int32, (t_all, ns), 1)
        onehot = jnp.where(rblk == cblk, 1.0, 0.0).astype(_BF16)
        for h in heads:
            kaug_ref[h, :, 0:dh] = ks_ref[0, h]
            kaug_ref[h, :, dh:dh + ns] = onehot
        krow = lax.broadcasted_iota(jnp.int32, (kt, lanes), 0)
        tl = lax.broadcasted_iota(jnp.int32, (kt, lanes), 1) & (qt - 1)
        far_off = ((NSA_WINDOW + kt - 1) // kt) * kt - NSA_WINDOW
        bias_ref[0] = jnp.where(krow <= tl, 0.0, NEG_BIG)
        bias_ref[1] = jnp.where(krow > tl + far_off, 0.0, NEG_BIG)
        bias_ref[2] = jnp.full((kt, lanes), NEG_BIG, _F32)

    lane = lax.broadcasted_iota(jnp.int32, (1, lanes), 1)
    tq = s0 + (lane & (qt - 1))
    qs = [q_ref[0, h, 0] for h in heads]

    cend = lax.broadcasted_iota(jnp.int32, (nc, 1), 0) * NSA_CMP_STRIDE + (NSA_CMP_LEN - 1)
    valid = cend <= tq
    sc = [jnp.dot(kc_ref[0, h], qs[h], preferred_element_type=_F32) for h in heads]
    pc = []
    for h in heads:
        scm = jnp.where(valid, sc[h], NEG_BIG)
        mc = jnp.max(scm, axis=0, keepdims=True)
        ec = jnp.where(valid, jnp.exp2(scm - mc), 0.0)
        lc = jnp.sum(ec, axis=0, keepdims=True)
        pc.append(ec / jnp.where(lc > 0.0, lc, 1.0))
    o_cmp = [jnp.dot(vct_ref[0, h], pc[h].astype(_BF16), preferred_element_type=_F32)
             for h in heads]

    cst = cst_ref[...]
    blk = lax.broadcasted_iota(jnp.int32, (ns, qt), 0)
    tq1 = s0 + lax.broadcasted_iota(jnp.int32, (ns, qt), 1)
    cur = tq1 // NSA_SEL_LEN
    visible = blk * NSA_SEL_LEN <= tq1
    forced = (blk == 0) | (blk == cur) | (blk == cur - 1)
    sub = lax.broadcasted_iota(jnp.int32, (8, qt), 0)
    qaug = []
    for h in heads:
        psum = pc[h][:, 0:qt]
        for g in range(1, grp):
            psum = psum + pc[h][:, g * qt:(g + 1) * qt]
        p_hi = psum.astype(_BF16)
        p_lo = (psum - p_hi.astype(_F32)).astype(_BF16)
        imp = (jnp.dot(cst, p_hi, preferred_element_type=_F32)
               + jnp.dot(cst, p_lo, preferred_element_type=_F32))
        imp = jnp.where(visible, jnp.where(forced, NSA_FORCE_SCORE, imp), -1.0)
        imp_rows = [imp[8 * r:8 * r + 8, :] for r in range(ns // 8)]
        ranks = [jnp.zeros((8, qt), _F32) for _ in imp_rows]
        for j in range(ns):
            rowj = imp[j:j + 1, :]
            for r, rows in enumerate(imp_rows):
                if 8 * r > j:
                    beats = jnp.where(rowj >= rows, 1.0, 0.0)
                elif 8 * r + 7 <= j:
                    beats = jnp.where(rowj > rows, 1.0, 0.0)
                else:
                    beats = jnp.where(sub + 8 * r > j, jnp.where(rowj >= rows, 1.0, 0.0),
                                      jnp.where(rowj > rows, 1.0, 0.0))
                ranks[r] = ranks[r] + beats
        rank = jnp.concatenate(ranks, axis=0)
        bias = jnp.where(rank < n_sel, 0.0, NEG_BIG).astype(_BF16)
        qaug.append(jnp.concatenate([qs[h], jnp.concatenate([bias] * grp, axis=1)], axis=0))

    m0 = jnp.full((1, lanes), M_INIT, _F32)
    ones_rows = jnp.where(lax.broadcasted_iota(jnp.int32, (acc_ref.shape[1] - dh, kt), 0) == 0,
                          1.0, 0.0).astype(_BF16)

    def v_aug(v_tile):
        return jnp.concatenate([v_tile, ones_rows], axis=0)

    nwt = (NSA_WINDOW + kt - 1) // kt + 1
    s_win = [[] for _ in heads]
    j_win = []
    for d in range(nwt):
        back = nwt - 1 - d
        j = i - back
        jc = jnp.maximum(j, 0)
        j_win.append(jc)
        for h in heads:
            k_tile = kw_ref[0, h, pl.ds(pl.multiple_of(jc * kt, kt), kt), :]
            s = jnp.dot(k_tile, qs[h], preferred_element_type=_F32)
            if back == 0:
                s = s + bias_ref[0]
            elif back == nwt - 1:
                s = s + bias_ref[jnp.where(j >= 0, 1, 2)]
            else:
                s = s + jnp.where(j >= 0, 0.0, NEG_BIG)
            s_win[h].append(s)
    o_win = []
    for h in heads:
        mw = s_win[h][0].max(axis=0, keepdims=True)
        for s in s_win[h][1:]:
            mw = jnp.maximum(mw, s.max(axis=0, keepdims=True))
        acc_w = jnp.zeros(acc_ref.shape[1:], _F32)
        for s, jc in zip(s_win[h], j_win):
            acc_w = acc_w + jnp.dot(v_aug(vwt_ref[0, h, jc]), jnp.exp2((s - mw).astype(_BF16)),
                                    preferred_element_type=_F32)
        o_win.append(acc_w[0:dh] / acc_w[dh:dh + 1])

    n_s, n_p = s_buf.shape[1], p_buf.shape[1]
    last_tile = t_all // kt - 1

    def qk_tile(h, j):
        start = pl.multiple_of(jnp.minimum(j, last_tile) * kt, kt)
        return jnp.dot(kaug_ref[h, pl.ds(start, kt), :], qaug[h], preferred_element_type=_F32)

    def softmax_step(s, m):
        m_new = jnp.maximum(m, jnp.max(s, axis=0, keepdims=True))
        return m_new, jnp.exp2(m - m_new), jnp.exp2((s - m_new).astype(_BF16))

    acc_ref[...] = jnp.zeros_like(acc_ref)
    for h in heads:
        p_buf[h, n_p - 1] = jnp.zeros((kt, lanes), _BF16)
        s_buf[h, 0] = qk_tile(h, 0)
        s_buf[h, 1] = qk_tile(h, 1)

    def sel_step(j, c, carry):
        for h in heads:
            s_buf[h, (c + 2) % n_s] = qk_tile(h, j + 2)
        pv = [jnp.dot(v_aug(vst_ref[0, h, jnp.maximum(j - 1, 0)]), p_buf[h, (c - 1) % n_p],
                      preferred_element_type=_F32) for h in heads]
        out = []
        for h in heads:
            m, a_prev = carry[h]
            m, a, p = softmax_step(s_buf[h, c], m)
            p_buf[h, c % n_p] = p
            acc_ref[h] = a_prev * acc_ref[h] + pv[h]
            out.append((m, a))
        return tuple(out)

    def sel_round(jj, carry):
        for c in range(n_s):
            carry = sel_step(n_s * jj + c, c, carry)
        return carry

    carry = lax.fori_loop(0, i // n_s, sel_round,
                          tuple((m0, jnp.ones((1, lanes), _F32)) for _ in heads))
    base = (i // n_s) * n_s
    rem = i - base
    carry = lax.cond(rem >= 2, lambda x: sel_step(base + 1, 1, sel_step(base, 0, x)), lambda x: x,
                     carry)
    carry = lax.cond(rem == 1, functools.partial(sel_step, base, 0), lambda x: x, carry)
    carry = lax.cond(rem == 3, functools.partial(sel_step, base + 2, 2), lambda x: x, carry)
    out_rows = []
    for h in heads:
        m, a_prev = carry[h]
        pv = jnp.dot(v_aug(vst_ref[0, h, jnp.maximum(i - 1, 0)]), p_buf[h, (i + n_p - 1) % n_p],
                     preferred_element_type=_F32)
        acc = a_prev * acc_ref[h] + pv
        s = s_buf[h, i % n_s] + bias_ref[0]
        m, a, p = softmax_step(s, m)
        acc = a * acc + jnp.dot(v_aug(vst_ref[0, h, i]), p, preferred_element_type=_F32)
        o_sel = acc[0:dh] / acc[dh:dh + 1]
        gate = _sigmoid(gt_ref[0, h, 0])
        out_t = gate[0:1, :] * o_cmp[h] + gate[1:2, :] * o_sel + gate[2:3, :] * o_win[h]
        out_rows += [out_t[:, g * qt:(g + 1) * qt] for g in range(grp)]
    o_ref[0] = jnp.concatenate(out_rows, axis=0).T


def _nsa_attention(qt_arr, k_cmp, v_cmp_t, cst, ks, vs_t, kw, vw_t, gates_t):
    bsz, hk, nq, dh, lanes = qt_arr.shape
    qt = lanes // NSA_GROUP
    kt = qt
    t = ks.shape[2]
    ns = t // NSA_SEL_LEN
    nc = k_cmp.shape[2]
    ntile = t // kt
    per_seq = lambda shape: pl.BlockSpec((1, hk) + shape, lambda b, i: (b, 0) + (0,) * len(shape))
    per_tile = lambda shape: pl.BlockSpec((1, hk, 1) + shape,
                                          lambda b, i: (b, 0, i) + (0,) * len(shape))
    return pl.pallas_call(
        functools.partial(_nsa_kernel, qt=qt, kt=kt, n_sel=min(NSA_TOPN, ns)),
        grid=(bsz, nq),
        in_specs=[per_tile((dh, lanes)),
                  per_seq((nc, dh)), per_seq((dh, nc)),
                  pl.BlockSpec((ns, nc), lambda b, i: (0, 0)),
                  per_seq((t, dh)), per_seq((ntile, dh, kt)),
                  per_seq((t, dh)), per_seq((ntile, dh, kt)),
                  per_tile((3, lanes))],
        out_specs=pl.BlockSpec((1, qt, hk * NSA_GROUP * dh), lambda b, i: (b, i, 0)),
        out_shape=jax.ShapeDtypeStruct((bsz, t, hk * NSA_GROUP * dh), _F32),
        scratch_shapes=[pltpu.VMEM((hk, t, dh + ns), _BF16),
                        pltpu.VMEM((hk, dh + 16, lanes), _F32),
                        pltpu.VMEM((hk, 4, kt, lanes), _F32),
                        pltpu.VMEM((hk, 2, kt, lanes), _BF16),
                        pltpu.VMEM((3, kt, lanes), _F32)],
        compiler_params=pltpu.CompilerParams(
            dimension_semantics=("parallel", "arbitrary"),
            vmem_limit_bytes=VMEM_LIMIT),
        name="nsa_attn",
    )(qt_arr, k_cmp, v_cmp_t, cst, ks, vs_t, kw, vw_t, gates_t)


def _cmp_to_sel_t(t):
    ncmp = (t - NSA_CMP_LEN) // NSA_CMP_STRIDE + 1
    c_start = np.arange(ncmp) * NSA_CMP_STRIDE
    s_start = np.arange(t // NSA_SEL_LEN) * NSA_SEL_LEN
    overlap = np.clip(np.minimum(c_start[:, None] + NSA_CMP_LEN, s_start[None, :] + NSA_SEL_LEN)
                      - np.maximum(c_start[:, None], s_start[None, :]), 0, None) / NSA_CMP_LEN
    out = np.zeros((t // NSA_SEL_LEN, t // NSA_CMP_STRIDE), np.float32)
    out[:, :ncmp] = overlap.T
    return jnp.asarray(out, _BF16)


def _nsa(q_t, kc, vc, ks, kw, vs_t, vw_t, gates_t, pos_k, pos_v, k_w1, k_w2, v_w1, v_w2):
    k_cmp, v_cmp_t = _nsa_compress(kc, vc, pos_k, pos_v, k_w1, k_w2, v_w1, v_w2)
    return _nsa_attention(q_t, k_cmp, v_cmp_t, _cmp_to_sel_t(ks.shape[2]), ks, vs_t, kw, vw_t, gates_t)


def _out_ffn_kernel(x_ref, a_ref, b_ref, wa_ref, wb_ref, g2_ref, b2_ref, wg_ref, wu_ref, wd_ref,
                    g3_ref, b3_ref, o_ref, h_ref, y_ref):
    @pl.when(pl.program_id(0) == 0)
    def _():
        y_ref[...] = jnp.zeros_like(y_ref)

    ln3 = _layer_norm_pieces(y_ref, g3_ref, b3_ref, o_ref, 8)
    mix = (jnp.dot(a_ref[...].astype(_BF16), wa_ref[...], preferred_element_type=_F32)
           + jnp.dot(b_ref[...].astype(_BF16), wb_ref[...], preferred_element_type=_F32))
    x2 = _layer_norm(DN_ALPHA * x_ref[...] + mix, g2_ref[...], b2_ref[...])
    y_ref[...] = DN_ALPHA * x2 + 0.5 * _swiglu(x2.astype(_BF16), wg_ref, wu_ref, wd_ref, h_ref,
                                               after=ln3)


def _out_ffn(x, oa, ob, wa, wb, g2, b2, wg, wu, wd, g3, b3):
    n, d = x.shape
    d_ff = wg.shape[1]
    tm = min(FFN_ROWS, n)
    nt = n // tm
    row = lambda width: pl.BlockSpec((tm, width), lambda i: (jnp.minimum(i, nt - 1), 0))
    return pl.pallas_call(
        _out_ffn_kernel,
        grid=(nt + 1,),
        in_specs=[row(d), row(oa.shape[1]), row(ob.shape[1]), _const_spec(wa.shape),
                  _const_spec(wb.shape), _const_spec((1, d)), _const_spec((1, d)),
                  _const_spec((d, d_ff)), _const_spec((d, d_ff)), _const_spec((d_ff, d)),
                  _const_spec((1, d)), _const_spec((1, d))],
        out_specs=pl.BlockSpec((tm, d), lambda i: (jnp.maximum(i - 1, 0), 0)),
        out_shape=jax.ShapeDtypeStruct((n, d), _F32),
        scratch_shapes=[pltpu.VMEM((tm, d_ff), _BF16), pltpu.VMEM((tm, d), _F32)],
        compiler_params=pltpu.CompilerParams(dimension_semantics=("arbitrary",),
                                             vmem_limit_bytes=VMEM_LIMIT),
        name="out_ffn",
    )(x, oa, ob, wa, wb, g2, b2, wg.astype(_BF16), wu.astype(_BF16), wd.astype(_BF16), g3, b3)


def kernel(x, ln1_g, ln1_b, ffn1_wg, ffn1_wu, ffn1_wd, w_in, gdn_conv_w, gdn_a_log, gdn_dt_bias,
           gdn_norm_w, nsa_cmp_pos_k, nsa_cmp_pos_v, nsa_cmp_k_w1, nsa_cmp_k_w2, nsa_cmp_v_w1,
           nsa_cmp_v_w2, w_out, ln2_g, ln2_b, ffn2_wg, ffn2_wu, ffn2_wd, ln3_g, ln3_b):
    bsz, t, d = x.shape
    n = bsz * t
    h = x.reshape(n, d)
    for i in range(DEPTH):
        h = _ffn_ln(h, ffn1_wg[i], ffn1_wu[i], ffn1_wd[i], ln1_g[i].reshape(1, d), ln1_b[i].reshape(1, d))
        w_nat, w_t = _proj_weights(w_in[i])
        qkv, z, small, kc, vc, ks, kw, q_t, vs_t, vw_t, gates_t = _in_proj(
            h, w_nat, w_t, gdn_conv_w[i], bsz, t)
        o_gdn = _gdn(qkv.reshape(bsz, t, _QKV), z.reshape(bsz, t, GDN_WIDTH),
                     small.reshape(bsz, t, LANES), gdn_a_log[i], gdn_dt_bias[i], gdn_norm_w[i])
        o_nsa = _nsa(q_t, kc, vc, ks, kw, vs_t, vw_t, gates_t,
                     nsa_cmp_pos_k[i], nsa_cmp_pos_v[i], nsa_cmp_k_w1[i], nsa_cmp_k_w2[i],
                     nsa_cmp_v_w1[i], nsa_cmp_v_w2[i])
        wo = w_out[i].astype(_BF16)
        h = _out_ffn(h, o_gdn.reshape(n, GDN_WIDTH), o_nsa.reshape(n, NSA_WIDTH),
                     wo[:GDN_WIDTH], wo[GDN_WIDTH:], ln2_g[i].reshape(1, d), ln2_b[i].reshape(1, d),
                     ffn2_wg[i], ffn2_wu[i], ffn2_wd[i], ln3_g[i].reshape(1, d), ln3_b[i].reshape(1, d))
    return h.reshape(bsz, t, d)
```

```python
import functools

import numpy as np
import jax
import jax.numpy as jnp
from jax import lax
from jax.experimental import pallas as pl
from jax.experimental.pallas import tpu as pltpu

_F32 = jnp.float32
_BF16 = jnp.bfloat16

GDN_HEADS = 4
GDN_HEAD_DIM = 128
GDN_WIDTH = GDN_HEADS * GDN_HEAD_DIM
GDN_CONV = 4
GDN_CHUNK = 64

NSA_Q_HEADS = 8
NSA_KV_HEADS = 2
NSA_GROUP = NSA_Q_HEADS // NSA_KV_HEADS
NSA_HEAD_DIM = 64
NSA_WIDTH = NSA_Q_HEADS * NSA_HEAD_DIM
NSA_KV_WIDTH = NSA_KV_HEADS * NSA_HEAD_DIM
NSA_CMP_LEN = 32
NSA_CMP_STRIDE = 16
NSA_SEL_LEN = 64
NSA_TOPN = 16
NSA_WINDOW = 512
NSA_FORCE_SCORE = 1e4

LN_EPS = 1e-5
RMS_EPS = 1e-6
L2_EPS = 1e-6
DEPTH = 1
DN_ALPHA = (2 * DEPTH) ** 0.25

NEG_BIG = -(2.0 ** 100)
M_INIT = -3.0e38

LANES = 128
VMEM_LIMIT = 56 * 1024 * 1024

FFN_ROWS = 512
FFN_CHUNK = 256
PROJ_ROWS = 512
GDN_ROWS = 512
NSA_QT = 256
NSA_KT = 256


def _sigmoid(x):
    return 0.5 * jnp.tanh(0.5 * x) + 0.5


def _silu(x):
    h = 0.5 * x
    return h * jnp.tanh(h) + h


def _softplus(x):
    return jnp.maximum(x, 0.0) + jnp.log(1.0 + jnp.exp(-jnp.abs(x)))


def _mm(a, b):
    return jnp.dot(a.astype(_BF16), b.astype(_BF16), preferred_element_type=_F32)


def _split3(x):
    hi = x.astype(_BF16)
    r = x - hi.astype(_F32)
    mid = r.astype(_BF16)
    lo = (r - mid.astype(_F32)).astype(_BF16)
    return hi, mid, lo


def _layer_norm(y, g, b):
    mu = jnp.mean(y, axis=-1, keepdims=True)
    yc = y - mu
    var = jnp.mean(yc * yc, axis=-1, keepdims=True)
    return yc * lax.rsqrt(var + LN_EPS) * g + b


def _ffn_ln_kernel(x_ref, wg_ref, wu_ref, wd_ref, g_ref, b_ref, o_ref, h_ref, y_ref):
    @pl.when(pl.program_id(0) == 0)
    def _():
        y_ref[...] = jnp.zeros_like(y_ref)

    ln = _layer_norm_pieces(y_ref, g_ref, b_ref, o_ref, 8)
    x = x_ref[...]
    y_ref[...] = DN_ALPHA * x + 0.5 * _swiglu(x.astype(_BF16), wg_ref, wu_ref, wd_ref, h_ref,
                                              after=ln)


def _layer_norm_pieces(src, g_ref, b_ref, dst_ref, n_pieces):
    rows = src.shape[0] // n_pieces
    pieces = []
    for p in range(n_pieces):
        ln = _layer_norm(src[p * rows:(p + 1) * rows, :], g_ref[...], b_ref[...])
        dst_ref[p * rows:(p + 1) * rows, :] = ln
        pieces.append(ln)
    return pieces


def _zero_after(v, width):
    t = jnp.sum(v, axis=0, keepdims=True)
    folded = t[:, 0:width]
    for c0 in range(width, t.shape[1], width):
        folded = folded + t[:, c0:c0 + width]
    return folded * 0.0


def _swiglu(xb, wg_ref, wu_ref, wd_ref, h_ref, after=None):
    d_ff = wg_ref.shape[1]
    chunks = list(range(0, d_ff, FFN_CHUNK))
    for c0 in chunks:
        gate = jnp.dot(xb, wg_ref[:, c0:c0 + FFN_CHUNK], preferred_element_type=_F32)
        up = jnp.dot(xb, wu_ref[:, c0:c0 + FFN_CHUNK], preferred_element_type=_F32)
        k = c0 // FFN_CHUNK - 1
        if after is not None and 0 <= k < len(after):
            up = up + _zero_after(after[k], FFN_CHUNK)
        h_ref[:, c0:c0 + FFN_CHUNK] = (_silu(gate) * up).astype(_BF16)
    return jnp.dot(h_ref[...], wd_ref[...], preferred_element_type=_F32)


def _const_spec(shape):
    zeros = (0,) * len(shape)
    return pl.BlockSpec(shape, lambda *_: zeros, pipeline_mode=pl.Buffered(1))


def _ffn_ln(x, wg, wu, wd, g, b):
    n, d = x.shape
    d_ff = wg.shape[1]
    tm = min(FFN_ROWS, n)
    nt = n // tm
    return pl.pallas_call(
        _ffn_ln_kernel,
        grid=(nt + 1,),
        in_specs=[pl.BlockSpec((tm, d), lambda i: (jnp.minimum(i, nt - 1), 0)),
                  _const_spec((d, d_ff)), _const_spec((d, d_ff)), _const_spec((d_ff, d)),
                  _const_spec((1, d)), _const_spec((1, d))],
        out_specs=pl.BlockSpec((tm, d), lambda i: (jnp.maximum(i - 1, 0), 0)),
        out_shape=jax.ShapeDtypeStruct((n, d), _F32),
        scratch_shapes=[pltpu.VMEM((tm, d_ff), _BF16), pltpu.VMEM((tm, d), _F32)],
        compiler_params=pltpu.CompilerParams(dimension_semantics=("arbitrary",),
                                             vmem_limit_bytes=VMEM_LIMIT),
        name="ffn_ln",
    )(x, wg.astype(_BF16), wu.astype(_BF16), wd.astype(_BF16), g, b)


_QKV = 3 * GDN_WIDTH
_NAT_KV = _QKV + GDN_WIDTH + LANES
_T_VS = NSA_WIDTH
_T_VW = _T_VS + NSA_KV_WIDTH
_T_GATE = _T_VW + NSA_KV_WIDTH
_T_ROWS = _T_GATE + 32
Q_SCALE = NSA_HEAD_DIM ** -0.5 * 1.4426950408889634


def _proj_kernel(x_ref, xh_ref, wn_ref, wt_ref, cw_ref, qkv_ref, z_ref, small_ref, kc_ref, vc_ref,
                 ks_ref, kw_ref, qt_ref, vst_ref, vwt_ref, gt_ref, kvs_ref, *, qt, tiles_per_seq):
    hk_n, grp, dh = NSA_KV_HEADS, NSA_GROUP, NSA_HEAD_DIM
    xb = x_ref[...].astype(_BF16)
    tm = xb.shape[0]
    halo = xh_ref.shape[0]
    at_start = pl.program_id(0) % tiles_per_seq == 0
    xh = jnp.where(at_start, 0.0, xh_ref[...]).astype(_BF16)
    xcat = jnp.concatenate([xh, xb], axis=0)

    def conv_chunk(c0, width):
        cols = slice(c0, c0 + width)
        pre = jnp.dot(xcat, wn_ref[:, cols], preferred_element_type=_F32)
        conv = cw_ref[GDN_CONV - 1:GDN_CONV, cols] * pre[halo:, :]
        for d in range(1, GDN_CONV):
            conv = conv + cw_ref[GDN_CONV - 1 - d:GDN_CONV - d, cols] * pltpu.roll(pre, d, 0)[halo:, :]
        qkv_ref[:, cols] = _silu(conv)

    def nat_cols(ref, c0, width, r0=0):
        ref[:, r0:r0 + width] = jnp.dot(xb, wn_ref[:, c0:c0 + width], preferred_element_type=_F32)

    def kv_heads():
        kv = jnp.dot(xb, wn_ref[:, _NAT_KV:_NAT_KV + 4 * NSA_KV_WIDTH], preferred_element_type=_F32)
        for hk in range(hk_n):
            for idx, ref in enumerate((ks_ref, kw_ref)):
                c0 = (2 + idx) * NSA_KV_WIDTH + hk * dh
                ref[0, hk] = kv[:, c0:c0 + dh].astype(ref.dtype)
        stride = NSA_CMP_STRIDE
        for idx, ref in enumerate((kc_ref, vc_ref)):
            kvs_ref[idx] = kv[:, idx * NSA_KV_WIDTH:(idx + 1) * NSA_KV_WIDTH]
            for r in range(stride):
                rows = kvs_ref[idx, pl.ds(r, tm // stride, stride=stride), :]
                for hk in range(hk_n):
                    ref[0, hk, :, r * dh:(r + 1) * dh] = rows[:, hk * dh:(hk + 1) * dh]

    def q_heads(hk):
        r0 = hk * grp * dh
        yt = lax.dot_general(wt_ref[r0:r0 + grp * dh, :], xb, (((1,), (1,)), ((), ())),
                             preferred_element_type=_F32)
        for ii in range(tm // qt):
            for g in range(grp):
                qt_ref[0, hk, ii, :, g * qt:(g + 1) * qt] = (
                    yt[g * dh:(g + 1) * dh, ii * qt:(ii + 1) * qt] * Q_SCALE).astype(_BF16)

    def v_gates():
        yt = lax.dot_general(wt_ref[_T_VS:_T_ROWS, :], xb, (((1,), (1,)), ((), ())),
                             preferred_element_type=_F32)
        for hk in range(hk_n):
            for ii in range(tm // qt):
                cols = slice(ii * qt, (ii + 1) * qt)
                vst_ref[0, hk, ii] = yt[hk * dh:(hk + 1) * dh, cols].astype(_BF16)
                vwt_ref[0, hk, ii] = yt[NSA_KV_WIDTH + hk * dh:NSA_KV_WIDTH + (hk + 1) * dh,
                                        cols].astype(_BF16)
                for g in range(grp):
                    for c in range(3):
                        r = 2 * NSA_KV_WIDTH + (hk * grp + g) * 3 + c
                        gt_ref[0, hk, ii, c:c + 1, g * qt:(g + 1) * qt] = yt[r:r + 1, cols]

    step = 2 * LANES
    others = [lambda: nat_cols(z_ref, _QKV, step), lambda: nat_cols(z_ref, _QKV + step, step, step),
              lambda: (nat_cols(small_ref, _QKV + GDN_WIDTH, LANES), kv_heads()),
              lambda: q_heads(0), lambda: q_heads(1), v_gates]
    assert _QKV // step == len(others) and GDN_WIDTH == 2 * step and hk_n == 2
    for k, other in enumerate(others):
        conv_chunk(k * step, step)
        other()


def _proj_weights(w):
    d = w.shape[0]
    sizes = (GDN_WIDTH,) * 4 + (GDN_HEADS,) * 2 + (NSA_WIDTH,) + (NSA_KV_WIDTH,) * 6 + (3 * NSA_Q_HEADS,)
    off = np.concatenate([[0], np.cumsum(sizes)])
    gq, gk, gv, gz, gb, ga, nq, kc, vc, ks, vs, kw, vw, gate = [
        w[:, int(off[i]):int(off[i + 1])] for i in range(len(sizes))]
    small = jnp.concatenate([gb, ga, jnp.zeros((d, LANES - 2 * GDN_HEADS), w.dtype)], axis=1)
    w_nat = jnp.concatenate([gq, gk, gv, gz, small, kc, vc, ks, kw], axis=1)
    w_t = jnp.concatenate([nq, vs, vw, gate, jnp.zeros((d, _T_ROWS - _T_GATE - gate.shape[1]), w.dtype)],
                          axis=1).T
    return w_nat.astype(_BF16), w_t.astype(_BF16)


def _in_proj(x, w_nat, w_t, conv_w, bsz, t):
    n, d = x.shape
    hk, grp, dh = NSA_KV_HEADS, NSA_GROUP, NSA_HEAD_DIM
    tm = min(PROJ_ROWS, t)
    qt = min(NSA_QT, t)
    nt = t // tm
    halo = 16
    halo_spec = pl.BlockSpec((halo, d), lambda r: (jnp.maximum(r * (tm // halo) - 1, 0), 0))
    row = lambda width: pl.BlockSpec((tm, width), lambda r: (r, 0))
    head = pl.BlockSpec((1, hk, tm, dh), lambda r: (r // nt, 0, r % nt, 0))
    cs = NSA_CMP_STRIDE
    cmp_in = pl.BlockSpec((1, hk, tm // cs, cs * dh), lambda r: (r // nt, 0, r % nt, 0))
    tile = lambda rows, lanes: pl.BlockSpec((1, hk, tm // qt, rows, lanes),
                                            lambda r: (r // nt, 0, r % nt, 0, 0))
    sds = jax.ShapeDtypeStruct
    return pl.pallas_call(
        functools.partial(_proj_kernel, qt=qt, tiles_per_seq=nt),
        grid=(n // tm,),
        in_specs=[pl.BlockSpec((tm, d), lambda r: (r, 0)), halo_spec, _const_spec(w_nat.shape),
                  _const_spec(w_t.shape), _const_spec(conv_w.shape)],
        out_specs=[row(_QKV), row(GDN_WIDTH), row(LANES), cmp_in, cmp_in, head, head,
                   tile(dh, grp * qt), tile(dh, qt), tile(dh, qt), tile(3, grp * qt)],
        out_shape=[sds((n, _QKV), _F32), sds((n, GDN_WIDTH), _F32), sds((n, LANES), _F32),
                   sds((bsz, hk, t // cs, cs * dh), _F32), sds((bsz, hk, t // cs, cs * dh), _F32),
                   sds((bsz, hk, t, dh), _BF16), sds((bsz, hk, t, dh), _BF16),
                   sds((bsz, hk, t // qt, dh, grp * qt), _BF16),
                   sds((bsz, hk, t // qt, dh, qt), _BF16), sds((bsz, hk, t // qt, dh, qt), _BF16),
                   sds((bsz, hk, t // qt, 3, grp * qt), _F32)],
        scratch_shapes=[pltpu.VMEM((2, tm, NSA_KV_WIDTH), _F32)],
        compiler_params=pltpu.CompilerParams(dimension_semantics=("parallel",),
                                             vmem_limit_bytes=VMEM_LIMIT),
        name="in_proj",
    )(x, x, w_nat, w_t, conv_w)


def _bdot(a, b):
    return jnp.dot(a, b, preferred_element_type=_F32)


def _unit_lower_inverses(lmats):
    c = lmats[0].shape[0]
    eye = jnp.where(lax.broadcasted_iota(jnp.int32, (c, c), 0)
                    == lax.broadcasted_iota(jnp.int32, (c, c), 1), 1.0, 0.0)
    rs = [eye - l for l in lmats]
    pb = [l.astype(_BF16) for l in lmats]
    ps = [_bdot(l, l) for l in pb]
    span = 2
    while True:
        pb = [p.astype(_BF16) for p in ps]
        rs = [r + _bdot(r.astype(_BF16), p) for r, p in zip(rs, pb)]
        span *= 2
        if span >= c:
            return rs
        ps = [_bdot(p, p) for p in pb]


def _gdn_kernel(act_ref, z_ref, bac_ref, bar_ref, hpr_ref, hpc_ref, nw_ref, o_ref, s_ref, *, tb):
    nh, hd, c = GDN_HEADS, GDN_HEAD_DIM, GDN_CHUNK

    @pl.when(pl.program_id(1) == 0)
    def _():
        s_ref[...] = jnp.zeros_like(s_ref)

    row = lax.broadcasted_iota(jnp.int32, (c, c), 0)
    col = lax.broadcasted_iota(jnp.int32, (c, c), 1)
    causal = row >= col
    strict = row > col
    tril = jnp.where(causal, 1.0, 0.0).astype(_BF16)
    triu = jnp.where(row <= col, 1.0, 0.0).astype(_BF16)

    nchunk = tb // c
    qg_l, qb_l, kb_l, kbeta_l, rhs_l, kd_l, decay_l, eg_l = [], [], [], [], [], [], [], []
    for ci in range(nchunk):
        r0 = ci * c
        bac = bac_ref[0, r0:r0 + c, :]
        beta_all = _sigmoid(bac)
        g_all = -jnp.exp(hpr_ref[0:1, :]) * _softplus(bac + hpr_ref[1:2, :])
        gc_all = sum(_bdot(tril, part) for part in _split3(g_all))
        bar = bar_ref[0, ci]
        gr_all = -jnp.exp(hpc_ref[:, 0:1]) * _softplus(bar + hpc_ref[:, 1:2])
        gcr_all = sum(_bdot(part, triu) for part in _split3(gr_all))
        for h in range(nh):
            q = act_ref[0, r0:r0 + c, h * hd:(h + 1) * hd]
            k = act_ref[0, r0:r0 + c, (nh + h) * hd:(nh + h + 1) * hd]
            v = act_ref[0, r0:r0 + c, (2 * nh + h) * hd:(2 * nh + h + 1) * hd]
            q = q * lax.rsqrt(jnp.sum(q * q, axis=-1, keepdims=True) + L2_EPS) * (hd ** -0.5)
            k = k * lax.rsqrt(jnp.sum(k * k, axis=-1, keepdims=True) + L2_EPS)
            beta = beta_all[:, h:h + 1]
            gcc = gc_all[:, nh + h:nh + h + 1]
            gcr = gcr_all[nh + h:nh + h + 1, :]
            glast = gc_all[c - 1:c, nh + h:nh + h + 1]
            decay_l.append(jnp.exp(jnp.where(causal, gcc - gcr, NEG_BIG)))
            kbeta = k * beta
            egc = jnp.exp(gcc)
            qg_l.append(q * egc)
            qb_l.append(q.astype(_BF16))
            kb_l.append(k.astype(_BF16))
            kbeta_l.append(kbeta.astype(_BF16))
            rhs_l.append(jnp.concatenate([kbeta * egc, v * beta], axis=1).astype(_BF16))
            kd_l.append((k * jnp.exp(glast - gcc)).astype(_BF16))
            eg_l.append(jnp.exp(glast))

    nt = (((1,), (1,)), ((), ()))
    tn = (((0,), (0,)), ((), ()))
    kk_l = [lax.dot_general(a, b, nt, preferred_element_type=_F32) for a, b in zip(kbeta_l, kb_l)]
    qk_l = [lax.dot_general(a, b, nt, preferred_element_type=_F32) for a, b in zip(qb_l, kb_l)]
    lmat_l = [jnp.where(strict, kk * d, 0.0) for kk, d in zip(kk_l, decay_l)]
    tinv_l = _unit_lower_inverses(lmat_l)
    wu_l = [_bdot(t.astype(_BF16), r).astype(_BF16) for t, r in zip(tinv_l, rhs_l)]
    qkb_l = [(qk * d).astype(_BF16) for qk, d in zip(qk_l, decay_l)]
    pn_l = [lax.dot_general(kd, wu, tn, preferred_element_type=_F32) for kd, wu in zip(kd_l, wu_l)]
    ab_l = [_bdot(qk, wu) for qk, wu in zip(qkb_l, wu_l)]
    a_l = [(qg - ab[:, :hd]).astype(_BF16) for qg, ab in zip(qg_l, ab_l)]
    p_l = [pn[:, :hd].astype(_BF16) for pn in pn_l]

    states = [s_ref[h] for h in range(nh)]
    for ci in range(nchunk):
        r0 = ci * c
        sb = [s.astype(_BF16) for s in states]
        outs = [_bdot(a_l[ci * nh + h], sb[h]) + ab_l[ci * nh + h][:, hd:] for h in range(nh)]
        states = [states[h] * eg_l[ci * nh + h] + pn_l[ci * nh + h][:, hd:]
                  - _bdot(p_l[ci * nh + h], sb[h]) for h in range(nh)]
        for h in range(nh):
            o = outs[h]
            zz = z_ref[0, r0:r0 + c, h * hd:(h + 1) * hd]
            o = (o * lax.rsqrt(jnp.mean(o * o, axis=-1, keepdims=True) + RMS_EPS) * nw_ref[...]
                 * _silu(zz))
            o_ref[0, r0:r0 + c, h * hd:(h + 1) * hd] = o
    for h in range(nh):
        s_ref[h] = states[h]


def _gdn(qkv, z, small, a_log, dt_bias, norm_w):
    bsz, t, width = qkv.shape
    nh, c = GDN_HEADS, GDN_CHUNK
    tb = min(GDN_ROWS, t)
    ba_rows = small[:, :, :8].reshape(bsz, t // c, c, 8).transpose(0, 1, 3, 2)
    hp_row = jnp.zeros((8, LANES), _F32)
    hp_row = hp_row.at[0, nh:2 * nh].set(a_log).at[1, nh:2 * nh].set(dt_bias)
    hp_col = jnp.zeros((8, LANES), _F32)
    hp_col = hp_col.at[nh:2 * nh, 0].set(a_log).at[nh:2 * nh, 1].set(dt_bias)
    return pl.pallas_call(
        functools.partial(_gdn_kernel, tb=tb),
        grid=(bsz, t // tb),
        in_specs=[pl.BlockSpec((1, tb, width), lambda b, i: (b, i, 0)),
                  pl.BlockSpec((1, tb, GDN_WIDTH), lambda b, i: (b, i, 0)),
                  pl.BlockSpec((1, tb, LANES), lambda b, i: (b, i, 0)),
                  pl.BlockSpec((1, tb // c, 8, c), lambda b, i: (b, i, 0, 0)),
                  pl.BlockSpec((8, LANES), lambda b, i: (0, 0)),
                  pl.BlockSpec((8, LANES), lambda b, i: (0, 0)),
                  pl.BlockSpec((1, GDN_HEAD_DIM), lambda b, i: (0, 0))],
        out_specs=pl.BlockSpec((1, tb, GDN_WIDTH), lambda b, i: (b, i, 0)),
        out_shape=jax.ShapeDtypeStruct((bsz, t, GDN_WIDTH), _F32),
        scratch_shapes=[pltpu.VMEM((nh, GDN_HEAD_DIM, GDN_HEAD_DIM), _F32)],
        compiler_params=pltpu.CompilerParams(dimension_semantics=("parallel", "arbitrary"),
                                             vmem_limit_bytes=VMEM_LIMIT),
        name="gdn",
    )(qkv, z, small, ba_rows, hp_row, hp_col, norm_w.reshape(1, -1))


def _gelu_tanh(x):
    return 0.5 * x * (1.0 + jnp.tanh(0.7978845608028654 * (x + 0.044715 * (x * x * x))))


def _cmp_kernel(xk_ref, xv_ref, pk_ref, pv_ref, wk1_ref, wk2_ref, wv1_ref, wv2t_ref, ok_ref, ovt_ref):
    def hidden(x, pos_ref, w1_ref):
        n = x.shape[0]
        ya = _mm(x + pos_ref[0:1, :], w1_ref[0])
        yb = _mm(x + pos_ref[1:2, :], w1_ref[1])
        hid = ya + pltpu.roll(yb, n - 1, 0)
        return _gelu_tanh(hid).astype(_BF16)

    ok_ref[0, 0] = _bdot(hidden(xk_ref[0, 0], pk_ref, wk1_ref), wk2_ref[...]).astype(_BF16)
    ovt_ref[0, 0] = lax.dot_general(wv2t_ref[...], hidden(xv_ref[0, 0], pv_ref, wv1_ref),
                                    (((1,), (1,)), ((), ())),
                                    preferred_element_type=_F32).astype(_BF16)


def _nsa_compress(xk, xv, pos_k, pos_v, k_w1, k_w2, v_w1, v_w2):
    bsz, hk, nrow, half = xk.shape
    dh = half // NSA_CMP_STRIDE
    hidden = k_w1.shape[1]
    x_spec = pl.BlockSpec((1, 1, nrow, half), lambda b, h: (b, h, 0, 0))
    full = lambda shape: pl.BlockSpec(shape, lambda b, h: (0,) * len(shape))
    return pl.pallas_call(
        _cmp_kernel,
        grid=(bsz, hk),
        in_specs=[x_spec, x_spec, full((2, half)), full((2, half)),
                  full((2, half, hidden)), full((hidden, dh)),
                  full((2, half, hidden)), full((dh, hidden))],
        out_specs=[pl.BlockSpec((1, 1, nrow, dh), lambda b, h: (b, h, 0, 0)),
                   pl.BlockSpec((1, 1, dh, nrow), lambda b, h: (b, h, 0, 0))],
        out_shape=[jax.ShapeDtypeStruct((bsz, hk, nrow, dh), _BF16),
                   jax.ShapeDtypeStruct((bsz, hk, dh, nrow), _BF16)],
        compiler_params=pltpu.CompilerParams(dimension_semantics=("parallel", "parallel"),
                                             vmem_limit_bytes=VMEM_LIMIT),
        name="nsa_cmp",
    )(xk, xv, pos_k.reshape(2, half), pos_v.reshape(2, half),
      k_w1.reshape(2, half, hidden).astype(_BF16), k_w2.astype(_BF16),
      v_w1.reshape(2, half, hidden).astype(_BF16), v_w2.T.astype(_BF16))


def _nsa_kernel(q_ref, kc_ref, vct_ref, cst_ref, ks_ref, vst_ref, kw_ref, vwt_ref, gt_ref, o_ref,
                kaug_ref, acc_ref, s_buf, p_buf, bias_ref, *, qt, kt, n_sel):
    assert qt == kt
    grp, dh = NSA_GROUP, NSA_HEAD_DIM
    heads = range(q_ref.shape[1])
    lanes = grp * qt
    t_all = ks_ref.shape[2]
    ns = t_all // NSA_SEL_LEN
    nc = kc_ref.shape[2]
    i = pl.program_id(1)
    s0 = i * qt

    @pl.when(i == 0)
    def _():
        rblk = lax.broadcasted_iota(jnp.int32, (t_all, ns), 0) // NSA_SEL_LEN
        cblk = lax.broadcasted_iota(jnp.int32, (t_all, ns), 1)
        onehot = jnp.where(rblk == cblk, 1.0, 0.0).astype(_BF16)
        for h in heads:
            kaug_ref[h, :, 0:dh] = ks_ref[0, h]
            kaug_ref[h, :, dh:dh + ns] = onehot
        krow = lax.broadcasted_iota(jnp.int32, (kt, lanes), 0)
        tl = lax.broadcasted_iota(jnp.int32, (kt, lanes), 1) & (qt - 1)
        far_off = ((NSA_WINDOW + kt - 1) // kt) * kt - NSA_WINDOW
        bias_ref[0] = jnp.where(krow <= tl, 0.0, NEG_BIG)
        bias_ref[1] = jnp.where(krow > tl + far_off, 0.0, NEG_BIG)
        bias_ref[2] = jnp.full((kt, lanes), NEG_BIG, _F32)

    lane = lax.broadcasted_iota(jnp.int32, (1, lanes), 1)
    tq = s0 + (lane & (qt - 1))
    qs = [q_ref[0, h, 0] for h in heads]

    cend = lax.broadcasted_iota(jnp.int32, (nc, 1), 0) * NSA_CMP_STRIDE + (NSA_CMP_LEN - 1)
    valid = cend <= tq
    sc = [jnp.dot(kc_ref[0, h], qs[h], preferred_element_type=_F32) for h in heads]
    pc = []
    for h in heads:
        scm = jnp.where(valid, sc[h], NEG_BIG)
        mc = jnp.max(scm, axis=0, keepdims=True)
        ec = jnp.where(valid, jnp.exp2(scm - mc), 0.0)
        lc = jnp.sum(ec, axis=0, keepdims=True)
        pc.append(ec / jnp.where(lc > 0.0, lc, 1.0))
    o_cmp = [jnp.dot(vct_ref[0, h], pc[h].astype(_BF16), preferred_element_type=_F32)
             for h in heads]

    cst = cst_ref[...]
    blk = lax.broadcasted_iota(jnp.int32, (ns, qt), 0)
    tq1 = s0 + lax.broadcasted_iota(jnp.int32, (ns, qt), 1)
    cur = tq1 // NSA_SEL_LEN
    visible = blk * NSA_SEL_LEN <= tq1
    forced = (blk == 0) | (blk == cur) | (blk == cur - 1)
    sub = lax.broadcasted_iota(jnp.int32, (8, qt), 0)
    qaug = []
    for h in heads:
        psum = pc[h][:, 0:qt]
        for g in range(1, grp):
            psum = psum + pc[h][:, g * qt:(g + 1) * qt]
        p_hi = psum.astype(_BF16)
        p_lo = (psum - p_hi.astype(_F32)).astype(_BF16)
        imp = (jnp.dot(cst, p_hi, preferred_element_type=_F32)
               + jnp.dot(cst, p_lo, preferred_element_type=_F32))
        imp = jnp.where(visible, jnp.where(forced, NSA_FORCE_SCORE, imp), -1.0)
        imp_rows = [imp[8 * r:8 * r + 8, :] for r in range(ns // 8)]
        ranks = [jnp.zeros((8, qt), _F32) for _ in imp_rows]
        for j in range(ns):
            rowj = imp[j:j + 1, :]
            for r, rows in enumerate(imp_rows):
                if 8 * r > j:
                    beats = jnp.where(rowj >= rows, 1.0, 0.0)
                elif 8 * r + 7 <= j:
                    beats = jnp.where(rowj > rows, 1.0, 0.0)
                else:
                    beats = jnp.where(sub + 8 * r > j, jnp.where(rowj >= rows, 1.0, 0.0),
                                      jnp.where(rowj > rows, 1.0, 0.0))
                ranks[r] = ranks[r] + beats
        rank = jnp.concatenate(ranks, axis=0)
        bias = jnp.where(rank < n_sel, 0.0, NEG_BIG).astype(_BF16)
        qaug.append(jnp.concatenate([qs[h], jnp.concatenate([bias] * grp, axis=1)], axis=0))

    m0 = jnp.full((1, lanes), M_INIT, _F32)
    ones_rows = jnp.where(lax.broadcasted_iota(jnp.int32, (acc_ref.shape[1] - dh, kt), 0) == 0,
                          1.0, 0.0).astype(_BF16)

    def v_aug(v_tile):
        return jnp.concatenate([v_tile, ones_rows], axis=0)

    nwt = (NSA_WINDOW + kt - 1) // kt + 1
    s_win = [[] for _ in heads]
    j_win = []
    for d in range(nwt):
        back = nwt - 1 - d
        j = i - back
        jc = jnp.maximum(j, 0)
        j_win.append(jc)
        for h in heads:
            k_tile = kw_ref[0, h, pl.ds(pl.multiple_of(jc * kt, kt), kt), :]
            s = jnp.dot(k_tile, qs[h], preferred_element_type=_F32)
            if back == 0:
                s = s + bias_ref[0]
            elif back == nwt - 1:
                s = s + bias_ref[jnp.where(j >= 0, 1, 2)]
            else:
                s = s + jnp.where(j >= 0, 0.0, NEG_BIG)
            s_win[h].append(s)
    o_win = []
    for h in heads:
        mw = s_win[h][0].max(axis=0, keepdims=True)
        for s in s_win[h][1:]:
            mw = jnp.maximum(mw, s.max(axis=0, keepdims=True))
        acc_w = jnp.zeros(acc_ref.shape[1:], _F32)
        for s, jc in zip(s_win[h], j_win):
            acc_w = acc_w + jnp.dot(v_aug(vwt_ref[0, h, jc]), jnp.exp2((s - mw).astype(_BF16)),
                                    preferred_element_type=_F32)
        o_win.append(acc_w[0:dh] / acc_w[dh:dh + 1])

    n_s, n_p = s_buf.shape[1], p_buf.shape[1]
    last_tile = t_all // kt - 1

    def qk_tile(h, j):
        start = pl.multiple_of(jnp.minimum(j, last_tile) * kt, kt)
        return jnp.dot(kaug_ref[h, pl.ds(start, kt), :], qaug[h], preferred_element_type=_F32)

    def softmax_step(s, m):
        m_new = jnp.maximum(m, jnp.max(s, axis=0, keepdims=True))
        return m_new, jnp.exp2(m - m_new), jnp.exp2((s - m_new).astype(_BF16))

    acc_ref[...] = jnp.zeros_like(acc_ref)
    for h in heads:
        p_buf[h, n_p - 1] = jnp.zeros((kt, lanes), _BF16)
        s_buf[h, 0] = qk_tile(h, 0)
        s_buf[h, 1] = qk_tile(h, 1)

    def sel_step(j, c, carry):
        for h in heads:
            s_buf[h, (c + 2) % n_s] = qk_tile(h, j + 2)
        pv = [jnp.dot(v_aug(vst_ref[0, h, jnp.maximum(j - 1, 0)]), p_buf[h, (c - 1) % n_p],
                      preferred_element_type=_F32) for h in heads]
        out = []
        for h in heads:
            m, a_prev = carry[h]
            m, a, p = softmax_step(s_buf[h, c], m)
            p_buf[h, c % n_p] = p
            acc_ref[h] = a_prev * acc_ref[h] + pv[h]
            out.append((m, a))
        return tuple(out)

    def sel_round(jj, carry):
        for c in range(n_s):
            carry = sel_step(n_s * jj + c, c, carry)
        return carry

    carry = lax.fori_loop(0, i // n_s, sel_round,
                          tuple((m0, jnp.ones((1, lanes), _F32)) for _ in heads))
    base = (i // n_s) * n_s
    rem = i - base
    carry = lax.cond(rem >= 2, lambda x: sel_step(base + 1, 1, sel_step(base, 0, x)), lambda x: x,
                     carry)
    carry = lax.cond(rem == 1, functools.partial(sel_step, base, 0), lambda x: x, carry)
    carry = lax.cond(rem == 3, functools.partial(sel_step, base + 2, 2), lambda x: x, carry)
    out_rows = []
    for h in heads:
        m, a_prev = carry[h]
        pv = jnp.dot(v_aug(vst_ref[0, h, jnp.maximum(i - 1, 0)]), p_buf[h, (i + n_p - 1) % n_p],
                     preferred_element_type=_F32)
        acc = a_prev * acc_ref[h] + pv
        s = s_buf[h, i % n_s] + bias_ref[0]
        m, a, p = softmax_step(s, m)
        acc = a * acc + jnp.dot(v_aug(vst_ref[0, h, i]), p, preferred_element_type=_F32)
        o_sel = acc[0:dh] / acc[dh:dh + 1]
        gate = _sigmoid(gt_ref[0, h, 0])
        out_t = gate[0:1, :] * o_cmp[h] + gate[1:2, :] * o_sel + gate[2:3, :] * o_win[h]
        out_rows += [out_t[:, g * qt:(g + 1) * qt] for g in range(grp)]
    o_ref[0] = jnp.concatenate(out_rows, axis=0).T


def _nsa_attention(qt_arr, k_cmp, v_cmp_t, cst, ks, vs_t, kw, vw_t, gates_t):
    bsz, hk, nq, dh, lanes = qt_arr.shape
    qt = lanes // NSA_GROUP
    kt = qt
    t = ks.shape[2]
    ns = t // NSA_SEL_LEN
    nc = k_cmp.shape[2]
    ntile = t // kt
    per_seq = lambda shape: pl.BlockSpec((1, hk) + shape, lambda b, i: (b, 0) + (0,) * len(shape))
    per_tile = lambda shape: pl.BlockSpec((1, hk, 1) + shape,
                                          lambda b, i: (b, 0, i) + (0,) * len(shape))
    return pl.pallas_call(
        functools.partial(_nsa_kernel, qt=qt, kt=kt, n_sel=min(NSA_TOPN, ns)),
        grid=(bsz, nq),
        in_specs=[per_tile((dh, lanes)),
                  per_seq((nc, dh)), per_seq((dh, nc)),
                  pl.BlockSpec((ns, nc), lambda b, i: (0, 0)),
                  per_seq((t, dh)), per_seq((ntile, dh, kt)),
                  per_seq((t, dh)), per_seq((ntile, dh, kt)),
                  per_tile((3, lanes))],
        out_specs=pl.BlockSpec((1, qt, hk * NSA_GROUP * dh), lambda b, i: (b, i, 0)),
        out_shape=jax.ShapeDtypeStruct((bsz, t, hk * NSA_GROUP * dh), _F32),
        scratch_shapes=[pltpu.VMEM((hk, t, dh + ns), _BF16),
                        pltpu.VMEM((hk, dh + 16, lanes), _F32),
                        pltpu.VMEM((hk, 4, kt, lanes), _F32),
                        pltpu.VMEM((hk, 2, kt, lanes), _BF16),
                        pltpu.VMEM((3, kt, lanes), _F32)],
        compiler_params=pltpu.CompilerParams(
            dimension_semantics=("parallel", "arbitrary"),
            vmem_limit_bytes=VMEM_LIMIT),
        name="nsa_attn",
    )(qt_arr, k_cmp, v_cmp_t, cst, ks, vs_t, kw, vw_t, gates_t)


def _cmp_to_sel_t(t):
    ncmp = (t - NSA_CMP_LEN) // NSA_CMP_STRIDE + 1
    c_start = np.arange(ncmp) * NSA_CMP_STRIDE
    s_start = np.arange(t // NSA_SEL_LEN) * NSA_SEL_LEN
    overlap = np.clip(np.minimum(c_start[:, None] + NSA_CMP_LEN, s_start[None, :] + NSA_SEL_LEN)
                      - np.maximum(c_start[:, None], s_start[None, :]), 0, None) / NSA_CMP_LEN
    out = np.zeros((t // NSA_SEL_LEN, t // NSA_CMP_STRIDE), np.float32)
    out[:, :ncmp] = overlap.T
    return jnp.asarray(out, _BF16)


def _nsa(q_t, kc, vc, ks, kw, vs_t, vw_t, gates_t, pos_k, pos_v, k_w1, k_w2, v_w1, v_w2):
    k_cmp, v_cmp_t = _nsa_compress(kc, vc, pos_k, pos_v, k_w1, k_w2, v_w1, v_w2)
    return _nsa_attention(q_t, k_cmp, v_cmp_t, _cmp_to_sel_t(ks.shape[2]), ks, vs_t, kw, vw_t, gates_t)


def _out_ffn_kernel(x_ref, a_ref, b_ref, wa_ref, wb_ref, g2_ref, b2_ref, wg_ref, wu_ref, wd_ref,
                    g3_ref, b3_ref, o_ref, h_ref, y_ref):
    @pl.when(pl.program_id(0) == 0)
    def _():
        y_ref[...] = jnp.zeros_like(y_ref)

    ln3 = _layer_norm_pieces(y_ref, g3_ref, b3_ref, o_ref, 8)
    mix = (jnp.dot(a_ref[...].astype(_BF16), wa_ref[...], preferred_element_type=_F32)
           + jnp.dot(b_ref[...].astype(_BF16), wb_ref[...], preferred_element_type=_F32))
    x2 = _layer_norm(DN_ALPHA * x_ref[...] + mix, g2_ref[...], b2_ref[...])
    y_ref[...] = DN_ALPHA * x2 + 0.5 * _swiglu(x2.astype(_BF16), wg_ref, wu_ref, wd_ref, h_ref,
                                               after=ln3)


def _out_ffn(x, oa, ob, wa, wb, g2, b2, wg, wu, wd, g3, b3):
    n, d = x.shape
    d_ff = wg.shape[1]
    tm = min(FFN_ROWS, n)
    nt = n // tm
    row = lambda width: pl.BlockSpec((tm, width), lambda i: (jnp.minimum(i, nt - 1), 0))
    return pl.pallas_call(
        _out_ffn_kernel,
        grid=(nt + 1,),
        in_specs=[row(d), row(oa.shape[1]), row(ob.shape[1]), _const_spec(wa.shape),
                  _const_spec(wb.shape), _const_spec((1, d)), _const_spec((1, d)),
                  _const_spec((d, d_ff)), _const_spec((d, d_ff)), _const_spec((d_ff, d)),
                  _const_spec((1, d)), _const_spec((1, d))],
        out_specs=pl.BlockSpec((tm, d), lambda i: (jnp.maximum(i - 1, 0), 0)),
        out_shape=jax.ShapeDtypeStruct((n, d), _F32),
        scratch_shapes=[pltpu.VMEM((tm, d_ff), _BF16), pltpu.VMEM((tm, d), _F32)],
        compiler_params=pltpu.CompilerParams(dimension_semantics=("arbitrary",),
                                             vmem_limit_bytes=VMEM_LIMIT),
        name="out_ffn",
    )(x, oa, ob, wa, wb, g2, b2, wg.astype(_BF16), wu.astype(_BF16), wd.astype(_BF16), g3, b3)


def kernel(x, ln1_g, ln1_b, ffn1_wg, ffn1_wu, ffn1_wd, w_in, gdn_conv_w, gdn_a_log, gdn_dt_bias,
           gdn_norm_w, nsa_cmp_pos_k, nsa_cmp_pos_v, nsa_cmp_k_w1, nsa_cmp_k_w2, nsa_cmp_v_w1,
           nsa_cmp_v_w2, w_out, ln2_g, ln2_b, ffn2_wg, ffn2_wu, ffn2_wd, ln3_g, ln3_b):
    bsz, t, d = x.shape
    n = bsz * t
    h = x.reshape(n, d)
    for i in range(DEPTH):
        h = _ffn_ln(h, ffn1_wg[i], ffn1_wu[i], ffn1_wd[i], ln1_g[i].reshape(1, d), ln1_b[i].reshape(1, d))
        w_nat, w_t = _proj_weights(w_in[i])
        qkv, z, small, kc, vc, ks, kw, q_t, vs_t, vw_t, gates_t = _in_proj(
            h, w_nat, w_t, gdn_conv_w[i], bsz, t)
        o_gdn = _gdn(qkv.reshape(bsz, t, _QKV), z.reshape(bsz, t, GDN_WIDTH),
                     small.reshape(bsz, t, LANES), gdn_a_log[i], gdn_dt_bias[i], gdn_norm_w[i])
        o_nsa = _nsa(q_t, kc, vc, ks, kw, vs_t, vw_t, gates_t,
                     nsa_cmp_pos_k[i], nsa_cmp_pos_v[i], nsa_cmp_k_w1[i], nsa_cmp_k_w2[i],
                     nsa_cmp_v_w1[i], nsa_cmp_v_w2[i])
        wo = w_out[i].astype(_BF16)
        h = _out_ffn(h, o_gdn.reshape(n, GDN_WIDTH), o_nsa.reshape(n, NSA_WIDTH),
                     wo[:GDN_WIDTH], wo[GDN_WIDTH:], ln2_g[i].reshape(1, d), ln2_b[i].reshape(1, d),
                     ffn2_wg[i], ffn2_wu[i], ffn2_wd[i], ln3_g[i].reshape(1, d), ln3_b[i].reshape(1, d))
    return h.reshape(bsz, t, d)
```

```python
import functools

import numpy as np
import jax
import jax.numpy as jnp
from jax import lax
from jax.experimental import pallas as pl
from jax.experimental.pallas import tpu as pltpu

_F32 = jnp.float32
_BF16 = jnp.bfloat16

GDN_HEADS = 4
GDN_HEAD_DIM = 128
GDN_WIDTH = GDN_HEADS * GDN_HEAD_DIM
GDN_CONV = 4
GDN_CHUNK = 64

NSA_Q_HEADS = 8
NSA_KV_HEADS = 2
NSA_GROUP = NSA_Q_HEADS // NSA_KV_HEADS
NSA_HEAD_DIM = 64
NSA_WIDTH = NSA_Q_HEADS * NSA_HEAD_DIM
NSA_KV_WIDTH = NSA_KV_HEADS * NSA_HEAD_DIM
NSA_CMP_LEN = 32
NSA_CMP_STRIDE = 16
NSA_SEL_LEN = 64
NSA_TOPN = 16
NSA_WINDOW = 512
NSA_FORCE_SCORE = 1e4

LN_EPS = 1e-5
RMS_EPS = 1e-6
L2_EPS = 1e-6
DEPTH = 1
DN_ALPHA = (2 * DEPTH) ** 0.25

NEG_BIG = -(2.0 ** 100)
M_INIT = -3.0e38

LANES = 128
VMEM_LIMIT = 56 * 1024 * 1024

FFN_ROWS = 512
FFN_CHUNK = 256
PROJ_ROWS = 512
GDN_ROWS = 512
NSA_QT = 256


def _sigmoid(x):
    return 0.5 * jnp.tanh(0.5 * x) + 0.5


def _silu(x):
    h = 0.5 * x
    return h * jnp.tanh(h) + h


def _softplus(x):
    return jnp.maximum(x, 0.0) + jnp.log(1.0 + jnp.exp(-jnp.abs(x)))


def _mm(a, b):
    return jnp.dot(a.astype(_BF16), b.astype(_BF16), preferred_element_type=_F32)


def _split3(x):
    hi = x.astype(_BF16)
    r = x - hi.astype(_F32)
    mid = r.astype(_BF16)
    lo = (r - mid.astype(_F32)).astype(_BF16)
    return hi, mid, lo


def _layer_norm(y, g, b):
    mu = jnp.mean(y, axis=-1, keepdims=True)
    yc = y - mu
    var = jnp.mean(yc * yc, axis=-1, keepdims=True)
    return yc * lax.rsqrt(var + LN_EPS) * g + b


def _ffn_ln_kernel(x_ref, wg_ref, wu_ref, wd_ref, g_ref, b_ref, o_ref, h_ref, y_ref):
    @pl.when(pl.program_id(0) == 0)
    def _():
        y_ref[...] = jnp.zeros_like(y_ref)

    ln = _layer_norm_pieces(y_ref, g_ref, b_ref, o_ref, 8)
    x = x_ref[...]
    y_ref[...] = DN_ALPHA * x + 0.5 * _swiglu(x.astype(_BF16), wg_ref, wu_ref, wd_ref, h_ref,
                                              after=ln)


def _layer_norm_pieces(src, g_ref, b_ref, dst_ref, n_pieces):
    rows = src.shape[0] // n_pieces
    pieces = []
    for p in range(n_pieces):
        ln = _layer_norm(src[p * rows:(p + 1) * rows, :], g_ref[...], b_ref[...])
        dst_ref[p * rows:(p + 1) * rows, :] = ln
        pieces.append(ln)
    return pieces


def _zero_after(v, width):
    t = jnp.sum(v, axis=0, keepdims=True)
    folded = t[:, 0:width]
    for c0 in range(width, t.shape[1], width):
        folded = folded + t[:, c0:c0 + width]
    return folded * 0.0


def _swiglu(xb, wg_ref, wu_ref, wd_ref, h_ref, after=None):
    d_ff = wg_ref.shape[1]
    chunks = list(range(0, d_ff, FFN_CHUNK))
    for c0 in chunks:
        gate = jnp.dot(xb, wg_ref[:, c0:c0 + FFN_CHUNK], preferred_element_type=_F32)
        up = jnp.dot(xb, wu_ref[:, c0:c0 + FFN_CHUNK], preferred_element_type=_F32)
        k = c0 // FFN_CHUNK - 1
        if after is not None and 0 <= k < len(after):
            up = up + _zero_after(after[k], FFN_CHUNK)
        h_ref[:, c0:c0 + FFN_CHUNK] = (_silu(gate) * up).astype(_BF16)
    return jnp.dot(h_ref[...], wd_ref[...], preferred_element_type=_F32)


def _const_spec(shape):
    zeros = (0,) * len(shape)
    return pl.BlockSpec(shape, lambda *_: zeros, pipeline_mode=pl.Buffered(1))


def _ffn_ln(x, wg, wu, wd, g, b):
    n, d = x.shape
    d_ff = wg.shape[1]
    tm = min(FFN_ROWS, n)
    nt = n // tm
    return pl.pallas_call(
        _ffn_ln_kernel,
        grid=(nt + 1,),
        in_specs=[pl.BlockSpec((tm, d), lambda i: (jnp.minimum(i, nt - 1), 0)),
                  _const_spec((d, d_ff)), _const_spec((d, d_ff)), _const_spec((d_ff, d)),
                  _const_spec((1, d)), _const_spec((1, d))],
        out_specs=pl.BlockSpec((tm, d), lambda i: (jnp.maximum(i - 1, 0), 0)),
        out_shape=jax.ShapeDtypeStruct((n, d), _F32),
        scratch_shapes=[pltpu.VMEM((tm, d_ff), _BF16), pltpu.VMEM((tm, d), _F32)],
        compiler_params=pltpu.CompilerParams(dimension_semantics=("arbitrary",),
                                             vmem_limit_bytes=VMEM_LIMIT),
        name="ffn_ln",
    )(x, wg.astype(_BF16), wu.astype(_BF16), wd.astype(_BF16), g, b)


_QKV = 3 * GDN_WIDTH
_NAT_KV = _QKV + GDN_WIDTH + LANES
_T_VS = NSA_WIDTH
_T_VW = _T_VS + NSA_KV_WIDTH
_T_GATE = _T_VW + NSA_KV_WIDTH
_T_ROWS = _T_GATE + 32
Q_SCALE = NSA_HEAD_DIM ** -0.5 * 1.4426950408889634


def _proj_kernel(x_ref, xh_ref, wn_ref, wt_ref, cw_ref, qkv_ref, z_ref, small_ref, kc_ref, vc_ref,
                 ks_ref, kw_ref, qt_ref, vst_ref, vwt_ref, gt_ref, kvs_ref, *, qt, tiles_per_seq):
    hk_n, grp, dh = NSA_KV_HEADS, NSA_GROUP, NSA_HEAD_DIM
    xb = x_ref[...].astype(_BF16)
    tm = xb.shape[0]
    halo = xh_ref.shape[0]
    at_start = pl.program_id(0) % tiles_per_seq == 0
    xh = jnp.where(at_start, 0.0, xh_ref[...]).astype(_BF16)
    xcat = jnp.concatenate([xh, xb], axis=0)

    def conv_chunk(c0, width):
        cols = slice(c0, c0 + width)
        pre = jnp.dot(xcat, wn_ref[:, cols], preferred_element_type=_F32)
        conv = cw_ref[GDN_CONV - 1:GDN_CONV, cols] * pre[halo:, :]
        for d in range(1, GDN_CONV):
            conv = conv + cw_ref[GDN_CONV - 1 - d:GDN_CONV - d, cols] * pltpu.roll(pre, d, 0)[halo:, :]
        qkv_ref[:, cols] = _silu(conv)

    def nat_cols(ref, c0, width, r0=0):
        ref[:, r0:r0 + width] = jnp.dot(xb, wn_ref[:, c0:c0 + width], preferred_element_type=_F32)

    def kv_heads():
        kv = jnp.dot(xb, wn_ref[:, _NAT_KV:_NAT_KV + 4 * NSA_KV_WIDTH], preferred_element_type=_F32)
        for hk in range(hk_n):
            for idx, ref in enumerate((ks_ref, kw_ref)):
                c0 = (2 + idx) * NSA_KV_WIDTH + hk * dh
                ref[0, hk] = kv[:, c0:c0 + dh].astype(ref.dtype)
        stride = NSA_CMP_STRIDE
        for idx, ref in enumerate((kc_ref, vc_ref)):
            kvs_ref[idx] = kv[:, idx * NSA_KV_WIDTH:(idx + 1) * NSA_KV_WIDTH]
            for r in range(stride):
                rows = kvs_ref[idx, pl.ds(r, tm // stride, stride=stride), :]
                for hk in range(hk_n):
                    ref[0, hk, :, r * dh:(r + 1) * dh] = rows[:, hk * dh:(hk + 1) * dh]

    def q_heads(hk):
        r0 = hk * grp * dh
        yt = lax.dot_general(wt_ref[r0:r0 + grp * dh, :], xb, (((1,), (1,)), ((), ())),
                             preferred_element_type=_F32)
        for ii in range(tm // qt):
            for g in range(grp):
                qt_ref[0, hk, ii, :, g * qt:(g + 1) * qt] = (
                    yt[g * dh:(g + 1) * dh, ii * qt:(ii + 1) * qt] * Q_SCALE).astype(_BF16)

    def v_gates():
        yt = lax.dot_general(wt_ref[_T_VS:_T_ROWS, :], xb, (((1,), (1,)), ((), ())),
                             preferred_element_type=_F32)
        for hk in range(hk_n):
            for ii in range(tm // qt):
                cols = slice(ii * qt, (ii + 1) * qt)
                vst_ref[0, hk, ii] = yt[hk * dh:(hk + 1) * dh, cols].astype(_BF16)
                vwt_ref[0, hk, ii] = yt[NSA_KV_WIDTH + hk * dh:NSA_KV_WIDTH + (hk + 1) * dh,
                                        cols].astype(_BF16)
                for g in range(grp):
                    for c in range(3):
                        r = 2 * NSA_KV_WIDTH + (hk * grp + g) * 3 + c
                        gt_ref[0, hk, ii, c:c + 1, g * qt:(g + 1) * qt] = yt[r:r + 1, cols]

    step = 2 * LANES
    others = [lambda: nat_cols(z_ref, _QKV, step), lambda: nat_cols(z_ref, _QKV + step, step, step),
              lambda: (nat_cols(small_ref, _QKV + GDN_WIDTH, LANES), kv_heads()),
              lambda: q_heads(0), lambda: q_heads(1), v_gates]
    assert _QKV // step == len(others) and GDN_WIDTH == 2 * step and hk_n == 2
    for k, other in enumerate(others):
        conv_chunk(k * step, step)
        other()


def _proj_weights(w):
    d = w.shape[0]
    sizes = (GDN_WIDTH,) * 4 + (GDN_HEADS,) * 2 + (NSA_WIDTH,) + (NSA_KV_WIDTH,) * 6 + (3 * NSA_Q_HEADS,)
    off = np.concatenate([[0], np.cumsum(sizes)])
    gq, gk, gv, gz, gb, ga, nq, kc, vc, ks, vs, kw, vw, gate = [
        w[:, int(off[i]):int(off[i + 1])] for i in range(len(sizes))]
    small = jnp.concatenate([gb, ga, jnp.zeros((d, LANES - 2 * GDN_HEADS), w.dtype)], axis=1)
    w_nat = jnp.concatenate([gq, gk, gv, gz, small, kc, vc, ks, kw], axis=1)
    w_t = jnp.concatenate([nq, vs, vw, gate, jnp.zeros((d, _T_ROWS - _T_GATE - gate.shape[1]), w.dtype)],
                          axis=1).T
    return w_nat.astype(_BF16), w_t.astype(_BF16)


def _in_proj(x, w_nat, w_t, conv_w, bsz, t):
    n, d = x.shape
    hk, grp, dh = NSA_KV_HEADS, NSA_GROUP, NSA_HEAD_DIM
    tm = min(PROJ_ROWS, t)
    qt = min(NSA_QT, t)
    nt = t // tm
    halo = 16
    halo_spec = pl.BlockSpec((halo, d), lambda r: (jnp.maximum(r * (tm // halo) - 1, 0), 0))
    row = lambda width: pl.BlockSpec((tm, width), lambda r: (r, 0))
    head = pl.BlockSpec((1, hk, tm, dh), lambda r: (r // nt, 0, r % nt, 0))
    cs = NSA_CMP_STRIDE
    cmp_in = pl.BlockSpec((1, hk, tm // cs, cs * dh), lambda r: (r // nt, 0, r % nt, 0))
    tile = lambda rows, lanes: pl.BlockSpec((1, hk, tm // qt, rows, lanes),
                                            lambda r: (r // nt, 0, r % nt, 0, 0))
    sds = jax.ShapeDtypeStruct
    return pl.pallas_call(
        functools.partial(_proj_kernel, qt=qt, tiles_per_seq=nt),
        grid=(n // tm,),
        in_specs=[pl.BlockSpec((tm, d), lambda r: (r, 0)), halo_spec, _const_spec(w_nat.shape),
                  _const_spec(w_t.shape), _const_spec(conv_w.shape)],
        out_specs=[row(_QKV), row(GDN_WIDTH), row(LANES), cmp_in, cmp_in, head, head,
                   tile(dh, grp * qt), tile(dh, qt), tile(dh, qt), tile(3, grp * qt)],
        out_shape=[sds((n, _QKV), _F32), sds((n, GDN_WIDTH), _F32), sds((n, LANES), _F32),
                   sds((bsz, hk, t // cs, cs * dh), _F32), sds((bsz, hk, t // cs, cs * dh), _F32),
                   sds((bsz, hk, t, dh), _BF16), sds((bsz, hk, t, dh), _BF16),
                   sds((bsz, hk, t // qt, dh, grp * qt), _BF16),
                   sds((bsz, hk, t // qt, dh, qt), _BF16), sds((bsz, hk, t // qt, dh, qt), _BF16),
                   sds((bsz, hk, t // qt, 3, grp * qt), _F32)],
        scratch_shapes=[pltpu.VMEM((2, tm, NSA_KV_WIDTH), _F32)],
        compiler_params=pltpu.CompilerParams(dimension_semantics=("parallel",),
                                             vmem_limit_bytes=VMEM_LIMIT),
        name="in_proj",
    )(x, x, w_nat, w_t, conv_w)


def _bdot(a, b):
    return jnp.dot(a, b, preferred_element_type=_F32)


def _unit_lower_inverses(lmats):
    c = lmats[0].shape[0]
    eye = jnp.where(lax.broadcasted_iota(jnp.int32, (c, c), 0)
                    == lax.broadcasted_iota(jnp.int32, (c, c), 1), 1.0, 0.0)
    rs = [eye - l for l in lmats]
    pb = [l.astype(_BF16) for l in lmats]
    ps = [_bdot(l, l) for l in pb]
    span = 2
    while True:
        pb = [p.astype(_BF16) for p in ps]
        rs = [r + _bdot(r.astype(_BF16), p) for r, p in zip(rs, pb)]
        span *= 2
        if span >= c:
            return rs
        ps = [_bdot(p, p) for p in pb]


def _gdn_kernel(act_ref, z_ref, bac_ref, bar_ref, hpr_ref, hpc_ref, nw_ref, o_ref, s_ref, *, tb):
    nh, hd, c = GDN_HEADS, GDN_HEAD_DIM, GDN_CHUNK

    @pl.when(pl.program_id(1) == 0)
    def _():
        s_ref[...] = jnp.zeros_like(s_ref)

    row = lax.broadcasted_iota(jnp.int32, (c, c), 0)
    col = lax.broadcasted_iota(jnp.int32, (c, c), 1)
    causal = row >= col
    strict = row > col
    tril = jnp.where(causal, 1.0, 0.0).astype(_BF16)
    triu = jnp.where(row <= col, 1.0, 0.0).astype(_BF16)

    nchunk = tb // c
    qg_l, qb_l, kb_l, kbeta_l, rhs_l, kd_l, decay_l, eg_l = [], [], [], [], [], [], [], []
    for ci in range(nchunk):
        r0 = ci * c
        bac = bac_ref[0, r0:r0 + c, :]
        beta_all = _sigmoid(bac)
        g_all = -jnp.exp(hpr_ref[0:1, :]) * _softplus(bac + hpr_ref[1:2, :])
        gc_all = sum(_bdot(tril, part) for part in _split3(g_all))
        bar = bar_ref[0, ci]
        gr_all = -jnp.exp(hpc_ref[:, 0:1]) * _softplus(bar + hpc_ref[:, 1:2])
        gcr_all = sum(_bdot(part, triu) for part in _split3(gr_all))
        for h in range(nh):
            q = act_ref[0, r0:r0 + c, h * hd:(h + 1) * hd]
            k = act_ref[0, r0:r0 + c, (nh + h) * hd:(nh + h + 1) * hd]
            v = act_ref[0, r0:r0 + c, (2 * nh + h) * hd:(2 * nh + h + 1) * hd]
            q = q * lax.rsqrt(jnp.sum(q * q, axis=-1, keepdims=True) + L2_EPS) * (hd ** -0.5)
            k = k * lax.rsqrt(jnp.sum(k * k, axis=-1, keepdims=True) + L2_EPS)
            beta = beta_all[:, h:h + 1]
            gcc = gc_all[:, nh + h:nh + h + 1]
            gcr = gcr_all[nh + h:nh + h + 1, :]
            glast = gc_all[c - 1:c, nh + h:nh + h + 1]
            decay_l.append(jnp.exp(jnp.where(causal, gcc - gcr, NEG_BIG)))
            kbeta = k * beta
            egc = jnp.exp(gcc)
            qg_l.append(q * egc)
            qb_l.append(q.astype(_BF16))
            kb_l.append(k.astype(_BF16))
            kbeta_l.append(kbeta.astype(_BF16))
            rhs_l.append(jnp.concatenate([kbeta * egc, v * beta], axis=1).astype(_BF16))
            kd_l.append((k * jnp.exp(glast - gcc)).astype(_BF16))
            eg_l.append(jnp.exp(glast))

    nt = (((1,), (1,)), ((), ()))
    tn = (((0,), (0,)), ((), ()))
    kk_l = [lax.dot_general(a, b, nt, preferred_element_type=_F32) for a, b in zip(kbeta_l, kb_l)]
    qk_l = [lax.dot_general(a, b, nt, preferred_element_type=_F32) for a, b in zip(qb_l, kb_l)]
    lmat_l = [jnp.where(strict, kk * d, 0.0) for kk, d in zip(kk_l, decay_l)]
    tinv_l = _unit_lower_inverses(lmat_l)
    wu_l = [_bdot(t.astype(_BF16), r).astype(_BF16) for t, r in zip(tinv_l, rhs_l)]
    qkb_l = [(qk * d).astype(_BF16) for qk, d in zip(qk_l, decay_l)]
    pn_l = [lax.dot_general(kd, wu, tn, preferred_element_type=_F32) for kd, wu in zip(kd_l, wu_l)]
    ab_l = [_bdot(qk, wu) for qk, wu in zip(qkb_l, wu_l)]
    a_l = [(qg - ab[:, :hd]).astype(_BF16) for qg, ab in zip(qg_l, ab_l)]
    p_l = [pn[:, :hd].astype(_BF16) for pn in pn_l]

    states = [s_ref[h] for h in range(nh)]
    for ci in range(nchunk):
        r0 = ci * c
        sb = [s.astype(_BF16) for s in states]
        outs = [_bdot(a_l[ci * nh + h], sb[h]) + ab_l[ci * nh + h][:, hd:] for h in range(nh)]
        states = [states[h] * eg_l[ci * nh + h] + pn_l[ci * nh + h][:, hd:]
                  - _bdot(p_l[ci * nh + h], sb[h]) for h in range(nh)]
        for h in range(nh):
            o = outs[h]
            zz = z_ref[0, r0:r0 + c, h * hd:(h + 1) * hd]
            o = (o * lax.rsqrt(jnp.mean(o * o, axis=-1, keepdims=True) + RMS_EPS) * nw_ref[...]
                 * _silu(zz))
            o_ref[0, r0:r0 + c, h * hd:(h + 1) * hd] = o
    for h in range(nh):
        s_ref[h] = states[h]


def _gdn(qkv, z, small, a_log, dt_bias, norm_w):
    bsz, t, width = qkv.shape
    nh, c = GDN_HEADS, GDN_CHUNK
    tb = min(GDN_ROWS, t)
    ba_rows = small[:, :, :8].reshape(bsz, t // c, c, 8).transpose(0, 1, 3, 2)
    hp_row = jnp.zeros((8, LANES), _F32)
    hp_row = hp_row.at[0, nh:2 * nh].set(a_log).at[1, nh:2 * nh].set(dt_bias)
    hp_col = jnp.zeros((8, LANES), _F32)
    hp_col = hp_col.at[nh:2 * nh, 0].set(a_log).at[nh:2 * nh, 1].set(dt_bias)
    return pl.pallas_call(
        functools.partial(_gdn_kernel, tb=tb),
        grid=(bsz, t // tb),
        in_specs=[pl.BlockSpec((1, tb, width), lambda b, i: (b, i, 0)),
                  pl.BlockSpec((1, tb, GDN_WIDTH), lambda b, i: (b, i, 0)),
                  pl.BlockSpec((1, tb, LANES), lambda b, i: (b, i, 0)),
                  pl.BlockSpec((1, tb // c, 8, c), lambda b, i: (b, i, 0, 0)),
                  pl.BlockSpec((8, LANES), lambda b, i: (0, 0)),
                  pl.BlockSpec((8, LANES), lambda b, i: (0, 0)),
                  pl.BlockSpec((1, GDN_HEAD_DIM), lambda b, i: (0, 0))],
        out_specs=pl.BlockSpec((1, tb, GDN_WIDTH), lambda b, i: (b, i, 0)),
        out_shape=jax.ShapeDtypeStruct((bsz, t, GDN_WIDTH), _F32),
        scratch_shapes=[pltpu.VMEM((nh, GDN_HEAD_DIM, GDN_HEAD_DIM), _F32)],
        compiler_params=pltpu.CompilerParams(dimension_semantics=("parallel", "arbitrary"),
                                             vmem_limit_bytes=VMEM_LIMIT),
        name="gdn",
    )(qkv, z, small, ba_rows, hp_row, hp_col, norm_w.reshape(1, -1))


def _gelu_tanh(x):
    return 0.5 * x * (1.0 + jnp.tanh(0.7978845608028654 * (x + 0.044715 * (x * x * x))))


def _cmp_kernel(xk_ref, xv_ref, pk_ref, pv_ref, wk1_ref, wk2_ref, wv1_ref, wv2t_ref, ok_ref, ovt_ref):
    def hidden(x, pos_ref, w1_ref):
        n = x.shape[0]
        ya = _mm(x + pos_ref[0:1, :], w1_ref[0])
        yb = _mm(x + pos_ref[1:2, :], w1_ref[1])
        hid = ya + pltpu.roll(yb, n - 1, 0)
        return _gelu_tanh(hid).astype(_BF16)

    ok_ref[0, 0] = _bdot(hidden(xk_ref[0, 0], pk_ref, wk1_ref), wk2_ref[...]).astype(_BF16)
    ovt_ref[0, 0] = lax.dot_general(wv2t_ref[...], hidden(xv_ref[0, 0], pv_ref, wv1_ref),
                                    (((1,), (1,)), ((), ())),
                                    preferred_element_type=_F32).astype(_BF16)


def _nsa_compress(xk, xv, pos_k, pos_v, k_w1, k_w2, v_w1, v_w2):
    bsz, hk, nrow, half = xk.shape
    dh = half // NSA_CMP_STRIDE
    hidden = k_w1.shape[1]
    x_spec = pl.BlockSpec((1, 1, nrow, half), lambda b, h: (b, h, 0, 0))
    full = lambda shape: pl.BlockSpec(shape, lambda b, h: (0,) * len(shape))
    return pl.pallas_call(
        _cmp_kernel,
        grid=(bsz, hk),
        in_specs=[x_spec, x_spec, full((2, half)), full((2, half)),
                  full((2, half, hidden)), full((hidden, dh)),
                  full((2, half, hidden)), full((dh, hidden))],
        out_specs=[pl.BlockSpec((1, 1, nrow, dh), lambda b, h: (b, h, 0, 0)),
                   pl.BlockSpec((1, 1, dh, nrow), lambda b, h: (b, h, 0, 0))],
        out_shape=[jax.ShapeDtypeStruct((bsz, hk, nrow, dh), _BF16),
                   jax.ShapeDtypeStruct((bsz, hk, dh, nrow), _BF16)],
        compiler_params=pltpu.CompilerParams(dimension_semantics=("parallel", "parallel"),
                                             vmem_limit_bytes=VMEM_LIMIT),
        name="nsa_cmp",
    )(xk, xv, pos_k.reshape(2, half), pos_v.reshape(2, half),
      k_w1.reshape(2, half, hidden).astype(_BF16), k_w2.astype(_BF16),
      v_w1.reshape(2, half, hidden).astype(_BF16), v_w2.T.astype(_BF16))


def _nsa_kernel(q_ref, kc_ref, vct_ref, cst_ref, ks_ref, vst_ref, kw_ref, vwt_ref, gt_ref, o_ref,
                kaug_ref, acc_ref, s_buf, p_buf, bias_ref, *, qt, kt, n_sel):
    assert qt == kt
    grp, dh = NSA_GROUP, NSA_HEAD_DIM
    heads = range(q_ref.shape[1])
    lanes = grp * qt
    t_all = ks_ref.shape[2]
    ns = t_all // NSA_SEL_LEN
    nc = kc_ref.shape[2]
    i = pl.program_id(1)
    s0 = i * qt

    @pl.when(i == 0)
    def _():
        rblk = lax.broadcasted_iota(jnp.int32, (t_all, ns), 0) // NSA_SEL_LEN
        cblk = lax.broadcasted_iota(jnp.int32, (t_all, ns), 1)
        onehot = jnp.where(rblk == cblk, 1.0, 0.0).astype(_BF16)
        for h in heads:
            kaug_ref[h, :, 0:dh] = ks_ref[0, h]
            kaug_ref[h, :, dh:dh + ns] = onehot
        krow = lax.broadcasted_iota(jnp.int32, (kt, lanes), 0)
        tl = lax.broadcasted_iota(jnp.int32, (kt, lanes), 1) & (qt - 1)
        far_off = ((NSA_WINDOW + kt - 1) // kt) * kt - NSA_WINDOW
        bias_ref[0] = jnp.where(krow <= tl, 0.0, NEG_BIG)
        bias_ref[1] = jnp.where(krow > tl + far_off, 0.0, NEG_BIG)
        bias_ref[2] = jnp.full((kt, lanes), NEG_BIG, _F32)

    lane = lax.broadcasted_iota(jnp.int32, (1, lanes), 1)
    tq = s0 + (lane & (qt - 1))
    qs = [q_ref[0, h, 0] for h in heads]

    cend = lax.broadcasted_iota(jnp.int32, (nc, 1), 0) * NSA_CMP_STRIDE + (NSA_CMP_LEN - 1)
    valid = cend <= tq
    sc = [jnp.dot(kc_ref[0, h], qs[h], preferred_element_type=_F32) for h in heads]
    pc = []
    for h in heads:
        scm = jnp.where(valid, sc[h], NEG_BIG)
        mc = jnp.max(scm, axis=0, keepdims=True)
        ec = jnp.where(valid, jnp.exp2(scm - mc), 0.0)
        lc = jnp.sum(ec, axis=0, keepdims=True)
        pc.append(ec / jnp.where(lc > 0.0, lc, 1.0))
    o_cmp = [jnp.dot(vct_ref[0, h], pc[h].astype(_BF16), preferred_element_type=_F32)
             for h in heads]

    cst = cst_ref[...]
    blk = lax.broadcasted_iota(jnp.int32, (ns, qt), 0)
    tq1 = s0 + lax.broadcasted_iota(jnp.int32, (ns, qt), 1)
    cur = tq1 // NSA_SEL_LEN
    visible = blk * NSA_SEL_LEN <= tq1
    forced = (blk == 0) | (blk == cur) | (blk == cur - 1)
    sub = lax.broadcasted_iota(jnp.int32, (8, qt), 0)
    qaug = []
    for h in heads:
        psum = pc[h][:, 0:qt]
        for g in range(1, grp):
            psum = psum + pc[h][:, g * qt:(g + 1) * qt]
        p_hi = psum.astype(_BF16)
        p_lo = (psum - p_hi.astype(_F32)).astype(_BF16)
        imp = (jnp.dot(cst, p_hi, preferred_element_type=_F32)
               + jnp.dot(cst, p_lo, preferred_element_type=_F32))
        imp = jnp.where(visible, jnp.where(forced, NSA_FORCE_SCORE, imp), -1.0)
        imp_rows = [imp[8 * r:8 * r + 8, :] for r in range(ns // 8)]
        ranks = [jnp.zeros((8, qt), _F32) for _ in imp_rows]
        for j in range(ns):
            rowj = imp[j:j + 1, :]
            for r, rows in enumerate(imp_rows):
                if 8 * r > j:
                    beats = jnp.where(rowj >= rows, 1.0, 0.0)
                elif 8 * r + 7 <= j:
                    beats = jnp.where(rowj > rows, 1.0, 0.0)
                else:
                    beats = jnp.where(sub + 8 * r > j, jnp.where(rowj >= rows, 1.0, 0.0),
                                      jnp.where(rowj > rows, 1.0, 0.0))
                ranks[r] = ranks[r] + beats
        rank = jnp.concatenate(ranks, axis=0)
        bias = jnp.where(rank < n_sel, 0.0, NEG_BIG).astype(_BF16)
        qaug.append(jnp.concatenate([qs[h], jnp.concatenate([bias] * grp, axis=1)], axis=0))

    m0 = jnp.full((1, lanes), M_INIT, _F32)
    ones_rows = jnp.where(lax.broadcasted_iota(jnp.int32, (acc_ref.shape[1] - dh, kt), 0) == 0,
                          1.0, 0.0).astype(_BF16)

    def v_aug(v_tile):
        return jnp.concatenate([v_tile, ones_rows], axis=0)

    nwt = (NSA_WINDOW + kt - 1) // kt + 1
    s_win = [[] for _ in heads]
    j_win = []
    for d in range(nwt):
        back = nwt - 1 - d
        j = i - back
        jc = jnp.maximum(j, 0)
        j_win.append(jc)
        for h in heads:
            k_tile = kw_ref[0, h, pl.ds(pl.multiple_of(jc * kt, kt), kt), :]
            s = jnp.dot(k_tile, qs[h], preferred_element_type=_F32)
            if back == 0:
                s = s + bias_ref[0]
            elif back == nwt - 1:
                s = s + bias_ref[jnp.where(j >= 0, 1, 2)]
            else:
                s = s + jnp.where(j >= 0, 0.0, NEG_BIG)
            s_win[h].append(s)
    o_win = []
    for h in heads:
        mw = s_win[h][0].max(axis=0, keepdims=True)
        for s in s_win[h][1:]:
            mw = jnp.maximum(mw, s.max(axis=0, keepdims=True))
        acc_w = jnp.zeros(acc_ref.shape[1:], _F32)
        for s, jc in zip(s_win[h], j_win):
            acc_w = acc_w + jnp.dot(v_aug(vwt_ref[0, h, jc]), jnp.exp2((s - mw).astype(_BF16)),
                                    preferred_element_type=_F32)
        o_win.append(acc_w[0:dh] / acc_w[dh:dh + 1])

    n_s, n_p = s_buf.shape[1], p_buf.shape[1]
    last_tile = t_all // kt - 1

    def qk_tile(h, j):
        start = pl.multiple_of(jnp.minimum(j, last_tile) * kt, kt)
        return jnp.dot(kaug_ref[h, pl.ds(start, kt), :], qaug[h], preferred_element_type=_F32)

    def softmax_step(s, m):
        m_new = jnp.maximum(m, jnp.max(s, axis=0, keepdims=True))
        return m_new, jnp.exp2(m - m_new), jnp.exp2((s - m_new).astype(_BF16))

    acc_ref[...] = jnp.zeros_like(acc_ref)
    for h in heads:
        p_buf[h, n_p - 1] = jnp.zeros((kt, lanes), _BF16)
        s_buf[h, 0] = qk_tile(h, 0)
        s_buf[h, 1] = qk_tile(h, 1)

    def sel_step(j, c, carry):
        for h in heads:
            s_buf[h, (c + 2) % n_s] = qk_tile(h, j + 2)
        pv = [jnp.dot(v_aug(vst_ref[0, h, jnp.maximum(j - 1, 0)]), p_buf[h, (c - 1) % n_p],
                      preferred_element_type=_F32) for h in heads]
        out = []
        for h in heads:
            m, a_prev = carry[h]
            m, a, p = softmax_step(s_buf[h, c], m)
            p_buf[h, c % n_p] = p
            acc_ref[h] = a_prev * acc_ref[h] + pv[h]
            out.append((m, a))
        return tuple(out)

    def sel_round(jj, carry):
        for c in range(n_s):
            carry = sel_step(n_s * jj + c, c, carry)
        return carry

    carry = lax.fori_loop(0, i // n_s, sel_round,
                          tuple((m0, jnp.ones((1, lanes), _F32)) for _ in heads))
    base = (i // n_s) * n_s
    rem = i - base
    carry = lax.cond(rem >= 2, lambda x: sel_step(base + 1, 1, sel_step(base, 0, x)), lambda x: x,
                     carry)
    carry = lax.cond(rem == 1, functools.partial(sel_step, base, 0), lambda x: x, carry)
    carry = lax.cond(rem == 3, functools.partial(sel_step, base + 2, 2), lambda x: x, carry)
    out_rows = []
    for h in heads:
        m, a_prev = carry[h]
        pv = jnp.dot(v_aug(vst_ref[0, h, jnp.maximum(i - 1, 0)]), p_buf[h, (i + n_p - 1) % n_p],
                     preferred_element_type=_F32)
        acc = a_prev * acc_ref[h] + pv
        s = s_buf[h, i % n_s] + bias_ref[0]
        m, a, p = softmax_step(s, m)
        acc = a * acc + jnp.dot(v_aug(vst_ref[0, h, i]), p, preferred_element_type=_F32)
        o_sel = acc[0:dh] / acc[dh:dh + 1]
        gate = _sigmoid(gt_ref[0, h, 0])
        out_t = gate[0:1, :] * o_cmp[h] + gate[1:2, :] * o_sel + gate[2:3, :] * o_win[h]
        out_rows += [out_t[:, g * qt:(g + 1) * qt] for g in range(grp)]
    o_ref[0] = jnp.concatenate(out_rows, axis=0).T


def _nsa_attention(qt_arr, k_cmp, v_cmp_t, cst, ks, vs_t, kw, vw_t, gates_t):
    bsz, hk, nq, dh, lanes = qt_arr.shape
    qt = lanes // NSA_GROUP
    kt = qt
    t = ks.shape[2]
    ns = t // NSA_SEL_LEN
    nc = k_cmp.shape[2]
    ntile = t // kt
    per_seq = lambda shape: pl.BlockSpec((1, hk) + shape, lambda b, i: (b, 0) + (0,) * len(shape))
    per_tile = lambda shape: pl.BlockSpec((1, hk, 1) + shape,
                                          lambda b, i: (b, 0, i) + (0,) * len(shape))
    return pl.pallas_call(
        functools.partial(_nsa_kernel, qt=qt, kt=kt, n_sel=min(NSA_TOPN, ns)),
        grid=(bsz, nq),
        in_specs=[per_tile((dh, lanes)),
                  per_seq((nc, dh)), per_seq((dh, nc)),
                  pl.BlockSpec((ns, nc), lambda b, i: (0, 0)),
                  per_seq((t, dh)), per_seq((ntile, dh, kt)),
                  per_seq((t, dh)), per_seq((ntile, dh, kt)),
                  per_tile((3, lanes))],
        out_specs=pl.BlockSpec((1, qt, hk * NSA_GROUP * dh), lambda b, i: (b, i, 0)),
        out_shape=jax.ShapeDtypeStruct((bsz, t, hk * NSA_GROUP * dh), _F32),
        scratch_shapes=[pltpu.VMEM((hk, t, dh + ns), _BF16),
                        pltpu.VMEM((hk, dh + 16, lanes), _F32),
                        pltpu.VMEM((hk, 4, kt, lanes), _F32),
                        pltpu.VMEM((hk, 2, kt, lanes), _BF16),
                        pltpu.VMEM((3, kt, lanes), _F32)],
        compiler_params=pltpu.CompilerParams(
            dimension_semantics=("parallel", "arbitrary"),
            vmem_limit_bytes=VMEM_LIMIT),
        name="nsa_attn",
    )(qt_arr, k_cmp, v_cmp_t, cst, ks, vs_t, kw, vw_t, gates_t)


def _cmp_to_sel_t(t):
    ncmp = (t - NSA_CMP_LEN) // NSA_CMP_STRIDE + 1
    c_start = np.arange(ncmp) * NSA_CMP_STRIDE
    s_start = np.arange(t // NSA_SEL_LEN) * NSA_SEL_LEN
    overlap = np.clip(np.minimum(c_start[:, None] + NSA_CMP_LEN, s_start[None, :] + NSA_SEL_LEN)
                      - np.maximum(c_start[:, None], s_start[None, :]), 0, None) / NSA_CMP_LEN
    out = np.zeros((t // NSA_SEL_LEN, t // NSA_CMP_STRIDE), np.float32)
    out[:, :ncmp] = overlap.T
    return jnp.asarray(out, _BF16)


def _nsa(q_t, kc, vc, ks, kw, vs_t, vw_t, gates_t, pos_k, pos_v, k_w1, k_w2, v_w1, v_w2):
    k_cmp, v_cmp_t = _nsa_compress(kc, vc, pos_k, pos_v, k_w1, k_w2, v_w1, v_w2)
    return _nsa_attention(q_t, k_cmp, v_cmp_t, _cmp_to_sel_t(ks.shape[2]), ks, vs_t, kw, vw_t, gates_t)


def _out_ffn_kernel(x_ref, a_ref, b_ref, wa_ref, wb_ref, g2_ref, b2_ref, wg_ref, wu_ref, wd_ref,
                    g3_ref, b3_ref, o_ref, h_ref, y_ref):
    @pl.when(pl.program_id(0) == 0)
    def _():
        y_ref[...] = jnp.zeros_like(y_ref)

    ln3 = _layer_norm_pieces(y_ref, g3_ref, b3_ref, o_ref, 8)
    mix = (jnp.dot(a_ref[...].astype(_BF16), wa_ref[...], preferred_element_type=_F32)
           + jnp.dot(b_ref[...].astype(_BF16), wb_ref[...], preferred_element_type=_F32))
    x2 = _layer_norm(DN_ALPHA * x_ref[...] + mix, g2_ref[...], b2_ref[...])
    y_ref[...] = DN_ALPHA * x2 + 0.5 * _swiglu(x2.astype(_BF16), wg_ref, wu_ref, wd_ref, h_ref,
                                               after=ln3)


def _out_ffn(x, oa, ob, wa, wb, g2, b2, wg, wu, wd, g3, b3):
    n, d = x.shape
    d_ff = wg.shape[1]
    tm = min(FFN_ROWS, n)
    nt = n // tm
    row = lambda width: pl.BlockSpec((tm, width), lambda i: (jnp.minimum(i, nt - 1), 0))
    return pl.pallas_call(
        _out_ffn_kernel,
        grid=(nt + 1,),
        in_specs=[row(d), row(oa.shape[1]), row(ob.shape[1]), _const_spec(wa.shape),
                  _const_spec(wb.shape), _const_spec((1, d)), _const_spec((1, d)),
                  _const_spec((d, d_ff)), _const_spec((d, d_ff)), _const_spec((d_ff, d)),
                  _const_spec((1, d)), _const_spec((1, d))],
        out_specs=pl.BlockSpec((tm, d), lambda i: (jnp.maximum(i - 1, 0), 0)),
        out_shape=jax.ShapeDtypeStruct((n, d), _F32),
        scratch_shapes=[pltpu.VMEM((tm, d_ff), _BF16), pltpu.VMEM((tm, d), _F32)],
        compiler_params=pltpu.CompilerParams(dimension_semantics=("arbitrary",),
                                             vmem_limit_bytes=VMEM_LIMIT),
        name="out_ffn",
    )(x, oa, ob, wa, wb, g2, b2, wg.astype(_BF16), wu.astype(_BF16), wd.astype(_BF16), g3, b3)


def kernel(x, ln1_g, ln1_b, ffn1_wg, ffn1_wu, ffn1_wd, w_in, gdn_conv_w, gdn_a_log, gdn_dt_bias,
           gdn_norm_w, nsa_cmp_pos_k, nsa_cmp_pos_v, nsa_cmp_k_w1, nsa_cmp_k_w2, nsa_cmp_v_w1,
           nsa_cmp_v_w2, w_out, ln2_g, ln2_b, ffn2_wg, ffn2_wu, ffn2_wd, ln3_g, ln3_b):
    bsz, t, d = x.shape
    n = bsz * t
    h = x.reshape(n, d)
    for i in range(DEPTH):
        h = _ffn_ln(h, ffn1_wg[i], ffn1_wu[i], ffn1_wd[i], ln1_g[i].reshape(1, d), ln1_b[i].reshape(1, d))
        w_nat, w_t = _proj_weights(w_in[i])
        qkv, z, small, kc, vc, ks, kw, q_t, vs_t, vw_t, gates_t = _in_proj(
            h, w_nat, w_t, gdn_conv_w[i], bsz, t)
        o_gdn = _gdn(qkv.reshape(bsz, t, _QKV), z.reshape(bsz, t, GDN_WIDTH),
                     small.reshape(bsz, t, LANES), gdn_a_log[i], gdn_dt_bias[i], gdn_norm_w[i])
        o_nsa = _nsa(q_t, kc, vc, ks, kw, vs_t, vw_t, gates_t,
                     nsa_cmp_pos_k[i], nsa_cmp_pos_v[i], nsa_cmp_k_w1[i], nsa_cmp_k_w2[i],
                     nsa_cmp_v_w1[i], nsa_cmp_v_w2[i])
        wo = w_out[i].astype(_BF16)
        h = _out_ffn(h, o_gdn.reshape(n, GDN_WIDTH), o_nsa.reshape(n, NSA_WIDTH),
                     wo[:GDN_WIDTH], wo[GDN_WIDTH:], ln2_g[i].reshape(1, d), ln2_b[i].reshape(1, d),
                     ffn2_wg[i], ffn2_wu[i], ffn2_wd[i], ln3_g[i].reshape(1, d), ln3_b[i].reshape(1, d))
    return h.reshape(bsz, t, d)
```

```python
import functools

import numpy as np
import jax
import jax.numpy as jnp
from jax import lax
from jax.experimental import pallas as pl
from jax.experimental.pallas import tpu as pltpu

_F32 = jnp.float32
_BF16 = jnp.bfloat16

GDN_HEADS = 4
GDN_HEAD_DIM = 128
GDN_WIDTH = GDN_HEADS * GDN_HEAD_DIM
GDN_CONV = 4
GDN_CHUNK = 64

NSA_Q_HEADS = 8
NSA_KV_HEADS = 2
NSA_GROUP = NSA_Q_HEADS // NSA_KV_HEADS
NSA_HEAD_DIM = 64
NSA_WIDTH = NSA_Q_HEADS * NSA_HEAD_DIM
NSA_KV_WIDTH = NSA_KV_HEADS * NSA_HEAD_DIM
NSA_CMP_LEN = 32
NSA_CMP_STRIDE = 16
NSA_SEL_LEN = 64
NSA_TOPN = 16
NSA_WINDOW = 512
NSA_FORCE_SCORE = 1e4

LN_EPS = 1e-5
RMS_EPS = 1e-6
L2_EPS = 1e-6
DEPTH = 1
DN_ALPHA = (2 * DEPTH) ** 0.25

NEG_BIG = -(2.0 ** 100)
M_INIT = -3.0e38

LANES = 128
VMEM_LIMIT = 56 * 1024 * 1024

FFN_ROWS = 512
FFN_CHUNK = 256
PROJ_ROWS = 512
GDN_ROWS = 256
GDN_SEQS = 2
NSA_QT = 256


def _sigmoid(x):
    return 0.5 * jnp.tanh(0.5 * x) + 0.5


def _silu(x):
    h = 0.5 * x
    return h * jnp.tanh(h) + h


def _softplus(x):
    return jnp.maximum(x, 0.0) + jnp.log(1.0 + jnp.exp(-jnp.abs(x)))


def _mm(a, b):
    return jnp.dot(a.astype(_BF16), b.astype(_BF16), preferred_element_type=_F32)


def _split3(x):
    hi = x.astype(_BF16)
    r = x - hi.astype(_F32)
    mid = r.astype(_BF16)
    lo = (r - mid.astype(_F32)).astype(_BF16)
    return hi, mid, lo


def _layer_norm(y, g, b):
    mu = jnp.mean(y, axis=-1, keepdims=True)
    yc = y - mu
    var = jnp.mean(yc * yc, axis=-1, keepdims=True)
    return yc * lax.rsqrt(var + LN_EPS) * g + b


def _ffn_ln_kernel(x_ref, wg_ref, wu_ref, wd_ref, g_ref, b_ref, o_ref, h_ref, y_ref):
    @pl.when(pl.program_id(0) == 0)
    def _():
        y_ref[...] = jnp.zeros_like(y_ref)

    ln = _layer_norm_pieces(y_ref, g_ref, b_ref, o_ref, 8)
    x = x_ref[...]
    y_ref[...] = DN_ALPHA * x + 0.5 * _swiglu(x.astype(_BF16), wg_ref, wu_ref, wd_ref, h_ref,
                                              after=ln)


def _layer_norm_pieces(src, g_ref, b_ref, dst_ref, n_pieces):
    rows = src.shape[0] // n_pieces
    pieces = []
    for p in range(n_pieces):
        ln = _layer_norm(src[p * rows:(p + 1) * rows, :], g_ref[...], b_ref[...])
        dst_ref[p * rows:(p + 1) * rows, :] = ln
        pieces.append(ln)
    return pieces


def _zero_after(v, width):
    t = jnp.sum(v, axis=0, keepdims=True)
    folded = t[:, 0:width]
    for c0 in range(width, t.shape[1], width):
        folded = folded + t[:, c0:c0 + width]
    return folded * 0.0


def _swiglu(xb, wg_ref, wu_ref, wd_ref, h_ref, after=None):
    d_ff = wg_ref.shape[1]
    chunks = list(range(0, d_ff, FFN_CHUNK))
    for c0 in chunks:
        gate = jnp.dot(xb, wg_ref[:, c0:c0 + FFN_CHUNK], preferred_element_type=_F32)
        up = jnp.dot(xb, wu_ref[:, c0:c0 + FFN_CHUNK], preferred_element_type=_F32)
        k = c0 // FFN_CHUNK - 1
        if after is not None and 0 <= k < len(after):
            up = up + _zero_after(after[k], FFN_CHUNK)
        h_ref[:, c0:c0 + FFN_CHUNK] = (_silu(gate) * up).astype(_BF16)
    return jnp.dot(h_ref[...], wd_ref[...], preferred_element_type=_F32)


def _const_spec(shape):
    zeros = (0,) * len(shape)
    return pl.BlockSpec(shape, lambda *_: zeros, pipeline_mode=pl.Buffered(1))


def _ffn_ln(x, wg, wu, wd, g, b):
    n, d = x.shape
    d_ff = wg.shape[1]
    tm = min(FFN_ROWS, n)
    nt = n // tm
    return pl.pallas_call(
        _ffn_ln_kernel,
        grid=(nt + 1,),
        in_specs=[pl.BlockSpec((tm, d), lambda i: (jnp.minimum(i, nt - 1), 0)),
                  _const_spec((d, d_ff)), _const_spec((d, d_ff)), _const_spec((d_ff, d)),
                  _const_spec((1, d)), _const_spec((1, d))],
        out_specs=pl.BlockSpec((tm, d), lambda i: (jnp.maximum(i - 1, 0), 0)),
        out_shape=jax.ShapeDtypeStruct((n, d), _F32),
        scratch_shapes=[pltpu.VMEM((tm, d_ff), _BF16), pltpu.VMEM((tm, d), _F32)],
        compiler_params=pltpu.CompilerParams(dimension_semantics=("arbitrary",),
                                             vmem_limit_bytes=VMEM_LIMIT),
        name="ffn_ln",
    )(x, wg.astype(_BF16), wu.astype(_BF16), wd.astype(_BF16), g, b)


_QKV = 3 * GDN_WIDTH
_NAT_KV = _QKV + GDN_WIDTH + LANES
_T_VS = NSA_WIDTH
_T_VW = _T_VS + NSA_KV_WIDTH
_T_GATE = _T_VW + NSA_KV_WIDTH
_T_ROWS = _T_GATE + 32
Q_SCALE = NSA_HEAD_DIM ** -0.5 * 1.4426950408889634


def _proj_kernel(x_ref, xh_ref, wn_ref, wt_ref, cw_ref, qkv_ref, z_ref, small_ref, kc_ref, vc_ref,
                 ks_ref, kw_ref, qt_ref, vst_ref, vwt_ref, gt_ref, kvs_ref, *, qt, tiles_per_seq):
    hk_n, grp, dh = NSA_KV_HEADS, NSA_GROUP, NSA_HEAD_DIM
    xb = x_ref[...].astype(_BF16)
    tm = xb.shape[0]
    halo = xh_ref.shape[0]
    at_start = pl.program_id(0) % tiles_per_seq == 0
    xh = jnp.where(at_start, 0.0, xh_ref[...]).astype(_BF16)
    xcat = jnp.concatenate([xh, xb], axis=0)

    def conv_chunk(c0, width):
        cols = slice(c0, c0 + width)
        pre = jnp.dot(xcat, wn_ref[:, cols], preferred_element_type=_F32)
        conv = cw_ref[GDN_CONV - 1:GDN_CONV, cols] * pre[halo:, :]
        for d in range(1, GDN_CONV):
            conv = conv + cw_ref[GDN_CONV - 1 - d:GDN_CONV - d, cols] * pltpu.roll(pre, d, 0)[halo:, :]
        qkv_ref[:, cols] = _silu(conv)

    def nat_cols(ref, c0, width, r0=0):
        ref[:, r0:r0 + width] = jnp.dot(xb, wn_ref[:, c0:c0 + width], preferred_element_type=_F32)

    def kv_heads():
        kv = jnp.dot(xb, wn_ref[:, _NAT_KV:_NAT_KV + 4 * NSA_KV_WIDTH], preferred_element_type=_F32)
        for hk in range(hk_n):
            for idx, ref in enumerate((ks_ref, kw_ref)):
                c0 = (2 + idx) * NSA_KV_WIDTH + hk * dh
                ref[0, hk] = kv[:, c0:c0 + dh].astype(ref.dtype)
        stride = NSA_CMP_STRIDE
        for idx, ref in enumerate((kc_ref, vc_ref)):
            kvs_ref[idx] = kv[:, idx * NSA_KV_WIDTH:(idx + 1) * NSA_KV_WIDTH]
            for r in range(stride):
                rows = kvs_ref[idx, pl.ds(r, tm // stride, stride=stride), :]
                for hk in range(hk_n):
                    ref[0, hk, :, r * dh:(r + 1) * dh] = rows[:, hk * dh:(hk + 1) * dh]

    def q_heads(hk):
        r0 = hk * grp * dh
        yt = lax.dot_general(wt_ref[r0:r0 + grp * dh, :], xb, (((1,), (1,)), ((), ())),
                             preferred_element_type=_F32)
        for ii in range(tm // qt):
            for g in range(grp):
                qt_ref[0, hk, ii, :, g * qt:(g + 1) * qt] = (
                    yt[g * dh:(g + 1) * dh, ii * qt:(ii + 1) * qt] * Q_SCALE).astype(_BF16)

    def v_gates():
        yt = lax.dot_general(wt_ref[_T_VS:_T_ROWS, :], xb, (((1,), (1,)), ((), ())),
                             preferred_element_type=_F32)
        for hk in range(hk_n):
            for ii in range(tm // qt):
                cols = slice(ii * qt, (ii + 1) * qt)
                vst_ref[0, hk, ii] = yt[hk * dh:(hk + 1) * dh, cols].astype(_BF16)
                vwt_ref[0, hk, ii] = yt[NSA_KV_WIDTH + hk * dh:NSA_KV_WIDTH + (hk + 1) * dh,
                                        cols].astype(_BF16)
                for g in range(grp):
                    for c in range(3):
                        r = 2 * NSA_KV_WIDTH + (hk * grp + g) * 3 + c
                        gt_ref[0, hk, ii, c:c + 1, g * qt:(g + 1) * qt] = yt[r:r + 1, cols]

    step = 2 * LANES
    others = [lambda: nat_cols(z_ref, _QKV, step), lambda: nat_cols(z_ref, _QKV + step, step, step),
              lambda: (nat_cols(small_ref, _QKV + GDN_WIDTH, LANES), kv_heads()),
              lambda: q_heads(0), lambda: q_heads(1), v_gates]
    assert _QKV // step == len(others) and GDN_WIDTH == 2 * step and hk_n == 2
    for k, other in enumerate(others):
        conv_chunk(k * step, step)
        other()


def _proj_weights(w):
    d = w.shape[0]
    sizes = (GDN_WIDTH,) * 4 + (GDN_HEADS,) * 2 + (NSA_WIDTH,) + (NSA_KV_WIDTH,) * 6 + (3 * NSA_Q_HEADS,)
    off = np.concatenate([[0], np.cumsum(sizes)])
    gq, gk, gv, gz, gb, ga, nq, kc, vc, ks, vs, kw, vw, gate = [
        w[:, int(off[i]):int(off[i + 1])] for i in range(len(sizes))]
    small = jnp.concatenate([gb, ga, jnp.zeros((d, LANES - 2 * GDN_HEADS), w.dtype)], axis=1)
    w_nat = jnp.concatenate([gq, gk, gv, gz, small, kc, vc, ks, kw], axis=1)
    w_t = jnp.concatenate([nq, vs, vw, gate, jnp.zeros((d, _T_ROWS - _T_GATE - gate.shape[1]), w.dtype)],
                          axis=1).T
    return w_nat.astype(_BF16), w_t.astype(_BF16)


def _in_proj(x, w_nat, w_t, conv_w, bsz, t):
    n, d = x.shape
    hk, grp, dh = NSA_KV_HEADS, NSA_GROUP, NSA_HEAD_DIM
    tm = min(PROJ_ROWS, t)
    qt = min(NSA_QT, t)
    nt = t // tm
    halo = 16
    halo_spec = pl.BlockSpec((halo, d), lambda r: (jnp.maximum(r * (tm // halo) - 1, 0), 0))
    row = lambda width: pl.BlockSpec((tm, width), lambda r: (r, 0))
    head = pl.BlockSpec((1, hk, tm, dh), lambda r: (r // nt, 0, r % nt, 0))
    cs = NSA_CMP_STRIDE
    cmp_in = pl.BlockSpec((1, hk, tm // cs, cs * dh), lambda r: (r // nt, 0, r % nt, 0))
    tile = lambda rows, lanes: pl.BlockSpec((1, hk, tm // qt, rows, lanes),
                                            lambda r: (r // nt, 0, r % nt, 0, 0))
    sds = jax.ShapeDtypeStruct
    return pl.pallas_call(
        functools.partial(_proj_kernel, qt=qt, tiles_per_seq=nt),
        grid=(n // tm,),
        in_specs=[pl.BlockSpec((tm, d), lambda r: (r, 0)), halo_spec, _const_spec(w_nat.shape),
                  _const_spec(w_t.shape), _const_spec(conv_w.shape)],
        out_specs=[row(_QKV), row(GDN_WIDTH), row(LANES), cmp_in, cmp_in, head, head,
                   tile(dh, grp * qt), tile(dh, qt), tile(dh, qt), tile(3, grp * qt)],
        out_shape=[sds((n, _QKV), _F32), sds((n, GDN_WIDTH), _F32), sds((n, LANES), _F32),
                   sds((bsz, hk, t // cs, cs * dh), _F32), sds((bsz, hk, t // cs, cs * dh), _F32),
                   sds((bsz, hk, t, dh), _BF16), sds((bsz, hk, t, dh), _BF16),
                   sds((bsz, hk, t // qt, dh, grp * qt), _BF16),
                   sds((bsz, hk, t // qt, dh, qt), _BF16), sds((bsz, hk, t // qt, dh, qt), _BF16),
                   sds((bsz, hk, t // qt, 3, grp * qt), _F32)],
        scratch_shapes=[pltpu.VMEM((2, tm, NSA_KV_WIDTH), _F32)],
        compiler_params=pltpu.CompilerParams(dimension_semantics=("parallel",),
                                             vmem_limit_bytes=VMEM_LIMIT),
        name="in_proj",
    )(x, x, w_nat, w_t, conv_w)


def _bdot(a, b):
    return jnp.dot(a, b, preferred_element_type=_F32)


def _unit_lower_inverses(lmats):
    c = lmats[0].shape[0]
    eye = jnp.where(lax.broadcasted_iota(jnp.int32, (c, c), 0)
                    == lax.broadcasted_iota(jnp.int32, (c, c), 1), 1.0, 0.0)
    rs = [eye - l for l in lmats]
    pb = [l.astype(_BF16) for l in lmats]
    ps = [_bdot(l, l) for l in pb]
    span = 2
    while True:
        pb = [p.astype(_BF16) for p in ps]
        rs = [r + _bdot(r.astype(_BF16), p) for r, p in zip(rs, pb)]
        span *= 2
        if span >= c:
            return rs
        ps = [_bdot(p, p) for p in pb]


def _gdn_kernel(act_ref, z_ref, bac_ref, bar_ref, hpr_ref, hpc_ref, nw_ref, o_ref, s_ref, *, tb):
    nh, hd, c = GDN_HEADS, GDN_HEAD_DIM, GDN_CHUNK

    @pl.when(pl.program_id(1) == 0)
    def _():
        s_ref[...] = jnp.zeros_like(s_ref)

    row = lax.broadcasted_iota(jnp.int32, (c, c), 0)
    col = lax.broadcasted_iota(jnp.int32, (c, c), 1)
    causal = row >= col
    strict = row > col
    tril = jnp.where(causal, 1.0, 0.0).astype(_BF16)
    triu = jnp.where(row <= col, 1.0, 0.0).astype(_BF16)

    nchunk = tb // c
    nseq = act_ref.shape[0]
    qg_l, qb_l, kb_l, kbeta_l, rhs_l, kd_l, decay_l, eg_l = [], [], [], [], [], [], [], []
    for bi, ci in ((bi, ci) for bi in range(nseq) for ci in range(nchunk)):
        r0 = ci * c
        bac = bac_ref[bi, r0:r0 + c, :]
        beta_all = _sigmoid(bac)
        g_all = -jnp.exp(hpr_ref[0:1, :]) * _softplus(bac + hpr_ref[1:2, :])
        gc_all = sum(_bdot(tril, part) for part in _split3(g_all))
        bar = bar_ref[bi, ci]
        gr_all = -jnp.exp(hpc_ref[:, 0:1]) * _softplus(bar + hpc_ref[:, 1:2])
        gcr_all = sum(_bdot(part, triu) for part in _split3(gr_all))
        for h in range(nh):
            q = act_ref[bi, r0:r0 + c, h * hd:(h + 1) * hd]
            k = act_ref[bi, r0:r0 + c, (nh + h) * hd:(nh + h + 1) * hd]
            v = act_ref[bi, r0:r0 + c, (2 * nh + h) * hd:(2 * nh + h + 1) * hd]
            q = q * lax.rsqrt(jnp.sum(q * q, axis=-1, keepdims=True) + L2_EPS) * (hd ** -0.5)
            k = k * lax.rsqrt(jnp.sum(k * k, axis=-1, keepdims=True) + L2_EPS)
            beta = beta_all[:, h:h + 1]
            gcc = gc_all[:, nh + h:nh + h + 1]
            gcr = gcr_all[nh + h:nh + h + 1, :]
            glast = gc_all[c - 1:c, nh + h:nh + h + 1]
            decay_l.append(jnp.exp(jnp.where(causal, gcc - gcr, NEG_BIG)))
            kbeta = k * beta
            egc = jnp.exp(gcc)
            qg_l.append(q * egc)
            qb_l.append(q.astype(_BF16))
            kb_l.append(k.astype(_BF16))
            kbeta_l.append(kbeta.astype(_BF16))
            rhs_l.append(jnp.concatenate([kbeta * egc, v * beta], axis=1).astype(_BF16))
            kd_l.append((k * jnp.exp(glast - gcc)).astype(_BF16))
            eg_l.append(jnp.exp(glast))

    nt = (((1,), (1,)), ((), ()))
    tn = (((0,), (0,)), ((), ()))
    kk_l = [lax.dot_general(a, b, nt, preferred_element_type=_F32) for a, b in zip(kbeta_l, kb_l)]
    qk_l = [lax.dot_general(a, b, nt, preferred_element_type=_F32) for a, b in zip(qb_l, kb_l)]
    lmat_l = [jnp.where(strict, kk * d, 0.0) for kk, d in zip(kk_l, decay_l)]
    tinv_l = _unit_lower_inverses(lmat_l)
    wu_l = [_bdot(t.astype(_BF16), r).astype(_BF16) for t, r in zip(tinv_l, rhs_l)]
    qkb_l = [(qk * d).astype(_BF16) for qk, d in zip(qk_l, decay_l)]
    pn_l = [lax.dot_general(kd, wu, tn, preferred_element_type=_F32) for kd, wu in zip(kd_l, wu_l)]
    ab_l = [_bdot(qk, wu) for qk, wu in zip(qkb_l, wu_l)]
    a_l = [(qg - ab[:, :hd]).astype(_BF16) for qg, ab in zip(qg_l, ab_l)]
    p_l = [pn[:, :hd].astype(_BF16) for pn in pn_l]

    chains = [(bi, h) for bi in range(nseq) for h in range(nh)]
    states = [s_ref[bi * nh + h] for bi, h in chains]
    for ci in range(nchunk):
        r0 = ci * c
        idx = [(bi * nchunk + ci) * nh + h for bi, h in chains]
        sb = [s.astype(_BF16) for s in states]
        outs = [_bdot(a_l[e], sb[n]) + ab_l[e][:, hd:] for n, e in enumerate(idx)]
        states = [states[n] * eg_l[e] + pn_l[e][:, hd:] - _bdot(p_l[e], sb[n])
                  for n, e in enumerate(idx)]
        for n, (bi, h) in enumerate(chains):
            o = outs[n]
            zz = z_ref[bi, r0:r0 + c, h * hd:(h + 1) * hd]
            o = (o * lax.rsqrt(jnp.mean(o * o, axis=-1, keepdims=True) + RMS_EPS) * nw_ref[...]
                 * _silu(zz))
            o_ref[bi, r0:r0 + c, h * hd:(h + 1) * hd] = o
    for n, (bi, h) in enumerate(chains):
        s_ref[bi * nh + h] = states[n]


def _gdn(qkv, z, small, a_log, dt_bias, norm_w):
    bsz, t, width = qkv.shape
    nh, c = GDN_HEADS, GDN_CHUNK
    nseq = GDN_SEQS if bsz % GDN_SEQS == 0 else 1
    tb = min(GDN_ROWS, t)
    ba_rows = small[:, :, :8].reshape(bsz, t // c, c, 8).transpose(0, 1, 3, 2)
    hp_row = jnp.zeros((8, LANES), _F32)
    hp_row = hp_row.at[0, nh:2 * nh].set(a_log).at[1, nh:2 * nh].set(dt_bias)
    hp_col = jnp.zeros((8, LANES), _F32)
    hp_col = hp_col.at[nh:2 * nh, 0].set(a_log).at[nh:2 * nh, 1].set(dt_bias)
    return pl.pallas_call(
        functools.partial(_gdn_kernel, tb=tb),
        grid=(bsz // nseq, t // tb),
        in_specs=[pl.BlockSpec((nseq, tb, width), lambda b, i: (b, i, 0)),
                  pl.BlockSpec((nseq, tb, GDN_WIDTH), lambda b, i: (b, i, 0)),
                  pl.BlockSpec((nseq, tb, LANES), lambda b, i: (b, i, 0)),
                  pl.BlockSpec((nseq, tb // c, 8, c), lambda b, i: (b, i, 0, 0)),
                  pl.BlockSpec((8, LANES), lambda b, i: (0, 0)),
                  pl.BlockSpec((8, LANES), lambda b, i: (0, 0)),
                  pl.BlockSpec((1, GDN_HEAD_DIM), lambda b, i: (0, 0))],
        out_specs=pl.BlockSpec((nseq, tb, GDN_WIDTH), lambda b, i: (b, i, 0)),
        out_shape=jax.ShapeDtypeStruct((bsz, t, GDN_WIDTH), _F32),
        scratch_shapes=[pltpu.VMEM((nseq * nh, GDN_HEAD_DIM, GDN_HEAD_DIM), _F32)],
        compiler_params=pltpu.CompilerParams(dimension_semantics=("parallel", "arbitrary"),
                                             vmem_limit_bytes=VMEM_LIMIT),
        name="gdn",
    )(qkv, z, small, ba_rows, hp_row, hp_col, norm_w.reshape(1, -1))


def _gelu_tanh(x):
    return 0.5 * x * (1.0 + jnp.tanh(0.7978845608028654 * (x + 0.044715 * (x * x * x))))


def _cmp_kernel(xk_ref, xv_ref, pk_ref, pv_ref, wk1_ref, wk2_ref, wv1_ref, wv2t_ref, ok_ref, ovt_ref):
    def hidden(x, pos_ref, w1_ref):
        n = x.shape[0]
        ya = _mm(x + pos_ref[0:1, :], w1_ref[0])
        yb = _mm(x + pos_ref[1:2, :], w1_ref[1])
        hid = ya + pltpu.roll(yb, n - 1, 0)
        return _gelu_tanh(hid).astype(_BF16)

    ok_ref[0, 0] = _bdot(hidden(xk_ref[0, 0], pk_ref, wk1_ref), wk2_ref[...]).astype(_BF16)
    ovt_ref[0, 0] = lax.dot_general(wv2t_ref[...], hidden(xv_ref[0, 0], pv_ref, wv1_ref),
                                    (((1,), (1,)), ((), ())),
                                    preferred_element_type=_F32).astype(_BF16)


def _nsa_compress(xk, xv, pos_k, pos_v, k_w1, k_w2, v_w1, v_w2):
    bsz, hk, nrow, half = xk.shape
    dh = half // NSA_CMP_STRIDE
    hidden = k_w1.shape[1]
    x_spec = pl.BlockSpec((1, 1, nrow, half), lambda b, h: (b, h, 0, 0))
    full = lambda shape: pl.BlockSpec(shape, lambda b, h: (0,) * len(shape))
    return pl.pallas_call(
        _cmp_kernel,
        grid=(bsz, hk),
        in_specs=[x_spec, x_spec, full((2, half)), full((2, half)),
                  full((2, half, hidden)), full((hidden, dh)),
                  full((2, half, hidden)), full((dh, hidden))],
        out_specs=[pl.BlockSpec((1, 1, nrow, dh), lambda b, h: (b, h, 0, 0)),
                   pl.BlockSpec((1, 1, dh, nrow), lambda b, h: (b, h, 0, 0))],
        out_shape=[jax.ShapeDtypeStruct((bsz, hk, nrow, dh), _BF16),
                   jax.ShapeDtypeStruct((bsz, hk, dh, nrow), _BF16)],
        compiler_params=pltpu.CompilerParams(dimension_semantics=("parallel", "parallel"),
                                             vmem_limit_bytes=VMEM_LIMIT),
        name="nsa_cmp",
    )(xk, xv, pos_k.reshape(2, half), pos_v.reshape(2, half),
      k_w1.reshape(2, half, hidden).astype(_BF16), k_w2.astype(_BF16),
      v_w1.reshape(2, half, hidden).astype(_BF16), v_w2.T.astype(_BF16))


def _nsa_kernel(q_ref, kc_ref, vct_ref, cst_ref, ks_ref, vst_ref, kw_ref, vwt_ref, gt_ref, o_ref,
                kaug_ref, acc_ref, s_buf, p_buf, bias_ref, *, qt, kt, n_sel):
    assert qt == kt
    grp, dh = NSA_GROUP, NSA_HEAD_DIM
    heads = range(q_ref.shape[1])
    lanes = grp * qt
    t_all = ks_ref.shape[2]
    ns = t_all // NSA_SEL_LEN
    nc = kc_ref.shape[2]
    i = pl.program_id(1)
    s0 = i * qt

    @pl.when(i == 0)
    def _():
        rblk = lax.broadcasted_iota(jnp.int32, (t_all, ns), 0) // NSA_SEL_LEN
        cblk = lax.broadcasted_iota(jnp.int32, (t_all, ns), 1)
        onehot = jnp.where(rblk == cblk, 1.0, 0.0).astype(_BF16)
        for h in heads:
            kaug_ref[h, :, 0:dh] = ks_ref[0, h]
            kaug_ref[h, :, dh:dh + ns] = onehot
        krow = lax.broadcasted_iota(jnp.int32, (kt, lanes), 0)
        tl = lax.broadcasted_iota(jnp.int32, (kt, lanes), 1) & (qt - 1)
        far_off = ((NSA_WINDOW + kt - 1) // kt) * kt - NSA_WINDOW
        bias_ref[0] = jnp.where(krow <= tl, 0.0, NEG_BIG)
        bias_ref[1] = jnp.where(krow > tl + far_off, 0.0, NEG_BIG)
        bias_ref[2] = jnp.full((kt, lanes), NEG_BIG, _F32)

    lane = lax.broadcasted_iota(jnp.int32, (1, lanes), 1)
    tq = s0 + (lane & (qt - 1))
    qs = [q_ref[0, h, 0] for h in heads]

    cend = lax.broadcasted_iota(jnp.int32, (nc, 1), 0) * NSA_CMP_STRIDE + (NSA_CMP_LEN - 1)
    valid = cend <= tq
    sc = [jnp.dot(kc_ref[0, h], qs[h], preferred_element_type=_F32) for h in heads]
    pc = []
    for h in heads:
        scm = jnp.where(valid, sc[h], NEG_BIG)
        mc = jnp.max(scm, axis=0, keepdims=True)
        ec = jnp.where(valid, jnp.exp2(scm - mc), 0.0)
        lc = jnp.sum(ec, axis=0, keepdims=True)
        pc.append(ec / jnp.where(lc > 0.0, lc, 1.0))
    o_cmp = [jnp.dot(vct_ref[0, h], pc[h].astype(_BF16), preferred_element_type=_F32)
             for h in heads]

    cst = cst_ref[...]
    blk = lax.broadcasted_iota(jnp.int32, (ns, qt), 0)
    tq1 = s0 + lax.broadcasted_iota(jnp.int32, (ns, qt), 1)
    cur = tq1 // NSA_SEL_LEN
    visible = blk * NSA_SEL_LEN <= tq1
    forced = (blk == 0) | (blk == cur) | (blk == cur - 1)
    sub = lax.broadcasted_iota(jnp.int32, (8, qt), 0)
    qaug = []
    for h in heads:
        psum = pc[h][:, 0:qt]
        for g in range(1, grp):
            psum = psum + pc[h][:, g * qt:(g + 1) * qt]
        p_hi = psum.astype(_BF16)
        p_lo = (psum - p_hi.astype(_F32)).astype(_BF16)
        imp = (jnp.dot(cst, p_hi, preferred_element_type=_F32)
               + jnp.dot(cst, p_lo, preferred_element_type=_F32))
        imp = jnp.where(visible, jnp.where(forced, NSA_FORCE_SCORE, imp), -1.0)
        imp_rows = [imp[8 * r:8 * r + 8, :] for r in range(ns // 8)]
        ranks = [jnp.zeros((8, qt), _F32) for _ in imp_rows]
        for j in range(ns):
            rowj = imp[j:j + 1, :]
            for r, rows in enumerate(imp_rows):
                if 8 * r > j:
                    beats = jnp.where(rowj >= rows, 1.0, 0.0)
                elif 8 * r + 7 <= j:
                    beats = jnp.where(rowj > rows, 1.0, 0.0)
                else:
                    beats = jnp.where(sub + 8 * r > j, jnp.where(rowj >= rows, 1.0, 0.0),
                                      jnp.where(rowj > rows, 1.0, 0.0))
                ranks[r] = ranks[r] + beats
        rank = jnp.concatenate(ranks, axis=0)
        bias = jnp.where(rank < n_sel, 0.0, NEG_BIG).astype(_BF16)
        qaug.append(jnp.concatenate([qs[h], jnp.concatenate([bias] * grp, axis=1)], axis=0))

    m0 = jnp.full((1, lanes), M_INIT, _F32)
    ones_rows = jnp.where(lax.broadcasted_iota(jnp.int32, (acc_ref.shape[1] - dh, kt), 0) == 0,
                          1.0, 0.0).astype(_BF16)

    def v_aug(v_tile):
        return jnp.concatenate([v_tile, ones_rows], axis=0)

    nwt = (NSA_WINDOW + kt - 1) // kt + 1
    s_win = [[] for _ in heads]
    j_win = []
    for d in range(nwt):
        back = nwt - 1 - d
        j = i - back
        jc = jnp.maximum(j, 0)
        j_win.append(jc)
        for h in heads:
            k_tile = kw_ref[0, h, pl.ds(pl.multiple_of(jc * kt, kt), kt), :]
            s = jnp.dot(k_tile, qs[h], preferred_element_type=_F32)
            if back == 0:
                s = s + bias_ref[0]
            elif back == nwt - 1:
                s = s + bias_ref[jnp.where(j >= 0, 1, 2)]
            else:
                s = s + jnp.where(j >= 0, 0.0, NEG_BIG)
            s_win[h].append(s)
    o_win = []
    for h in heads:
        mw = s_win[h][0].max(axis=0, keepdims=True)
        for s in s_win[h][1:]:
            mw = jnp.maximum(mw, s.max(axis=0, keepdims=True))
        acc_w = jnp.zeros(acc_ref.shape[1:], _F32)
        for s, jc in zip(s_win[h], j_win):
            acc_w = acc_w + jnp.dot(v_aug(vwt_ref[0, h, jc]), jnp.exp2((s - mw).astype(_BF16)),
                                    preferred_element_type=_F32)
        o_win.append(acc_w[0:dh] / acc_w[dh:dh + 1])

    n_s, n_p = s_buf.shape[1], p_buf.shape[1]
    last_tile = t_all // kt - 1

    def qk_tile(h, j):
        start = pl.multiple_of(jnp.minimum(j, last_tile) * kt, kt)
        return jnp.dot(kaug_ref[h, pl.ds(start, kt), :], qaug[h], preferred_element_type=_F32)

    def softmax_step(s, m):
        m_new = jnp.maximum(m, jnp.max(s, axis=0, keepdims=True))
        return m_new, jnp.exp2(m - m_new), jnp.exp2((s - m_new).astype(_BF16))

    acc_ref[...] = jnp.zeros_like(acc_ref)
    for h in heads:
        p_buf[h, n_p - 1] = jnp.zeros((kt, lanes), _BF16)
        s_buf[h, 0] = qk_tile(h, 0)
        s_buf[h, 1] = qk_tile(h, 1)

    def sel_step(j, c, carry):
        for h in heads:
            s_buf[h, (c + 2) % n_s] = qk_tile(h, j + 2)
        pv = [jnp.dot(v_aug(vst_ref[0, h, jnp.maximum(j - 1, 0)]), p_buf[h, (c - 1) % n_p],
                      preferred_element_type=_F32) for h in heads]
        out = []
        for h in heads:
            m, a_prev = carry[h]
            m, a, p = softmax_step(s_buf[h, c], m)
            p_buf[h, c % n_p] = p
            acc_ref[h] = a_prev * acc_ref[h] + pv[h]
            out.append((m, a))
        return tuple(out)

    def sel_round(jj, carry):
        for c in range(n_s):
            carry = sel_step(n_s * jj + c, c, carry)
        return carry

    carry = lax.fori_loop(0, i // n_s, sel_round,
                          tuple((m0, jnp.ones((1, lanes), _F32)) for _ in heads))
    base = (i // n_s) * n_s
    rem = i - base
    carry = lax.cond(rem >= 2, lambda x: sel_step(base + 1, 1, sel_step(base, 0, x)), lambda x: x,
                     carry)
    carry = lax.cond(rem == 1, functools.partial(sel_step, base, 0), lambda x: x, carry)
    carry = lax.cond(rem == 3, functools.partial(sel_step, base + 2, 2), lambda x: x, carry)
    out_rows = []
    for h in heads:
        m, a_prev = carry[h]
        pv = jnp.dot(v_aug(vst_ref[0, h, jnp.maximum(i - 1, 0)]), p_buf[h, (i + n_p - 1) % n_p],
                     preferred_element_type=_F32)
        acc = a_prev * acc_ref[h] + pv
        s = s_buf[h, i % n_s] + bias_ref[0]
        m, a, p = softmax_step(s, m)
        acc = a * acc + jnp.dot(v_aug(vst_ref[0, h, i]), p, preferred_element_type=_F32)
        o_sel = acc[0:dh] / acc[dh:dh + 1]
        gate = _sigmoid(gt_ref[0, h, 0])
        out_t = gate[0:1, :] * o_cmp[h] + gate[1:2, :] * o_sel + gate[2:3, :] * o_win[h]
        out_rows += [out_t[:, g * qt:(g + 1) * qt] for g in range(grp)]
    o_ref[0] = jnp.concatenate(out_rows, axis=0).T


def _nsa_attention(qt_arr, k_cmp, v_cmp_t, cst, ks, vs_t, kw, vw_t, gates_t):
    bsz, hk, nq, dh, lanes = qt_arr.shape
    qt = lanes // NSA_GROUP
    kt = qt
    t = ks.shape[2]
    ns = t // NSA_SEL_LEN
    nc = k_cmp.shape[2]
    ntile = t // kt
    per_seq = lambda shape: pl.BlockSpec((1, hk) + shape, lambda b, i: (b, 0) + (0,) * len(shape))
    per_tile = lambda shape: pl.BlockSpec((1, hk, 1) + shape,
                                          lambda b, i: (b, 0, i) + (0,) * len(shape))
    return pl.pallas_call(
        functools.partial(_nsa_kernel, qt=qt, kt=kt, n_sel=min(NSA_TOPN, ns)),
        grid=(bsz, nq),
        in_specs=[per_tile((dh, lanes)),
                  per_seq((nc, dh)), per_seq((dh, nc)),
                  pl.BlockSpec((ns, nc), lambda b, i: (0, 0)),
                  per_seq((t, dh)), per_seq((ntile, dh, kt)),
                  per_seq((t, dh)), per_seq((ntile, dh, kt)),
                  per_tile((3, lanes))],
        out_specs=pl.BlockSpec((1, qt, hk * NSA_GROUP * dh), lambda b, i: (b, i, 0)),
        out_shape=jax.ShapeDtypeStruct((bsz, t, hk * NSA_GROUP * dh), _F32),
        scratch_shapes=[pltpu.VMEM((hk, t, dh + ns), _BF16),
                        pltpu.VMEM((hk, dh + 16, lanes), _F32),
                        pltpu.VMEM((hk, 4, kt, lanes), _F32),
                        pltpu.VMEM((hk, 2, kt, lanes), _BF16),
                        pltpu.VMEM((3, kt, lanes), _F32)],
        compiler_params=pltpu.CompilerParams(
            dimension_semantics=("parallel", "arbitrary"),
            vmem_limit_bytes=VMEM_LIMIT),
        name="nsa_attn",
    )(qt_arr, k_cmp, v_cmp_t, cst, ks, vs_t, kw, vw_t, gates_t)


def _cmp_to_sel_t(t):
    ncmp = (t - NSA_CMP_LEN) // NSA_CMP_STRIDE + 1
    c_start = np.arange(ncmp) * NSA_CMP_STRIDE
    s_start = np.arange(t // NSA_SEL_LEN) * NSA_SEL_LEN
    overlap = np.clip(np.minimum(c_start[:, None] + NSA_CMP_LEN, s_start[None, :] + NSA_SEL_LEN)
                      - np.maximum(c_start[:, None], s_start[None, :]), 0, None) / NSA_CMP_LEN
    out = np.zeros((t // NSA_SEL_LEN, t // NSA_CMP_STRIDE), np.float32)
    out[:, :ncmp] = overlap.T
    return jnp.asarray(out, _BF16)


def _nsa(q_t, kc, vc, ks, kw, vs_t, vw_t, gates_t, pos_k, pos_v, k_w1, k_w2, v_w1, v_w2):
    k_cmp, v_cmp_t = _nsa_compress(kc, vc, pos_k, pos_v, k_w1, k_w2, v_w1, v_w2)
    return _nsa_attention(q_t, k_cmp, v_cmp_t, _cmp_to_sel_t(ks.shape[2]), ks, vs_t, kw, vw_t, gates_t)


def _out_ffn_kernel(x_ref, a_ref, b_ref, wa_ref, wb_ref, g2_ref, b2_ref, wg_ref, wu_ref, wd_ref,
                    g3_ref, b3_ref, o_ref, h_ref, y_ref):
    @pl.when(pl.program_id(0) == 0)
    def _():
        y_ref[...] = jnp.zeros_like(y_ref)

    ln3 = _layer_norm_pieces(y_ref, g3_ref, b3_ref, o_ref, 8)
    mix = (jnp.dot(a_ref[...].astype(_BF16), wa_ref[...], preferred_element_type=_F32)
           + jnp.dot(b_ref[...].astype(_BF16), wb_ref[...], preferred_element_type=_F32))
    x2 = _layer_norm(DN_ALPHA * x_ref[...] + mix, g2_ref[...], b2_ref[...])
    y_ref[...] = DN_ALPHA * x2 + 0.5 * _swiglu(x2.astype(_BF16), wg_ref, wu_ref, wd_ref, h_ref,
                                               after=ln3)


def _out_ffn(x, oa, ob, wa, wb, g2, b2, wg, wu, wd, g3, b3):
    n, d = x.shape
    d_ff = wg.shape[1]
    tm = min(FFN_ROWS, n)
    nt = n // tm
    row = lambda width: pl.BlockSpec((tm, width), lambda i: (jnp.minimum(i, nt - 1), 0))
    return pl.pallas_call(
        _out_ffn_kernel,
        grid=(nt + 1,),
        in_specs=[row(d), row(oa.shape[1]), row(ob.shape[1]), _const_spec(wa.shape),
                  _const_spec(wb.shape), _const_spec((1, d)), _const_spec((1, d)),
                  _const_spec((d, d_ff)), _const_spec((d, d_ff)), _const_spec((d_ff, d)),
                  _const_spec((1, d)), _const_spec((1, d))],
        out_specs=pl.BlockSpec((tm, d), lambda i: (jnp.maximum(i - 1, 0), 0)),
        out_shape=jax.ShapeDtypeStruct((n, d), _F32),
        scratch_shapes=[pltpu.VMEM((tm, d_ff), _BF16), pltpu.VMEM((tm, d), _F32)],
        compiler_params=pltpu.CompilerParams(dimension_semantics=("arbitrary",),
                                             vmem_limit_bytes=VMEM_LIMIT),
        name="out_ffn",
    )(x, oa, ob, wa, wb, g2, b2, wg.astype(_BF16), wu.astype(_BF16), wd.astype(_BF16), g3, b3)


def kernel(x, ln1_g, ln1_b, ffn1_wg, ffn1_wu, ffn1_wd, w_in, gdn_conv_w, gdn_a_log, gdn_dt_bias,
           gdn_norm_w, nsa_cmp_pos_k, nsa_cmp_pos_v, nsa_cmp_k_w1, nsa_cmp_k_w2, nsa_cmp_v_w1,
           nsa_cmp_v_w2, w_out, ln2_g, ln2_b, ffn2_wg, ffn2_wu, ffn2_wd, ln3_g, ln3_b):
    bsz, t, d = x.shape
    n = bsz * t
    h = x.reshape(n, d)
    for i in range(DEPTH):
        h = _ffn_ln(h, ffn1_wg[i], ffn1_wu[i], ffn1_wd[i], ln1_g[i].reshape(1, d), ln1_b[i].reshape(1, d))
        w_nat, w_t = _proj_weights(w_in[i])
        qkv, z, small, kc, vc, ks, kw, q_t, vs_t, vw_t, gates_t = _in_proj(
            h, w_nat, w_t, gdn_conv_w[i], bsz, t)
        o_gdn = _gdn(qkv.reshape(bsz, t, _QKV), z.reshape(bsz, t, GDN_WIDTH),
                     small.reshape(bsz, t, LANES), gdn_a_log[i], gdn_dt_bias[i], gdn_norm_w[i])
        o_nsa = _nsa(q_t, kc, vc, ks, kw, vs_t, vw_t, gates_t,
                     nsa_cmp_pos_k[i], nsa_cmp_pos_v[i], nsa_cmp_k_w1[i], nsa_cmp_k_w2[i],
                     nsa_cmp_v_w1[i], nsa_cmp_v_w2[i])
        wo = w_out[i].astype(_BF16)
        h = _out_ffn(h, o_gdn.reshape(n, GDN_WIDTH), o_nsa.reshape(n, NSA_WIDTH),
                     wo[:GDN_WIDTH], wo[GDN_WIDTH:], ln2_g[i].reshape(1, d), ln2_b[i].reshape(1, d),
                     ffn2_wg[i], ffn2_wu[i], ffn2_wd[i], ln3_g[i].reshape(1, d), ln3_b[i].reshape(1, d))
    return h.reshape(bsz, t, d)
```

```python
import functools

import numpy as np
import jax
import jax.numpy as jnp
from jax import lax
from jax.experimental import pallas as pl
from jax.experimental.pallas import tpu as pltpu

_F32 = jnp.float32
_BF16 = jnp.bfloat16

GDN_HEADS = 4
GDN_HEAD_DIM = 128
GDN_WIDTH = GDN_HEADS * GDN_HEAD_DIM
GDN_CONV = 4
GDN_CHUNK = 64

NSA_Q_HEADS = 8
NSA_KV_HEADS = 2
NSA_GROUP = NSA_Q_HEADS // NSA_KV_HEADS
NSA_HEAD_DIM = 64
NSA_WIDTH = NSA_Q_HEADS * NSA_HEAD_DIM
NSA_KV_WIDTH = NSA_KV_HEADS * NSA_HEAD_DIM
NSA_CMP_LEN = 32
NSA_CMP_STRIDE = 16
NSA_SEL_LEN = 64
NSA_TOPN = 16
NSA_WINDOW = 512
NSA_FORCE_SCORE = 1e4

LN_EPS = 1e-5
RMS_EPS = 1e-6
L2_EPS = 1e-6
DEPTH = 1
DN_ALPHA = (2 * DEPTH) ** 0.25

NEG_BIG = -(2.0 ** 100)
M_INIT = -3.0e38

LANES = 128
VMEM_LIMIT = 56 * 1024 * 1024

FFN_ROWS = 512
FFN_CHUNK = 256
PROJ_ROWS = 512
GDN_ROWS = 256
GDN_SEQS = 2
NSA_QT = 256


def _sigmoid(x):
    return 0.5 * jnp.tanh(0.5 * x) + 0.5


def _silu(x):
    h = 0.5 * x
    return h * jnp.tanh(h) + h


def _softplus(x):
    return jnp.maximum(x, 0.0) + jnp.log(1.0 + jnp.exp(-jnp.abs(x)))


def _mm(a, b):
    return jnp.dot(a.astype(_BF16), b.astype(_BF16), preferred_element_type=_F32)


def _split3(x):
    hi = x.astype(_BF16)
    r = x - hi.astype(_F32)
    mid = r.astype(_BF16)
    lo = (r - mid.astype(_F32)).astype(_BF16)
    return hi, mid, lo


def _layer_norm(y, g, b):
    mu = jnp.mean(y, axis=-1, keepdims=True)
    yc = y - mu
    var = jnp.mean(yc * yc, axis=-1, keepdims=True)
    return yc * lax.rsqrt(var + LN_EPS) * g + b


def _ffn_ln_kernel(x_ref, wg_ref, wu_ref, wd_ref, g_ref, b_ref, o_ref, h_ref, y_ref):
    @pl.when(pl.program_id(0) == 0)
    def _():
        y_ref[...] = jnp.zeros_like(y_ref)

    ln = _layer_norm_pieces(y_ref, g_ref, b_ref, o_ref, 8)
    x = x_ref[...]
    y_ref[...] = DN_ALPHA * x + 0.5 * _swiglu(x.astype(_BF16), wg_ref, wu_ref, wd_ref, h_ref,
                                              after=ln)


def _layer_norm_pieces(src, g_ref, b_ref, dst_ref, n_pieces):
    rows = src.shape[0] // n_pieces
    pieces = []
    for p in range(n_pieces):
        ln = _layer_norm(src[p * rows:(p + 1) * rows, :], g_ref[...], b_ref[...])
        dst_ref[p * rows:(p + 1) * rows, :] = ln
        pieces.append(ln)
    return pieces


def _zero_after(v, width):
    t = jnp.sum(v, axis=0, keepdims=True)
    folded = t[:, 0:width]
    for c0 in range(width, t.shape[1], width):
        folded = folded + t[:, c0:c0 + width]
    return folded * 0.0


def _swiglu(xb, wg_ref, wu_ref, wd_ref, h_ref, after=None):
    d_ff = wg_ref.shape[1]
    chunks = list(range(0, d_ff, FFN_CHUNK))
    for c0 in chunks:
        gate = jnp.dot(xb, wg_ref[:, c0:c0 + FFN_CHUNK], preferred_element_type=_F32)
        up = jnp.dot(xb, wu_ref[:, c0:c0 + FFN_CHUNK], preferred_element_type=_F32)
        k = c0 // FFN_CHUNK - 1
        if after is not None and 0 <= k < len(after):
            up = up + _zero_after(after[k], FFN_CHUNK)
        h_ref[:, c0:c0 + FFN_CHUNK] = (_silu(gate) * up).astype(_BF16)
    return jnp.dot(h_ref[...], wd_ref[...], preferred_element_type=_F32)


def _const_spec(shape):
    zeros = (0,) * len(shape)
    return pl.BlockSpec(shape, lambda *_: zeros, pipeline_mode=pl.Buffered(1))


def _ffn_ln(x, wg, wu, wd, g, b):
    n, d = x.shape
    d_ff = wg.shape[1]
    tm = min(FFN_ROWS, n)
    nt = n // tm
    return pl.pallas_call(
        _ffn_ln_kernel,
        grid=(nt + 1,),
        in_specs=[pl.BlockSpec((tm, d), lambda i: (jnp.minimum(i, nt - 1), 0)),
                  _const_spec((d, d_ff)), _const_spec((d, d_ff)), _const_spec((d_ff, d)),
                  _const_spec((1, d)), _const_spec((1, d))],
        out_specs=pl.BlockSpec((tm, d), lambda i: (jnp.maximum(i - 1, 0), 0)),
        out_shape=jax.ShapeDtypeStruct((n, d), _F32),
        scratch_shapes=[pltpu.VMEM((tm, d_ff), _BF16), pltpu.VMEM((tm, d), _F32)],
        compiler_params=pltpu.CompilerParams(dimension_semantics=("arbitrary",),
                                             vmem_limit_bytes=VMEM_LIMIT),
        name="ffn_ln",
    )(x, wg.astype(_BF16), wu.astype(_BF16), wd.astype(_BF16), g, b)


_QKV = 3 * GDN_WIDTH
_NAT_KV = _QKV + GDN_WIDTH + LANES
_T_VS = NSA_WIDTH
_T_VW = _T_VS + NSA_KV_WIDTH
_T_GATE = _T_VW + NSA_KV_WIDTH
_T_ROWS = _T_GATE + 32
Q_SCALE = NSA_HEAD_DIM ** -0.5 * 1.4426950408889634


def _proj_kernel(x_ref, xh_ref, wn_ref, wt_ref, cw_ref, qkv_ref, z_ref, small_ref, kc_ref, vc_ref,
                 ks_ref, kw_ref, qt_ref, vst_ref, vwt_ref, gt_ref, kvs_ref, *, qt, tiles_per_seq):
    hk_n, grp, dh = NSA_KV_HEADS, NSA_GROUP, NSA_HEAD_DIM
    xb = x_ref[...].astype(_BF16)
    tm = xb.shape[0]
    halo = xh_ref.shape[0]
    at_start = pl.program_id(0) % tiles_per_seq == 0
    xh = jnp.where(at_start, 0.0, xh_ref[...]).astype(_BF16)
    xcat = jnp.concatenate([xh, xb], axis=0)

    def conv_chunk(c0, width):
        cols = slice(c0, c0 + width)
        pre = jnp.dot(xcat, wn_ref[:, cols], preferred_element_type=_F32)
        conv = cw_ref[GDN_CONV - 1:GDN_CONV, cols] * pre[halo:, :]
        for d in range(1, GDN_CONV):
            conv = conv + cw_ref[GDN_CONV - 1 - d:GDN_CONV - d, cols] * pltpu.roll(pre, d, 0)[halo:, :]
        qkv_ref[:, cols] = _silu(conv)

    def nat_cols(ref, c0, width, r0=0):
        ref[:, r0:r0 + width] = jnp.dot(xb, wn_ref[:, c0:c0 + width], preferred_element_type=_F32)

    def kv_heads():
        kv = jnp.dot(xb, wn_ref[:, _NAT_KV:_NAT_KV + 4 * NSA_KV_WIDTH], preferred_element_type=_F32)
        for hk in range(hk_n):
            for idx, ref in enumerate((ks_ref, kw_ref)):
                c0 = (2 + idx) * NSA_KV_WIDTH + hk * dh
                ref[0, hk] = kv[:, c0:c0 + dh].astype(ref.dtype)
        stride = NSA_CMP_STRIDE
        for idx, ref in enumerate((kc_ref, vc_ref)):
            kvs_ref[idx] = kv[:, idx * NSA_KV_WIDTH:(idx + 1) * NSA_KV_WIDTH]
            for r in range(stride):
                rows = kvs_ref[idx, pl.ds(r, tm // stride, stride=stride), :]
                for hk in range(hk_n):
                    ref[0, hk, :, r * dh:(r + 1) * dh] = rows[:, hk * dh:(hk + 1) * dh]

    def q_heads(hk):
        r0 = hk * grp * dh
        yt = lax.dot_general(wt_ref[r0:r0 + grp * dh, :], xb, (((1,), (1,)), ((), ())),
                             preferred_element_type=_F32)
        for ii in range(tm // qt):
            for g in range(grp):
                qt_ref[0, hk, ii, :, g * qt:(g + 1) * qt] = (
                    yt[g * dh:(g + 1) * dh, ii * qt:(ii + 1) * qt] * Q_SCALE).astype(_BF16)

    def v_gates():
        yt = lax.dot_general(wt_ref[_T_VS:_T_ROWS, :], xb, (((1,), (1,)), ((), ())),
                             preferred_element_type=_F32)
        for hk in range(hk_n):
            for ii in range(tm // qt):
                cols = slice(ii * qt, (ii + 1) * qt)
                vst_ref[0, hk, ii] = yt[hk * dh:(hk + 1) * dh, cols].astype(_BF16)
                vwt_ref[0, hk, ii] = yt[NSA_KV_WIDTH + hk * dh:NSA_KV_WIDTH + (hk + 1) * dh,
                                        cols].astype(_BF16)
                for g in range(grp):
                    for c in range(3):
                        r = 2 * NSA_KV_WIDTH + (hk * grp + g) * 3 + c
                        gt_ref[0, hk, ii, c:c + 1, g * qt:(g + 1) * qt] = yt[r:r + 1, cols]

    step = 2 * LANES
    others = [lambda: nat_cols(z_ref, _QKV, step), lambda: nat_cols(z_ref, _QKV + step, step, step),
              lambda: (nat_cols(small_ref, _QKV + GDN_WIDTH, LANES), kv_heads()),
              lambda: q_heads(0), lambda: q_heads(1), v_gates]
    assert _QKV // step == len(others) and GDN_WIDTH == 2 * step and hk_n == 2
    for k, other in enumerate(others):
        conv_chunk(k * step, step)
        other()


def _proj_weights(w):
    d = w.shape[0]
    sizes = (GDN_WIDTH,) * 4 + (GDN_HEADS,) * 2 + (NSA_WIDTH,) + (NSA_KV_WIDTH,) * 6 + (3 * NSA_Q_HEADS,)
    off = np.concatenate([[0], np.cumsum(sizes)])
    gq, gk, gv, gz, gb, ga, nq, kc, vc, ks, vs, kw, vw, gate = [
        w[:, int(off[i]):int(off[i + 1])] for i in range(len(sizes))]
    small = jnp.concatenate([gb, ga, jnp.zeros((d, LANES - 2 * GDN_HEADS), w.dtype)], axis=1)
    w_nat = jnp.concatenate([gq, gk, gv, gz, small, kc, vc, ks, kw], axis=1)
    w_t = jnp.concatenate([nq, vs, vw, gate, jnp.zeros((d, _T_ROWS - _T_GATE - gate.shape[1]), w.dtype)],
                          axis=1).T
    return w_nat.astype(_BF16), w_t.astype(_BF16)


def _in_proj(x, w_nat, w_t, conv_w, bsz, t):
    n, d = x.shape
    hk, grp, dh = NSA_KV_HEADS, NSA_GROUP, NSA_HEAD_DIM
    tm = min(PROJ_ROWS, t)
    qt = min(NSA_QT, t)
    nt = t // tm
    halo = 16
    halo_spec = pl.BlockSpec((halo, d), lambda r: (jnp.maximum(r * (tm // halo) - 1, 0), 0))
    row = lambda width: pl.BlockSpec((tm, width), lambda r: (r, 0))
    head = pl.BlockSpec((1, hk, tm, dh), lambda r: (r // nt, 0, r % nt, 0))
    cs = NSA_CMP_STRIDE
    cmp_in = pl.BlockSpec((1, hk, tm // cs, cs * dh), lambda r: (r // nt, 0, r % nt, 0))
    tile = lambda rows, lanes: pl.BlockSpec((1, hk, tm // qt, rows, lanes),
                                            lambda r: (r // nt, 0, r % nt, 0, 0))
    sds = jax.ShapeDtypeStruct
    return pl.pallas_call(
        functools.partial(_proj_kernel, qt=qt, tiles_per_seq=nt),
        grid=(n // tm,),
        in_specs=[pl.BlockSpec((tm, d), lambda r: (r, 0)), halo_spec, _const_spec(w_nat.shape),
                  _const_spec(w_t.shape), _const_spec(conv_w.shape)],
        out_specs=[row(_QKV), row(GDN_WIDTH), row(LANES), cmp_in, cmp_in, head, head,
                   tile(dh, grp * qt), tile(dh, qt), tile(dh, qt), tile(3, grp * qt)],
        out_shape=[sds((n, _QKV), _F32), sds((n, GDN_WIDTH), _F32), sds((n, LANES), _F32),
                   sds((bsz, hk, t // cs, cs * dh), _F32), sds((bsz, hk, t // cs, cs * dh), _F32),
                   sds((bsz, hk, t, dh), _BF16), sds((bsz, hk, t, dh), _BF16),
                   sds((bsz, hk, t // qt, dh, grp * qt), _BF16),
                   sds((bsz, hk, t // qt, dh, qt), _BF16), sds((bsz, hk, t // qt, dh, qt), _BF16),
                   sds((bsz, hk, t // qt, 3, grp * qt), _F32)],
        scratch_shapes=[pltpu.VMEM((2, tm, NSA_KV_WIDTH), _F32)],
        compiler_params=pltpu.CompilerParams(dimension_semantics=("parallel",),
                                             vmem_limit_bytes=VMEM_LIMIT),
        name="in_proj",
    )(x, x, w_nat, w_t, conv_w)


def _bdot(a, b):
    return jnp.dot(a, b, preferred_element_type=_F32)


def _unit_lower_inverses(lmats):
    c = lmats[0].shape[0]
    eye = jnp.where(lax.broadcasted_iota(jnp.int32, (c, c), 0)
                    == lax.broadcasted_iota(jnp.int32, (c, c), 1), 1.0, 0.0)
    rs = [eye - l for l in lmats]
    pb = [l.astype(_BF16) for l in lmats]
    ps = [_bdot(l, l) for l in pb]
    span = 2
    while True:
        pb = [p.astype(_BF16) for p in ps]
        rs = [r + _bdot(r.astype(_BF16), p) for r, p in zip(rs, pb)]
        span *= 2
        if span >= c:
            return rs
        ps = [_bdot(p, p) for p in pb]


def _gdn_kernel(act_ref, z_ref, bac_ref, bar_ref, hpr_ref, hpc_ref, nw_ref, o_ref, s_ref, *, tb):
    nh, hd, c = GDN_HEADS, GDN_HEAD_DIM, GDN_CHUNK

    @pl.when(pl.program_id(1) == 0)
    def _():
        s_ref[...] = jnp.zeros_like(s_ref)

    row = lax.broadcasted_iota(jnp.int32, (c, c), 0)
    col = lax.broadcasted_iota(jnp.int32, (c, c), 1)
    causal = row >= col
    strict = row > col
    tril = jnp.where(causal, 1.0, 0.0).astype(_BF16)
    triu = jnp.where(row <= col, 1.0, 0.0).astype(_BF16)

    nchunk = tb // c
    nseq = act_ref.shape[0]
    qg_l, qb_l, kb_l, kbeta_l, rhs_l, kd_l, decay_l, eg_l = [], [], [], [], [], [], [], []
    for bi, ci in ((bi, ci) for bi in range(nseq) for ci in range(nchunk)):
        r0 = ci * c
        bac = bac_ref[bi, r0:r0 + c, :]
        beta_all = _sigmoid(bac)
        g_all = -jnp.exp(hpr_ref[0:1, :]) * _softplus(bac + hpr_ref[1:2, :])
        gc_all = sum(_bdot(tril, part) for part in _split3(g_all))
        bar = bar_ref[bi, ci]
        gr_all = -jnp.exp(hpc_ref[:, 0:1]) * _softplus(bar + hpc_ref[:, 1:2])
        gcr_all = sum(_bdot(part, triu) for part in _split3(gr_all))
        for h in range(nh):
            q = act_ref[bi, r0:r0 + c, h * hd:(h + 1) * hd]
            k = act_ref[bi, r0:r0 + c, (nh + h) * hd:(nh + h + 1) * hd]
            v = act_ref[bi, r0:r0 + c, (2 * nh + h) * hd:(2 * nh + h + 1) * hd]
            q = q * lax.rsqrt(jnp.sum(q * q, axis=-1, keepdims=True) + L2_EPS) * (hd ** -0.5)
            k = k * lax.rsqrt(jnp.sum(k * k, axis=-1, keepdims=True) + L2_EPS)
            beta = beta_all[:, h:h + 1]
            gcc = gc_all[:, nh + h:nh + h + 1]
            gcr = gcr_all[nh + h:nh + h + 1, :]
            glast = gc_all[c - 1:c, nh + h:nh + h + 1]
            decay_l.append(jnp.exp(jnp.where(causal, gcc - gcr, NEG_BIG)))
            kbeta = k * beta
            egc = jnp.exp(gcc)
            qg_l.append(q * egc)
            qb_l.append(q.astype(_BF16))
            kb_l.append(k.astype(_BF16))
            kbeta_l.append(kbeta.astype(_BF16))
            rhs_l.append(jnp.concatenate([kbeta * egc, v * beta], axis=1).astype(_BF16))
            kd_l.append((k * jnp.exp(glast - gcc)).astype(_BF16))
            eg_l.append(jnp.exp(glast))

    nt = (((1,), (1,)), ((), ()))
    tn = (((0,), (0,)), ((), ()))
    kk_l = [lax.dot_general(a, b, nt, preferred_element_type=_F32) for a, b in zip(kbeta_l, kb_l)]
    qk_l = [lax.dot_general(a, b, nt, preferred_element_type=_F32) for a, b in zip(qb_l, kb_l)]
    lmat_l = [jnp.where(strict, kk * d, 0.0) for kk, d in zip(kk_l, decay_l)]
    tinv_l = _unit_lower_inverses(lmat_l)
    wu_l = [_bdot(t.astype(_BF16), r).astype(_BF16) for t, r in zip(tinv_l, rhs_l)]
    qkb_l = [(qk * d).astype(_BF16) for qk, d in zip(qk_l, decay_l)]
    pn_l = [lax.dot_general(kd, wu, tn, preferred_element_type=_F32) for kd, wu in zip(kd_l, wu_l)]
    ab_l = [_bdot(qk, wu) for qk, wu in zip(qkb_l, wu_l)]
    a_l = [(qg - ab[:, :hd]).astype(_BF16) for qg, ab in zip(qg_l, ab_l)]
    p_l = [pn[:, :hd].astype(_BF16) for pn in pn_l]

    chains = [(bi, h) for bi in range(nseq) for h in range(nh)]
    states = [s_ref[bi * nh + h] for bi, h in chains]
    for ci in range(nchunk):
        r0 = ci * c
        idx = [(bi * nchunk + ci) * nh + h for bi, h in chains]
        sb = [s.astype(_BF16) for s in states]
        outs = [_bdot(a_l[e], sb[n]) + ab_l[e][:, hd:] for n, e in enumerate(idx)]
        states = [states[n] * eg_l[e] + pn_l[e][:, hd:] - _bdot(p_l[e], sb[n])
                  for n, e in enumerate(idx)]
        for n, (bi, h) in enumerate(chains):
            o = outs[n]
            zz = z_ref[bi, r0:r0 + c, h * hd:(h + 1) * hd]
            o = (o * lax.rsqrt(jnp.mean(o * o, axis=-1, keepdims=True) + RMS_EPS) * nw_ref[...]
                 * _silu(zz))
            o_ref[bi, r0:r0 + c, h * hd:(h + 1) * hd] = o
    for n, (bi, h) in enumerate(chains):
        s_ref[bi * nh + h] = states[n]


def _gdn(qkv, z, small, a_log, dt_bias, norm_w):
    bsz, t, width = qkv.shape
    nh, c = GDN_HEADS, GDN_CHUNK
    nseq = GDN_SEQS if bsz % GDN_SEQS == 0 else 1
    tb = min(GDN_ROWS, t)
    ba_rows = small[:, :, :8].reshape(bsz, t // c, c, 8).transpose(0, 1, 3, 2)
    hp_row = jnp.zeros((8, LANES), _F32)
    hp_row = hp_row.at[0, nh:2 * nh].set(a_log).at[1, nh:2 * nh].set(dt_bias)
    hp_col = jnp.zeros((8, LANES), _F32)
    hp_col = hp_col.at[nh:2 * nh, 0].set(a_log).at[nh:2 * nh, 1].set(dt_bias)
    return pl.pallas_call(
        functools.partial(_gdn_kernel, tb=tb),
        grid=(bsz // nseq, t // tb),
        in_specs=[pl.BlockSpec((nseq, tb, width), lambda b, i: (b, i, 0)),
                  pl.BlockSpec((nseq, tb, GDN_WIDTH), lambda b, i: (b, i, 0)),
                  pl.BlockSpec((nseq, tb, LANES), lambda b, i: (b, i, 0)),
                  pl.BlockSpec((nseq, tb // c, 8, c), lambda b, i: (b, i, 0, 0)),
                  pl.BlockSpec((8, LANES), lambda b, i: (0, 0)),
                  pl.BlockSpec((8, LANES), lambda b, i: (0, 0)),
                  pl.BlockSpec((1, GDN_HEAD_DIM), lambda b, i: (0, 0))],
        out_specs=pl.BlockSpec((nseq, tb, GDN_WIDTH), lambda b, i: (b, i, 0)),
        out_shape=jax.ShapeDtypeStruct((bsz, t, GDN_WIDTH), _F32),
        scratch_shapes=[pltpu.VMEM((nseq * nh, GDN_HEAD_DIM, GDN_HEAD_DIM), _F32)],
        compiler_params=pltpu.CompilerParams(dimension_semantics=("parallel", "arbitrary"),
                                             vmem_limit_bytes=VMEM_LIMIT),
        name="gdn",
    )(qkv, z, small, ba_rows, hp_row, hp_col, norm_w.reshape(1, -1))


def _gelu_tanh(x):
    return 0.5 * x * (1.0 + jnp.tanh(0.7978845608028654 * (x + 0.044715 * (x * x * x))))


def _cmp_kernel(xk_ref, xv_ref, pk_ref, pv_ref, wk1_ref, wk2_ref, wv1_ref, wv2t_ref, ok_ref, ovt_ref):
    def hidden(x, pos_ref, w1_ref):
        n = x.shape[0]
        ya = _mm(x + pos_ref[0:1, :], w1_ref[0])
        yb = _mm(x + pos_ref[1:2, :], w1_ref[1])
        hid = ya + pltpu.roll(yb, n - 1, 0)
        return _gelu_tanh(hid).astype(_BF16)

    ok_ref[0, 0] = _bdot(hidden(xk_ref[0, 0], pk_ref, wk1_ref), wk2_ref[...]).astype(_BF16)
    ovt_ref[0, 0] = lax.dot_general(wv2t_ref[...], hidden(xv_ref[0, 0], pv_ref, wv1_ref),
                                    (((1,), (1,)), ((), ())),
                                    preferred_element_type=_F32).astype(_BF16)


def _nsa_compress(xk, xv, pos_k, pos_v, k_w1, k_w2, v_w1, v_w2):
    bsz, hk, nrow, half = xk.shape
    dh = half // NSA_CMP_STRIDE
    hidden = k_w1.shape[1]
    x_spec = pl.BlockSpec((1, 1, nrow, half), lambda b, h: (b, h, 0, 0))
    full = lambda shape: pl.BlockSpec(shape, lambda b, h: (0,) * len(shape))
    return pl.pallas_call(
        _cmp_kernel,
        grid=(bsz, hk),
        in_specs=[x_spec, x_spec, full((2, half)), full((2, half)),
                  full((2, half, hidden)), full((hidden, dh)),
                  full((2, half, hidden)), full((dh, hidden))],
        out_specs=[pl.BlockSpec((1, 1, nrow, dh), lambda b, h: (b, h, 0, 0)),
                   pl.BlockSpec((1, 1, dh, nrow), lambda b, h: (b, h, 0, 0))],
        out_shape=[jax.ShapeDtypeStruct((bsz, hk, nrow, dh), _BF16),
                   jax.ShapeDtypeStruct((bsz, hk, dh, nrow), _BF16)],
        compiler_params=pltpu.CompilerParams(dimension_semantics=("parallel", "parallel"),
                                             vmem_limit_bytes=VMEM_LIMIT),
        name="nsa_cmp",
    )(xk, xv, pos_k.reshape(2, half), pos_v.reshape(2, half),
      k_w1.reshape(2, half, hidden).astype(_BF16), k_w2.astype(_BF16),
      v_w1.reshape(2, half, hidden).astype(_BF16), v_w2.T.astype(_BF16))


def _nsa_kernel(q_ref, kc_ref, vct_ref, cst_ref, ks_ref, vst_ref, kw_ref, vwt_ref, gt_ref, o_ref,
                kaug_ref, acc_ref, s_buf, p_buf, bias_ref, *, qt, kt, n_sel):
    assert qt == kt
    grp, dh = NSA_GROUP, NSA_HEAD_DIM
    heads = range(q_ref.shape[1])
    lanes = grp * qt
    t_all = ks_ref.shape[2]
    ns = t_all // NSA_SEL_LEN
    nc = kc_ref.shape[2]
    i = pl.program_id(1)
    s0 = i * qt

    @pl.when(i == 0)
    def _():
        rblk = lax.broadcasted_iota(jnp.int32, (t_all, ns), 0) // NSA_SEL_LEN
        cblk = lax.broadcasted_iota(jnp.int32, (t_all, ns), 1)
        onehot = jnp.where(rblk == cblk, 1.0, 0.0).astype(_BF16)
        for h in heads:
            kaug_ref[h, :, 0:dh] = ks_ref[0, h]
            kaug_ref[h, :, dh:dh + ns] = onehot
        krow = lax.broadcasted_iota(jnp.int32, (kt, lanes), 0)
        tl = lax.broadcasted_iota(jnp.int32, (kt, lanes), 1) & (qt - 1)
        far_off = ((NSA_WINDOW + kt - 1) // kt) * kt - NSA_WINDOW
        bias_ref[0] = jnp.where(krow <= tl, 0.0, NEG_BIG)
        bias_ref[1] = jnp.where(krow > tl + far_off, 0.0, NEG_BIG)
        bias_ref[2] = jnp.full((kt, lanes), NEG_BIG, _F32)

    lane = lax.broadcasted_iota(jnp.int32, (1, lanes), 1)
    tq = s0 + (lane & (qt - 1))
    qs = [q_ref[0, h, 0] for h in heads]

    cend = lax.broadcasted_iota(jnp.int32, (nc, 1), 0) * NSA_CMP_STRIDE + (NSA_CMP_LEN - 1)
    valid = cend <= tq
    sc = [jnp.dot(kc_ref[0, h], qs[h], preferred_element_type=_F32) for h in heads]
    pc = []
    for h in heads:
        scm = jnp.where(valid, sc[h], NEG_BIG)
        mc = jnp.max(scm, axis=0, keepdims=True)
        ec = jnp.where(valid, jnp.exp2(scm - mc), 0.0)
        lc = jnp.sum(ec, axis=0, keepdims=True)
        pc.append(ec / jnp.where(lc > 0.0, lc, 1.0))
    o_cmp = [jnp.dot(vct_ref[0, h], pc[h].astype(_BF16), preferred_element_type=_F32)
             for h in heads]

    cst = cst_ref[...]
    blk = lax.broadcasted_iota(jnp.int32, (ns, qt), 0)
    tq1 = s0 + lax.broadcasted_iota(jnp.int32, (ns, qt), 1)
    cur = tq1 // NSA_SEL_LEN
    visible = blk * NSA_SEL_LEN <= tq1
    forced = (blk == 0) | (blk == cur) | (blk == cur - 1)
    sub = lax.broadcasted_iota(jnp.int32, (8, qt), 0)
    qaug = []
    for h in heads:
        psum = pc[h][:, 0:qt]
        for g in range(1, grp):
            psum = psum + pc[h][:, g * qt:(g + 1) * qt]
        p_hi = psum.astype(_BF16)
        p_lo = (psum - p_hi.astype(_F32)).astype(_BF16)
        imp = (jnp.dot(cst, p_hi, preferred_element_type=_F32)
               + jnp.dot(cst, p_lo, preferred_element_type=_F32))
        imp = jnp.where(visible, jnp.where(forced, NSA_FORCE_SCORE, imp), -1.0)
        imp_rows = [imp[8 * r:8 * r + 8, :] for r in range(ns // 8)]
        ranks = [jnp.zeros((8, qt), _F32) for _ in imp_rows]
        for j in range(ns):
            rowj = imp[j:j + 1, :]
            for r, rows in enumerate(imp_rows):
                if 8 * r > j:
                    beats = jnp.where(rowj >= rows, 1.0, 0.0)
                elif 8 * r + 7 <= j:
                    beats = jnp.where(rowj > rows, 1.0, 0.0)
                else:
                    beats = jnp.where(sub + 8 * r > j, jnp.where(rowj >= rows, 1.0, 0.0),
                                      jnp.where(rowj > rows, 1.0, 0.0))
                ranks[r] = ranks[r] + beats
        rank = jnp.concatenate(ranks, axis=0)
        bias = jnp.where(rank < n_sel, 0.0, NEG_BIG).astype(_BF16)
        qaug.append(jnp.concatenate([qs[h], jnp.concatenate([bias] * grp, axis=1)], axis=0))

    m0 = jnp.full((1, lanes), M_INIT, _F32)
    ones_rows = jnp.where(lax.broadcasted_iota(jnp.int32, (acc_ref.shape[1] - dh, kt), 0) == 0,
                          1.0, 0.0).astype(_BF16)

    def v_aug(v_tile):
        return jnp.concatenate([v_tile, ones_rows], axis=0)

    nwt = (NSA_WINDOW + kt - 1) // kt + 1
    half = lanes // 2
    o_win = []
    for h in heads:
        halves = []
        for l0 in (0, half):
            ls = slice(l0, l0 + half)
            s_win, j_win = [], []
            for d in range(nwt):
                back = nwt - 1 - d
                j = i - back
                jc = jnp.maximum(j, 0)
                k_tile = kw_ref[0, h, pl.ds(pl.multiple_of(jc * kt, kt), kt), :]
                s = jnp.dot(k_tile, qs[h][:, ls], preferred_element_type=_F32)
                if back == 0:
                    s = s + bias_ref[0, :, ls]
                elif back == nwt - 1:
                    s = s + bias_ref[jnp.where(j >= 0, 1, 2), :, ls]
                else:
                    s = s + jnp.where(j >= 0, 0.0, NEG_BIG)
                s_win.append(s)
                j_win.append(jc)
            mw = s_win[0].max(axis=0, keepdims=True)
            for s in s_win[1:]:
                mw = jnp.maximum(mw, s.max(axis=0, keepdims=True))
            acc_w = jnp.zeros((acc_ref.shape[1], half), _F32)
            for s, jc in zip(s_win, j_win):
                acc_w = acc_w + jnp.dot(v_aug(vwt_ref[0, h, jc]), jnp.exp2((s - mw).astype(_BF16)),
                                        preferred_element_type=_F32)
            halves.append(acc_w[0:dh] / acc_w[dh:dh + 1])
        o_win.append(jnp.concatenate(halves, axis=1))

    n_s, n_p = s_buf.shape[1], p_buf.shape[1]
    last_tile = t_all // kt - 1

    def qk_tile(h, j):
        start = pl.multiple_of(jnp.minimum(j, last_tile) * kt, kt)
        return jnp.dot(kaug_ref[h, pl.ds(start, kt), :], qaug[h], preferred_element_type=_F32)

    def softmax_step(s, m):
        m_new = jnp.maximum(m, jnp.max(s, axis=0, keepdims=True))
        return m_new, jnp.exp2(m - m_new), jnp.exp2((s - m_new).astype(_BF16))

    acc_ref[...] = jnp.zeros_like(acc_ref)
    for h in heads:
        p_buf[h, n_p - 1] = jnp.zeros((kt, lanes), _BF16)
        s_buf[h, 0] = qk_tile(h, 0)
        s_buf[h, 1] = qk_tile(h, 1)

    def sel_step(j, c, carry):
        for h in heads:
            s_buf[h, (c + 2) % n_s] = qk_tile(h, j + 2)
        pv = [jnp.dot(v_aug(vst_ref[0, h, jnp.maximum(j - 1, 0)]), p_buf[h, (c - 1) % n_p],
                      preferred_element_type=_F32) for h in heads]
        out = []
        for h in heads:
            m, a_prev = carry[h]
            m, a, p = softmax_step(s_buf[h, c], m)
            p_buf[h, c % n_p] = p
            acc_ref[h] = a_prev * acc_ref[h] + pv[h]
            out.append((m, a))
        return tuple(out)

    def sel_round(jj, carry):
        for c in range(n_s):
            carry = sel_step(n_s * jj + c, c, carry)
        return carry

    carry = lax.fori_loop(0, i // n_s, sel_round,
                          tuple((m0, jnp.ones((1, lanes), _F32)) for _ in heads))
    base = (i // n_s) * n_s
    rem = i - base
    carry = lax.cond(rem >= 2, lambda x: sel_step(base + 1, 1, sel_step(base, 0, x)), lambda x: x,
                     carry)
    carry = lax.cond(rem == 1, functools.partial(sel_step, base, 0), lambda x: x, carry)
    carry = lax.cond(rem == 3, functools.partial(sel_step, base + 2, 2), lambda x: x, carry)
    out_rows = []
    for h in heads:
        m, a_prev = carry[h]
        pv = jnp.dot(v_aug(vst_ref[0, h, jnp.maximum(i - 1, 0)]), p_buf[h, (i + n_p - 1) % n_p],
                     preferred_element_type=_F32)
        acc = a_prev * acc_ref[h] + pv
        s = s_buf[h, i % n_s] + bias_ref[0]
        m, a, p = softmax_step(s, m)
        acc = a * acc + jnp.dot(v_aug(vst_ref[0, h, i]), p, preferred_element_type=_F32)
        o_sel = acc[0:dh] / acc[dh:dh + 1]
        gate = _sigmoid(gt_ref[0, h, 0])
        out_t = gate[0:1, :] * o_cmp[h] + gate[1:2, :] * o_sel + gate[2:3, :] * o_win[h]
        out_rows += [out_t[:, g * qt:(g + 1) * qt] for g in range(grp)]
    o_ref[0] = jnp.concatenate(out_rows, axis=0).T


def _nsa_attention(qt_arr, k_cmp, v_cmp_t, cst, ks, vs_t, kw, vw_t, gates_t):
    bsz, hk, nq, dh, lanes = qt_arr.shape
    qt = lanes // NSA_GROUP
    kt = qt
    t = ks.shape[2]
    ns = t // NSA_SEL_LEN
    nc = k_cmp.shape[2]
    ntile = t // kt
    per_seq = lambda shape: pl.BlockSpec((1, hk) + shape, lambda b, i: (b, 0) + (0,) * len(shape))
    per_tile = lambda shape: pl.BlockSpec((1, hk, 1) + shape,
                                          lambda b, i: (b, 0, i) + (0,) * len(shape))
    return pl.pallas_call(
        functools.partial(_nsa_kernel, qt=qt, kt=kt, n_sel=min(NSA_TOPN, ns)),
        grid=(bsz, nq),
        in_specs=[per_tile((dh, lanes)),
                  per_seq((nc, dh)), per_seq((dh, nc)),
                  pl.BlockSpec((ns, nc), lambda b, i: (0, 0)),
                  per_seq((t, dh)), per_seq((ntile, dh, kt)),
                  per_seq((t, dh)), per_seq((ntile, dh, kt)),
                  per_tile((3, lanes))],
        out_specs=pl.BlockSpec((1, qt, hk * NSA_GROUP * dh), lambda b, i: (b, i, 0)),
        out_shape=jax.ShapeDtypeStruct((bsz, t, hk * NSA_GROUP * dh), _F32),
        scratch_shapes=[pltpu.VMEM((hk, t, dh + ns), _BF16),
                        pltpu.VMEM((hk, dh + 16, lanes), _F32),
                        pltpu.VMEM((hk, 4, kt, lanes), _F32),
                        pltpu.VMEM((hk, 2, kt, lanes), _BF16),
                        pltpu.VMEM((3, kt, lanes), _F32)],
        compiler_params=pltpu.CompilerParams(
            dimension_semantics=("parallel", "arbitrary"),
            vmem_limit_bytes=VMEM_LIMIT),
        name="nsa_attn",
    )(qt_arr, k_cmp, v_cmp_t, cst, ks, vs_t, kw, vw_t, gates_t)


def _cmp_to_sel_t(t):
    ncmp = (t - NSA_CMP_LEN) // NSA_CMP_STRIDE + 1
    c_start = np.arange(ncmp) * NSA_CMP_STRIDE
    s_start = np.arange(t // NSA_SEL_LEN) * NSA_SEL_LEN
    overlap = np.clip(np.minimum(c_start[:, None] + NSA_CMP_LEN, s_start[None, :] + NSA_SEL_LEN)
                      - np.maximum(c_start[:, None], s_start[None, :]), 0, None) / NSA_CMP_LEN
    out = np.zeros((t // NSA_SEL_LEN, t // NSA_CMP_STRIDE), np.float32)
    out[:, :ncmp] = overlap.T
    return jnp.asarray(out, _BF16)


def _nsa(q_t, kc, vc, ks, kw, vs_t, vw_t, gates_t, pos_k, pos_v, k_w1, k_w2, v_w1, v_w2):
    k_cmp, v_cmp_t = _nsa_compress(kc, vc, pos_k, pos_v, k_w1, k_w2, v_w1, v_w2)
    return _nsa_attention(q_t, k_cmp, v_cmp_t, _cmp_to_sel_t(ks.shape[2]), ks, vs_t, kw, vw_t, gates_t)


def _out_ffn_kernel(x_ref, a_ref, b_ref, wa_ref, wb_ref, g2_ref, b2_ref, wg_ref, wu_ref, wd_ref,
                    g3_ref, b3_ref, o_ref, h_ref, y_ref):
    @pl.when(pl.program_id(0) == 0)
    def _():
        y_ref[...] = jnp.zeros_like(y_ref)

    ln3 = _layer_norm_pieces(y_ref, g3_ref, b3_ref, o_ref, 8)
    mix = (jnp.dot(a_ref[...].astype(_BF16), wa_ref[...], preferred_element_type=_F32)
           + jnp.dot(b_ref[...].astype(_BF16), wb_ref[...], preferred_element_type=_F32))
    x2 = _layer_norm(DN_ALPHA * x_ref[...] + mix, g2_ref[...], b2_ref[...])
    y_ref[...] = DN_ALPHA * x2 + 0.5 * _swiglu(x2.astype(_BF16), wg_ref, wu_ref, wd_ref, h_ref,
                                               after=ln3)


def _out_ffn(x, oa, ob, wa, wb, g2, b2, wg, wu, wd, g3, b3):
    n, d = x.shape
    d_ff = wg.shape[1]
    tm = min(FFN_ROWS, n)
    nt = n // tm
    row = lambda width: pl.BlockSpec((tm, width), lambda i: (jnp.minimum(i, nt - 1), 0))
    return pl.pallas_call(
        _out_ffn_kernel,
        grid=(nt + 1,),
        in_specs=[row(d), row(oa.shape[1]), row(ob.shape[1]), _const_spec(wa.shape),
                  _const_spec(wb.shape), _const_spec((1, d)), _const_spec((1, d)),
                  _const_spec((d, d_ff)), _const_spec((d, d_ff)), _const_spec((d_ff, d)),
                  _const_spec((1, d)), _const_spec((1, d))],
        out_specs=pl.BlockSpec((tm, d), lambda i: (jnp.maximum(i - 1, 0), 0)),
        out_shape=jax.ShapeDtypeStruct((n, d), _F32),
        scratch_shapes=[pltpu.VMEM((tm, d_ff), _BF16), pltpu.VMEM((tm, d), _F32)],
        compiler_params=pltpu.CompilerParams(dimension_semantics=("arbitrary",),
                                             vmem_limit_bytes=VMEM_LIMIT),
        name="out_ffn",
    )(x, oa, ob, wa, wb, g2, b2, wg.astype(_BF16), wu.astype(_BF16), wd.astype(_BF16), g3, b3)


def kernel(x, ln1_g, ln1_b, ffn1_wg, ffn1_wu, ffn1_wd, w_in, gdn_conv_w, gdn_a_log, gdn_dt_bias,
           gdn_norm_w, nsa_cmp_pos_k, nsa_cmp_pos_v, nsa_cmp_k_w1, nsa_cmp_k_w2, nsa_cmp_v_w1,
           nsa_cmp_v_w2, w_out, ln2_g, ln2_b, ffn2_wg, ffn2_wu, ffn2_wd, ln3_g, ln3_b):
    bsz, t, d = x.shape
    n = bsz * t
    h = x.reshape(n, d)
    for i in range(DEPTH):
        h = _ffn_ln(h, ffn1_wg[i], ffn1_wu[i], ffn1_wd[i], ln1_g[i].reshape(1, d), ln1_b[i].reshape(1, d))
        w_nat, w_t = _proj_weights(w_in[i])
        qkv, z, small, kc, vc, ks, kw, q_t, vs_t, vw_t, gates_t = _in_proj(
            h, w_nat, w_t, gdn_conv_w[i], bsz, t)
        o_gdn = _gdn(qkv.reshape(bsz, t, _QKV), z.reshape(bsz, t, GDN_WIDTH),
                     small.reshape(bsz, t, LANES), gdn_a_log[i], gdn_dt_bias[i], gdn_norm_w[i])
        o_nsa = _nsa(q_t, kc, vc, ks, kw, vs_t, vw_t, gates_t,
                     nsa_cmp_pos_k[i], nsa_cmp_pos_v[i], nsa_cmp_k_w1[i], nsa_cmp_k_w2[i],
                     nsa_cmp_v_w1[i], nsa_cmp_v_w2[i])
        wo = w_out[i].astype(_BF16)
        h = _out_ffn(h, o_gdn.reshape(n, GDN_WIDTH), o_nsa.reshape(n, NSA_WIDTH),
                     wo[:GDN_WIDTH], wo[GDN_WIDTH:], ln2_g[i].reshape(1, d), ln2_b[i].reshape(1, d),
                     ffn2_wg[i], ffn2_wu[i], ffn2_wd[i], ln3_g[i].reshape(1, d), ln3_b[i].reshape(1, d))
    return h.reshape(bsz, t, d)
```
